```python
import jax, jax.numpy as jnp
from jax import lax
import numpy as np

D_MODEL = 1024
BATCH = 2
SEQ = 16384
DEPTH = 1
DEC_BATCH = 4
DEC_SEQ = 8192
PAST_LEN = 128

D_CONV = 512
CONV_WIDTH = 3
N_HEADS = 8
QK_NOPE = 64
QK_ROPE = 32
V_HEAD = 64
Q_LORA = 256
KV_LORA = 128
QK_HEAD = QK_NOPE + QK_ROPE
ROPE_THETA = 10000.0
Q_BLOCK = 128
PEER_HEADS = 8
N_KEYS = 128
N_EXPERTS = N_KEYS * N_KEYS
PEER_DK = 128
PEER_TOPK = 16
PEER_CHUNK = 128
EPS = 1e-6
IN_COLS = 3 * D_CONV + Q_LORA + KV_LORA + QK_ROPE + 2 * D_MODEL

kernel_name = "hybrid_conv_mla_peer_encoder"


def rmsnorm(x, g):
    xf = x.astype(jnp.float32)
    y = xf * lax.rsqrt(jnp.mean(xf * xf, axis=-1, keepdims=True) + EPS)
    return (y * g.astype(jnp.float32)).astype(x.dtype)


def rotary_tables(S, dtype):
    pos = jnp.arange(S, dtype=jnp.float32)
    inv = ROPE_THETA ** (-jnp.arange(0, QK_ROPE, 2, dtype=jnp.float32) / QK_ROPE)
    ang = pos[:, None] * inv[None, :]
    return jnp.cos(ang).astype(dtype), jnp.sin(ang).astype(dtype)


def apply_rope(t, cos, sin):
    c = cos[None, :, None, :]
    s = sin[None, :, None, :]
    t1, t2 = jnp.split(t, 2, axis=-1)
    return jnp.concatenate([t1 * c - t2 * s, t1 * s + t2 * c], axis=-1)


def block_attention(q, k, v):
    B, S, H, Dh = q.shape
    nb = S // Q_BLOCK
    qb = q.reshape(B, nb, Q_BLOCK, H, Dh).transpose(1, 0, 2, 3, 4)

    def one_block(qi):
        s = jnp.einsum('bqhd,bkhd->bhqk', qi, k, preferred_element_type=jnp.float32)
        p = jax.nn.softmax(s, axis=-1).astype(v.dtype)
        return jnp.einsum('bhqk,bkhd->bqhd', p, v)

    out = lax.map(one_block, qb)
    return out.transpose(1, 0, 2, 3, 4).reshape(B, S, H * V_HEAD)


def peer(xn, w_pq, k1, k2, u_tab, v_tab):
    B, S, D = xn.shape
    xt = xn.reshape((B * S) // PEER_CHUNK, PEER_CHUNK, D)
    half = PEER_DK // 2

    def one_chunk(xc):
        q = (xc @ w_pq).reshape(PEER_CHUNK, PEER_HEADS, PEER_DK)
        s1 = jnp.einsum('chd,hnd->chn', q[..., :half], k1)
        s2 = jnp.einsum('chd,hnd->chn', q[..., half:], k2)
        v1, i1 = lax.top_k(s1, PEER_TOPK)
        v2, i2 = lax.top_k(s2, PEER_TOPK)
        comb = (v1[..., :, None] + v2[..., None, :]).reshape(PEER_CHUNK, PEER_HEADS, PEER_TOPK * PEER_TOPK)
        sc, pos = lax.top_k(comb, PEER_TOPK)
        ia = pos // PEER_TOPK
        ib = pos % PEER_TOPK
        e = (jnp.take_along_axis(i1, ia, axis=-1) * N_KEYS
             + jnp.take_along_axis(i2, ib, axis=-1))
        g = jax.nn.softmax(sc.astype(jnp.float32), axis=-1).astype(xc.dtype)
        a = jax.nn.gelu(jnp.einsum('chkd,cd->chk', u_tab[e], xc))
        return jnp.einsum('chk,chkd->cd', g * a, v_tab[e])

    return lax.map(one_chunk, xt).reshape(B, S, D)


def encoder_layer(x, c, w_ada, b_ada, g_norm1, w_in, conv_w, w_conv_out, g_q_lora, w_uq,
                  g_kv_lora, w_ukv, g_qnorm, g_knorm, w_attn_out, w_out, g_norm2,
                  peer_wq, peer_k1, peer_k2, peer_u, peer_v):
    B, S, D = x.shape
    ada = (c @ w_ada + b_ada)[:, None, :]
    shift1, scale1, gate1, shift2, scale2, gate2 = jnp.split(ada, 6, axis=-1)

    h = rmsnorm(x, g_norm1) * (1.0 + scale1) + shift1
    proj = h @ w_in
    o1 = 3 * D_CONV
    o2 = o1 + Q_LORA
    o3 = o2 + KV_LORA
    o4 = o3 + QK_ROPE
    o5 = o4 + D_MODEL
    b_g, c_g, hc, cq, ckv, kr, ga, gb = jnp.split(
        proj, [D_CONV, 2 * D_CONV, o1, o2, o3, o4, o5], axis=-1)

    z = c_g * hc
    zp = jnp.pad(z, ((0, 0), (1, 1), (0, 0)))
    y = zp[:, :-2] * conv_w[0] + zp[:, 1:-1] * conv_w[1] + zp[:, 2:] * conv_w[2]
    out_a = (b_g * y) @ w_conv_out

    q = (rmsnorm(cq, g_q_lora) @ w_uq).reshape(B, S, N_HEADS, QK_HEAD)
    kv = (rmsnorm(ckv, g_kv_lora) @ w_ukv).reshape(B, S, N_HEADS, QK_NOPE + V_HEAD)
    k_nope, v = kv[..., :QK_NOPE], kv[..., QK_NOPE:]
    k_r = jnp.broadcast_to(kr[:, :, None, :], (B, S, N_HEADS, QK_ROPE))
    k = jnp.concatenate([k_nope, k_r], axis=-1)
    q = rmsnorm(q, g_qnorm)
    k = rmsnorm(k, g_knorm)
    cos, sin = rotary_tables(S, x.dtype)
    q = jnp.concatenate([q[..., :QK_NOPE], apply_rope(q[..., QK_NOPE:], cos, sin)], axis=-1)
    k = jnp.concatenate([k[..., :QK_NOPE], apply_rope(k[..., QK_NOPE:], cos, sin)], axis=-1)
    out_b = block_attention(q * (QK_HEAD ** -0.5), k, v) @ w_attn_out

    merged = jax.nn.sigmoid(ga) * out_a + jax.nn.sigmoid(gb) * out_b
    x = x + gate1 * (merged @ w_out)

    h2 = rmsnorm(x, g_norm2) * (1.0 + scale2) + shift2
    x = x + gate2 * peer(h2, peer_wq, peer_k1, peer_k2, peer_u, peer_v)
    return x


def setup_inputs(seed: int = 0) -> dict:
    key = jax.random.key(seed)
    ks = jax.random.split(key, 32)
    f32 = jnp.float32

    def nrm(k, shape, scale):
        return jax.random.normal(k, shape, dtype=f32) * scale

    def gain(k, n):
        return 1.0 + 0.05 * jax.random.normal(k, (DEPTH, n), dtype=f32)

    L = DEPTH
    return {
        "x_prompt": nrm(ks[0], (BATCH, SEQ, D_MODEL), 1.0),
        "x_sample": nrm(ks[1], (DEC_BATCH, DEC_SEQ, D_MODEL), 1.0),
        "c_prompt": nrm(ks[2], (BATCH, D_MODEL), 1.0),
        "c_sample": nrm(ks[3], (DEC_BATCH, D_MODEL), 1.0),
        "w_ada": nrm(ks[4], (L, D_MODEL, 6 * D_MODEL), 0.3 * D_MODEL ** -0.5),
        "b_ada": nrm(ks[5], (L, 6 * D_MODEL), 0.02),
        "g_norm1": gain(ks[6], D_MODEL),
        "w_in": nrm(ks[7], (L, D_MODEL, IN_COLS), D_MODEL ** -0.5),
        "conv_w": nrm(ks[8], (L, CONV_WIDTH, D_CONV), CONV_WIDTH ** -0.5),
        "w_conv_out": nrm(ks[9], (L, D_CONV, D_MODEL), D_CONV ** -0.5),
        "g_q_lora": gain(ks[10], Q_LORA),
        "w_uq": nrm(ks[11], (L, Q_LORA, N_HEADS * QK_HEAD), Q_LORA ** -0.5),
        "g_kv_lora": gain(ks[12], KV_LORA),
        "w_ukv": nrm(ks[13], (L, KV_LORA, N_HEADS * (QK_NOPE + V_HEAD)), KV_LORA ** -0.5),
        "g_qnorm": gain(ks[14], QK_HEAD),
        "g_knorm": gain(ks[15], QK_HEAD),
        "w_attn_out": nrm(ks[16], (L, N_HEADS * V_HEAD, D_MODEL), (N_HEADS * V_HEAD) ** -0.5),
        "w_out": nrm(ks[17], (L, D_MODEL, D_MODEL), D_MODEL ** -0.5),
        "g_norm2": gain(ks[18], D_MODEL),
        "peer_wq": nrm(ks[19], (L, D_MODEL, PEER_HEADS * PEER_DK), D_MODEL ** -0.5),
        "peer_k1": nrm(ks[20], (L, PEER_HEADS, N_KEYS, PEER_DK // 2), (PEER_DK // 2) ** -0.5),
        "peer_k2": nrm(ks[21], (L, PEER_HEADS, N_KEYS, PEER_DK // 2), (PEER_DK // 2) ** -0.5),
        "peer_u": nrm(ks[22], (L, N_EXPERTS, D_MODEL), D_MODEL ** -0.5),
        "peer_v": nrm(ks[23], (L, N_EXPERTS, D_MODEL), PEER_HEADS ** -0.5),
    }


def reference(x_prompt, x_sample, c_prompt, c_sample, w_ada, b_ada, g_norm1, w_in, conv_w,
              w_conv_out, g_q_lora, w_uq, g_kv_lora, w_ukv, g_qnorm, g_knorm, w_attn_out,
              w_out, g_norm2, peer_wq, peer_k1, peer_k2, peer_u, peer_v):
    layer_params = (w_ada, b_ada, g_norm1, w_in, conv_w, w_conv_out, g_q_lora, w_uq,
                    g_kv_lora, w_ukv, g_qnorm, g_knorm, w_attn_out, w_out, g_norm2,
                    peer_wq, peer_k1, peer_k2, peer_u, peer_v)
    y_prompt = x_prompt
    y_sample = x_sample
    for l in range(DEPTH):
        p_l = [p[l] for p in layer_params]
        y_prompt = encoder_layer(y_prompt, c_prompt, *p_l)
        y_sample = encoder_layer(y_sample, c_sample, *p_l)
    return (y_prompt, y_sample)
```

```python
import jax
import jax.numpy as jnp
from jax import lax
from jax.experimental import pallas as pl

D_MODEL = 1024
D_CONV = 512
N_HEADS = 8
QK_NOPE = 64
QK_ROPE = 32
V_HEAD = 64
Q_LORA = 256
KV_LORA = 128
QK_HEAD = QK_NOPE + QK_ROPE
ROPE_THETA = 10000.0
Q_BLOCK = 128
PEER_HEADS = 8
N_KEYS = 128
PEER_DK = 128
PEER_TOPK = 16
PEER_CHUNK = 128
EPS = 1e-6


def _rmsnorm(x, g):
    xf = x.astype(jnp.float32)
    y = xf * lax.rsqrt(jnp.mean(xf * xf, axis=-1, keepdims=True) + EPS)
    return (y * g.astype(jnp.float32)).astype(x.dtype)


def _rotary_tables(S, dtype):
    pos = jnp.arange(S, dtype=jnp.float32)
    inv = ROPE_THETA ** (-jnp.arange(0, QK_ROPE, 2, dtype=jnp.float32) / QK_ROPE)
    ang = pos[:, None] * inv[None, :]
    return jnp.cos(ang).astype(dtype), jnp.sin(ang).astype(dtype)


def _apply_rope(t, cos, sin):
    c = cos[None, :, None, :]
    s = sin[None, :, None, :]
    t1, t2 = jnp.split(t, 2, axis=-1)
    return jnp.concatenate([t1 * c - t2 * s, t1 * s + t2 * c], axis=-1)


def _block_attention(q, k, v):
    B, S, H, Dh = q.shape
    nb = S // Q_BLOCK
    qb = q.reshape(B, nb, Q_BLOCK, H, Dh).transpose(1, 0, 2, 3, 4)

    def one_block(qi):
        s = jnp.einsum('bqhd,bkhd->bhqk', qi, k, preferred_element_type=jnp.float32)
        p = jax.nn.softmax(s, axis=-1).astype(v.dtype)
        return jnp.einsum('bhqk,bkhd->bqhd', p, v)

    out = lax.map(one_block, qb)
    return out.transpose(1, 0, 2, 3, 4).reshape(B, S, H * V_HEAD)


def _peer(xn, w_pq, k1, k2, u_tab, v_tab):
    B, S, D = xn.shape
    xt = xn.reshape((B * S) // PEER_CHUNK, PEER_CHUNK, D)
    half = PEER_DK // 2

    def one_chunk(xc):
        q = (xc @ w_pq).reshape(PEER_CHUNK, PEER_HEADS, PEER_DK)
        s1 = jnp.einsum('chd,hnd->chn', q[..., :half], k1)
        s2 = jnp.einsum('chd,hnd->chn', q[..., half:], k2)
        v1, i1 = lax.top_k(s1, PEER_TOPK)
        v2, i2 = lax.top_k(s2, PEER_TOPK)
        comb = (v1[..., :, None] + v2[..., None, :]).reshape(PEER_CHUNK, PEER_HEADS, PEER_TOPK * PEER_TOPK)
        sc, pos = lax.top_k(comb, PEER_TOPK)
        ia = pos // PEER_TOPK
        ib = pos % PEER_TOPK
        e = (jnp.take_along_axis(i1, ia, axis=-1) * N_KEYS
             + jnp.take_along_axis(i2, ib, axis=-1))
        g = jax.nn.softmax(sc.astype(jnp.float32), axis=-1).astype(xc.dtype)
        a = jax.nn.gelu(jnp.einsum('chkd,cd->chk', u_tab[e], xc))
        return jnp.einsum('chk,chkd->cd', g * a, v_tab[e])

    return lax.map(one_chunk, xt).reshape(B, S, D)


def _add_kernel(a_ref, b_ref, o_ref):
    o_ref[...] = a_ref[...] + b_ref[...]


def _pallas_add(a, b):
    n, d = a.shape
    tm = 1024
    return pl.pallas_call(
        _add_kernel,
        grid=(n // tm,),
        in_specs=[pl.BlockSpec((tm, d), lambda i: (i, 0)),
                  pl.BlockSpec((tm, d), lambda i: (i, 0))],
        out_specs=pl.BlockSpec((tm, d), lambda i: (i, 0)),
        out_shape=jax.ShapeDtypeStruct((n, d), a.dtype),
    )(a, b)


def _encoder_layer(x, c, w_ada, b_ada, g_norm1, w_in, conv_w, w_conv_out, g_q_lora, w_uq,
                   g_kv_lora, w_ukv, g_qnorm, g_knorm, w_attn_out, w_out, g_norm2,
                   peer_wq, peer_k1, peer_k2, peer_u, peer_v):
    B, S, D = x.shape
    ada = (c @ w_ada + b_ada)[:, None, :]
    shift1, scale1, gate1, shift2, scale2, gate2 = jnp.split(ada, 6, axis=-1)
    h = _rmsnorm(x, g_norm1) * (1.0 + scale1) + shift1
    proj = h @ w_in
    o1 = 3 * D_CONV
    o2 = o1 + Q_LORA
    o3 = o2 + KV_LORA
    o4 = o3 + QK_ROPE
    o5 = o4 + D_MODEL
    b_g, c_g, hc, cq, ckv, kr, ga, gb = jnp.split(
        proj, [D_CONV, 2 * D_CONV, o1, o2, o3, o4, o5], axis=-1)
    z = c_g * hc
    zp = jnp.pad(z, ((0, 0), (1, 1), (0, 0)))
    y = zp[:, :-2] * conv_w[0] + zp[:, 1:-1] * conv_w[1] + zp[:, 2:] * conv_w[2]
    out_a = (b_g * y) @ w_conv_out
    q = (_rmsnorm(cq, g_q_lora) @ w_uq).reshape(B, S, N_HEADS, QK_HEAD)
    kv = (_rmsnorm(ckv, g_kv_lora) @ w_ukv).reshape(B, S, N_HEADS, QK_NOPE + V_HEAD)
    k_nope, v = kv[..., :QK_NOPE], kv[..., QK_NOPE:]
    k_r = jnp.broadcast_to(kr[:, :, None, :], (B, S, N_HEADS, QK_ROPE))
    k = jnp.concatenate([k_nope, k_r], axis=-1)
    q = _rmsnorm(q, g_qnorm)
    k = _rmsnorm(k, g_knorm)
    cos, sin = _rotary_tables(S, x.dtype)
    q = jnp.concatenate([q[..., :QK_NOPE], _apply_rope(q[..., QK_NOPE:], cos, sin)], axis=-1)
    k = jnp.concatenate([k[..., :QK_NOPE], _apply_rope(k[..., QK_NOPE:], cos, sin)], axis=-1)
    out_b = _block_attention(q * (QK_HEAD ** -0.5), k, v) @ w_attn_out
    merged = jax.nn.sigmoid(ga) * out_a + jax.nn.sigmoid(gb) * out_b
    x = x + gate1 * (merged @ w_out)
    h2 = _rmsnorm(x, g_norm2) * (1.0 + scale2) + shift2
    upd = gate2 * _peer(h2, peer_wq, peer_k1, peer_k2, peer_u, peer_v)
    return _pallas_add(x.reshape(B * S, D), upd.reshape(B * S, D)).reshape(B, S, D)


def kernel(x_prompt, x_sample, c_prompt, c_sample, w_ada, b_ada, g_norm1, w_in, conv_w, w_conv_out, g_q_lora, w_uq, g_kv_lora, w_ukv, g_qnorm, g_knorm, w_attn_out, w_out, g_norm2, peer_wq, peer_k1, peer_k2, peer_u, peer_v):
    params = (w_ada, b_ada, g_norm1, w_in, conv_w, w_conv_out, g_q_lora, w_uq,
              g_kv_lora, w_ukv, g_qnorm, g_knorm, w_attn_out, w_out, g_norm2,
              peer_wq, peer_k1, peer_k2, peer_u, peer_v)
    p0 = [p[0] for p in params]
    return (_encoder_layer(x_prompt, c_prompt, *p0), _encoder_layer(x_sample, c_sample, *p0))
```

```python
import functools
import math

import jax
import jax.numpy as jnp
from jax import lax
from jax.experimental import pallas as pl
from jax.experimental.pallas import tpu as pltpu

D_MODEL = 1024
D_CONV = 512
N_HEADS = 8
QK_NOPE = 64
QK_ROPE = 32
V_HEAD = 64
Q_LORA = 256
KV_LORA = 128
QK_HEAD = QK_NOPE + QK_ROPE
ROPE_THETA = 10000.0
PEER_HEADS = 8
N_KEYS = 128
PEER_DK = 128
PEER_TOPK = 16
EPS = 1e-6

LANES = 128
SUBLANES = 8
HEAD_SLOT = LANES
HALF_ROPE = QK_ROPE // 2
N_PICKS = PEER_HEADS * PEER_TOPK
ROW_WORDS = D_MODEL // 2
ROW_SUB = ROW_WORDS // LANES
VMEM_LIMIT = 56 * 1024 * 1024

C_BG, C_CG, C_HC = 0, 512, 1024
C_CQ = 1536
C_CKV = C_CQ + Q_LORA
C_KR = C_CKV + KV_LORA
C_GA = C_KR + HEAD_SLOT
C_GB = C_GA + D_MODEL
IN_COLS_R = C_GB + D_MODEL

TS_IN = 256
TQ = 512
TK = 512
TS_MIX = 256
TT_PEER = 64

_NEG_INF = float("-inf")


def _cparams(sem):
    return pltpu.CompilerParams(dimension_semantics=sem, vmem_limit_bytes=VMEM_LIMIT)


def _ada_kernel(c_ref, w_ref, b_ref, o_ref):
    o_ref[...] = jnp.dot(c_ref[...], w_ref[...], preferred_element_type=jnp.float32,
                         precision=lax.Precision.HIGHEST) + b_ref[...]


def _ada(c, w_ada, b_ada):
    bsz = c.shape[0]
    rows = -(-bsz // SUBLANES) * SUBLANES
    c_p = jnp.zeros((rows, D_MODEL), jnp.float32).at[:bsz].set(c)
    out = pl.pallas_call(
        _ada_kernel,
        grid=(6,),
        in_specs=[pl.BlockSpec((rows, D_MODEL), lambda j: (0, 0)),
                  pl.BlockSpec((D_MODEL, D_MODEL), lambda j: (0, j)),
                  pl.BlockSpec((1, D_MODEL), lambda j: (0, j))],
        out_specs=pl.BlockSpec((rows, D_MODEL), lambda j: (0, j)),
        out_shape=jax.ShapeDtypeStruct((rows, 6 * D_MODEL), jnp.float32),
        compiler_params=_cparams(("arbitrary",)),
        name="ada",
    )(c_p, w_ada, b_ada.reshape(1, -1))
    return out[:bsz]


def _adaln(x, g, scale, shift):
    ms = jnp.mean(x * x, axis=-1, keepdims=True)
    return (x * lax.rsqrt(ms + EPS) * g) * (1.0 + scale) + shift


def _rms_rows(x, g, n):
    ms = jnp.sum(x * x, axis=-1, keepdims=True) * (1.0 / n)
    return x * lax.rsqrt(ms + EPS) * g


def _rope(x, cos, sin_a, sin_b):
    return (x * cos + pltpu.roll(x, LANES - HALF_ROPE, axis=1) * sin_a
            + pltpu.roll(x, HALF_ROPE, axis=1) * sin_b)


def _inproj_kernel(x_ref, xp_ref, xn_ref, scale_ref, shift_ref, g1_ref, w_in_ref, conv_w_ref,
                   w_co_ref, gql_ref, w_uq_ref, gkvl_ref, w_uk_ref, w_uv_ref, gqn_ref, gkn_ref,
                   cos_ref, sa_ref, sb_ref,
                   q_ref, k_ref, v_ref, a_ref, sgb_ref):
    i = pl.program_id(1)
    n_i = pl.num_programs(1)
    g1 = g1_ref[...]
    scale = scale_ref[...]
    shift = shift_ref[...]
    ts = x_ref.shape[0]

    h = _adaln(x_ref[...], g1, scale, shift).astype(jnp.bfloat16)

    def proj(lo, width):
        return jnp.dot(h, w_in_ref[:, lo:lo + width], preferred_element_type=jnp.float32)

    z = proj(C_CG, D_CONV) * proj(C_HC, D_CONV)

    def halo_z(xh_ref):
        hh = _adaln(xh_ref[...], g1, scale, shift).astype(jnp.bfloat16)
        zc = jnp.dot(hh, w_in_ref[:, C_CG:C_CG + 2 * D_CONV], preferred_element_type=jnp.float32)
        return zc[:, :D_CONV] * zc[:, D_CONV:]

    z_prev = halo_z(xp_ref)[SUBLANES - 1:SUBLANES, :]
    z_next = halo_z(xn_ref)[0:1, :]
    z_prev = jnp.where(i == 0, 0.0, z_prev)
    z_next = jnp.where(i == n_i - 1, 0.0, z_next)
    row = lax.broadcasted_iota(jnp.int32, (ts, D_CONV), 0)
    z_up = jnp.where(row == 0, z_prev, pltpu.roll(z, 1, axis=0))
    z_dn = jnp.where(row == ts - 1, z_next, pltpu.roll(z, ts - 1, axis=0))
    cw = conv_w_ref[...]
    y = z_up * cw[0:1, :] + z * cw[1:2, :] + z_dn * cw[2:3, :]
    out_a = jnp.dot((proj(C_BG, D_CONV) * y).astype(jnp.bfloat16), w_co_ref[...],
                    preferred_element_type=jnp.float32)
    a_ref[...] = jax.nn.sigmoid(proj(C_GA, D_MODEL)) * out_a
    sgb_ref[...] = jax.nn.sigmoid(proj(C_GB, D_MODEL))

    cos = cos_ref[...]
    sin_a = sa_ref[...]
    sin_b = sb_ref[...]
    cq = _rms_rows(proj(C_CQ, Q_LORA), gql_ref[...], Q_LORA).astype(jnp.bfloat16)
    qf = jnp.dot(cq, w_uq_ref[...], preferred_element_type=jnp.float32)
    ckv = _rms_rows(proj(C_CKV, KV_LORA), gkvl_ref[...], KV_LORA).astype(jnp.bfloat16)
    kf = jnp.dot(ckv, w_uk_ref[...], preferred_element_type=jnp.float32)
    v_ref[...] = jnp.dot(ckv, w_uv_ref[...], preferred_element_type=jnp.float32).astype(jnp.bfloat16)
    kr = proj(C_KR, HEAD_SLOT)
    gqn = gqn_ref[...]
    gkn = gkn_ref[...]
    q_scale = QK_HEAD ** -0.5
    for hd in range(N_HEADS):
        sl = slice(hd * HEAD_SLOT, (hd + 1) * HEAD_SLOT)
        qh = _rope(_rms_rows(qf[:, sl], gqn, QK_HEAD), cos, sin_a, sin_b)
        q_ref[:, sl] = (qh * q_scale).astype(jnp.bfloat16)
        kh = _rope(_rms_rows(kf[:, sl] + kr, gkn, QK_HEAD), cos, sin_a, sin_b)
        k_ref[:, sl] = kh.astype(jnp.bfloat16)


def _inproj(x, scale1, shift1, wts, rope):
    bsz, seq, _ = x.shape
    ts = min(TS_IN, seq)
    n_i = seq // ts
    nb8 = seq // SUBLANES
    per8 = ts // SUBLANES
    tile = lambda b, i: (b, i, 0)
    per_b = lambda b, i: (b, 0, 0)
    full2 = lambda b, i: (0, 0)
    hw = N_HEADS * HEAD_SLOT
    in_specs = [
        pl.BlockSpec((None, ts, D_MODEL), tile),
        pl.BlockSpec((None, SUBLANES, D_MODEL), lambda b, i: (b, jnp.maximum(i * per8 - 1, 0), 0)),
        pl.BlockSpec((None, SUBLANES, D_MODEL), lambda b, i: (b, jnp.minimum((i + 1) * per8, nb8 - 1), 0)),
        pl.BlockSpec((None, 1, D_MODEL), per_b),
        pl.BlockSpec((None, 1, D_MODEL), per_b),
        pl.BlockSpec((1, D_MODEL), full2),
        pl.BlockSpec((D_MODEL, IN_COLS_R), full2),
        pl.BlockSpec((3, D_CONV), full2),
        pl.BlockSpec((D_CONV, D_MODEL), full2),
        pl.BlockSpec((1, Q_LORA), full2),
        pl.BlockSpec((Q_LORA, hw), full2),
        pl.BlockSpec((1, KV_LORA), full2),
        pl.BlockSpec((KV_LORA, hw), full2),
        pl.BlockSpec((KV_LORA, hw), full2),
        pl.BlockSpec((1, HEAD_SLOT), full2),
        pl.BlockSpec((1, HEAD_SLOT), full2),
        pl.BlockSpec((ts, HEAD_SLOT), lambda b, i: (i, 0)),
        pl.BlockSpec((ts, HEAD_SLOT), lambda b, i: (i, 0)),
        pl.BlockSpec((ts, HEAD_SLOT), lambda b, i: (i, 0)),
    ]
    out_specs = [pl.BlockSpec((None, ts, hw), tile)] * 3 + [pl.BlockSpec((None, ts, D_MODEL), tile)] * 2
    out_shape = ([jax.ShapeDtypeStruct((bsz, seq, hw), jnp.bfloat16)] * 3
                 + [jax.ShapeDtypeStruct((bsz, seq, D_MODEL), jnp.float32)] * 2)
    return pl.pallas_call(
        _inproj_kernel,
        grid=(bsz, n_i),
        in_specs=in_specs,
        out_specs=out_specs,
        out_shape=out_shape,
        compiler_params=_cparams(("parallel", "parallel")),
        name="inproj",
    )(x, x, x, scale1, shift1, wts["g_norm1"], wts["w_in"], wts["conv_w"], wts["w_conv_out"],
      wts["g_q_lora"], wts["w_uq"], wts["g_kv_lora"], wts["w_uk"], wts["w_uv"],
      wts["g_qnorm"], wts["g_knorm"], rope[0], rope[1], rope[2])


def _attn_kernel(q_ref, k_ref, v_ref, o_ref):
    q = q_ref[...]
    tq = q.shape[0]
    seq = k_ref.shape[0]
    tk = min(TK, seq)

    def body(j, carry):
        m, l, acc = carry
        off = pl.multiple_of(j * tk, tk)
        kc = k_ref[pl.ds(off, tk), :]
        vc = v_ref[pl.ds(off, tk), :]
        s = lax.dot_general(q, kc, (((1,), (1,)), ((), ())), preferred_element_type=jnp.float32)
        m_new = jnp.maximum(m, jnp.max(s, axis=-1, keepdims=True))
        p = jnp.exp(s - m_new)
        alpha = jnp.exp(m - m_new)
        l = alpha * l + jnp.sum(p, axis=-1, keepdims=True)
        acc = alpha * acc + jnp.dot(p.astype(jnp.bfloat16), vc, preferred_element_type=jnp.float32)
        return m_new, l, acc

    m0 = jnp.full((tq, 1), _NEG_INF, jnp.float32)
    l0 = jnp.zeros((tq, 1), jnp.float32)
    acc0 = jnp.zeros((tq, HEAD_SLOT), jnp.float32)
    _, l, acc = lax.fori_loop(0, seq // tk, body, (m0, l0, acc0))
    o_ref[...] = (acc / l).astype(jnp.bfloat16)


def _attention(q, k, v):
    bsz, seq, hw = q.shape
    tq = min(TQ, seq)
    return pl.pallas_call(
        _attn_kernel,
        grid=(bsz, N_HEADS, seq // tq),
        in_specs=[pl.BlockSpec((None, tq, HEAD_SLOT), lambda b, h, i: (b, i, h)),
                  pl.BlockSpec((None, seq, HEAD_SLOT), lambda b, h, i: (b, 0, h)),
                  pl.BlockSpec((None, seq, HEAD_SLOT), lambda b, h, i: (b, 0, h))],
        out_specs=pl.BlockSpec((None, tq, HEAD_SLOT), lambda b, h, i: (b, i, h)),
        out_shape=jax.ShapeDtypeStruct((bsz, seq, hw), jnp.bfloat16),
        compiler_params=_cparams(("parallel", "parallel", "arbitrary")),
        name="attn",
    )(q, k, v)


def _topk_rows(s, iota):
    vals, idxs = [], []
    big = jnp.float32(1e9)
    for _ in range(PEER_TOPK):
        m = jnp.max(s, axis=0, keepdims=True)
        am = jnp.min(jnp.where(s == m, iota, big), axis=0, keepdims=True)
        vals.append(m)
        idxs.append(am)
        s = jnp.where(iota == am, _NEG_INF, s)
    return jnp.concatenate(vals, axis=0), jnp.concatenate(idxs, axis=0)


def _take_rows(tab, idx):
    out = jnp.zeros_like(tab)
    for a in range(tab.shape[0]):
        out = out + jnp.where(idx == jnp.float32(a), tab[a:a + 1, :], 0.0)
    return out


def _mix_kernel(x_ref, a_ref, sgb_ref, o_ref, gate1_ref, scale2_ref, shift2_ref, g2_ref,
                w_ao_ref, w_out_ref, w_pq_ref, kcat_ref,
                x1_ref, h2_ref, e_ref, g_ref):
    ts = x_ref.shape[0]
    out_b = jnp.dot(o_ref[...], w_ao_ref[...], preferred_element_type=jnp.float32)
    merged = (a_ref[...] + sgb_ref[...] * out_b).astype(jnp.bfloat16)
    x1 = x_ref[...] + gate1_ref[...] * jnp.dot(merged, w_out_ref[...],
                                              preferred_element_type=jnp.float32)
    x1_ref[...] = x1
    h2 = _adaln(x1, g2_ref[...], scale2_ref[...], shift2_ref[...])
    h2_ref[...] = h2
    qp = jnp.dot(h2.astype(jnp.bfloat16), w_pq_ref[...],
                 preferred_element_type=jnp.float32).astype(jnp.bfloat16)

    iota_k = lax.broadcasted_iota(jnp.int32, (N_KEYS, ts), 0).astype(jnp.float32)
    iota_c = lax.broadcasted_iota(jnp.int32, (PEER_TOPK * PEER_TOPK, ts), 0).astype(jnp.float32)
    e_rows, g_rows = [], []
    for hd in range(PEER_HEADS):
        qh = qp[:, hd * PEER_DK:(hd + 1) * PEER_DK]
        st = lax.dot_general(kcat_ref[hd], qh, (((1,), (1,)), ((), ())),
                             preferred_element_type=jnp.float32)
        v1, i1 = _topk_rows(st[:N_KEYS], iota_k)
        v2, i2 = _topk_rows(st[N_KEYS:], iota_k)
        comb = jnp.concatenate([v1[a:a + 1, :] + v2 for a in range(PEER_TOPK)], axis=0)
        sc, pos = _topk_rows(comb, iota_c)
        ia = jnp.floor(pos * (1.0 / PEER_TOPK))
        ib = pos - ia * PEER_TOPK
        e_rows.append(_take_rows(i1, ia) * N_KEYS + _take_rows(i2, ib))
        p = jnp.exp(sc - sc[0:1, :])
        g_rows.append(p / jnp.sum(p, axis=0, keepdims=True))
    e_all = jnp.concatenate(e_rows, axis=0)
    g_all = jnp.concatenate(g_rows, axis=0)
    for c in range(ts // LANES):
        cs = slice(c * LANES, (c + 1) * LANES)
        e_ref[cs, :] = (e_all[:, cs].T * ROW_SUB).astype(jnp.int32)
        g_ref[cs, :] = g_all[:, cs].T


def _mix(x, a, sgb, o, gate1, scale2, shift2, wts):
    bsz, seq, _ = x.shape
    ts = min(TS_MIX, seq)
    tile = lambda b, i: (b, i, 0)
    per_b = lambda b, i: (b, 0, 0)
    full2 = lambda b, i: (0, 0)
    hw = N_HEADS * HEAD_SLOT
    big = pl.BlockSpec((None, ts, D_MODEL), tile)
    vec = pl.BlockSpec((None, 1, D_MODEL), per_b)
    picks = pl.BlockSpec((None, ts, N_PICKS), tile)
    return pl.pallas_call(
        _mix_kernel,
        grid=(bsz, seq // ts),
        in_specs=[big, big, big, pl.BlockSpec((None, ts, hw), tile), vec, vec, vec,
                  pl.BlockSpec((1, D_MODEL), full2),
                  pl.BlockSpec((hw, D_MODEL), full2),
                  pl.BlockSpec((D_MODEL, D_MODEL), full2),
                  pl.BlockSpec((D_MODEL, PEER_HEADS * PEER_DK), full2),
                  pl.BlockSpec((PEER_HEADS, 2 * N_KEYS, PEER_DK), lambda b, i: (0, 0, 0))],
        out_specs=[big, big, picks, picks],
        out_shape=[jax.ShapeDtypeStruct((bsz, seq, D_MODEL), jnp.float32),
                   jax.ShapeDtypeStruct((bsz, seq, D_MODEL), jnp.float32),
                   jax.ShapeDtypeStruct((bsz, seq, N_PICKS), jnp.int32),
                   jax.ShapeDtypeStruct((bsz, seq, N_PICKS), jnp.float32)],
        compiler_params=_cparams(("parallel", "parallel")),
        name="mix",
    )(x, a, sgb, o, gate1, scale2, shift2, wts["g_norm2"], wts["w_attn_out"], wts["w_out"],
      wts["peer_wq"], wts["kcat"])


def _unpack_row(w):
    lo = lax.bitcast_convert_type(lax.shift_left(w, 16), jnp.float32)
    hi = lax.bitcast_convert_type(jnp.bitwise_and(w, jnp.int32(-65536)), jnp.float32)
    return lo, hi


def _gelu_tanh(x):
    c = math.sqrt(2.0 / math.pi)
    return 0.5 * x * (1.0 + jnp.tanh(c * (x + 0.044715 * (x * x * x))))


def _peer_u_kernel(e_ref, x_ref, g_ref, tab_ref, w_ref, pbuf, abuf):
    tt = g_ref.shape[0]
    ones = jnp.ones((SUBLANES, LANES), jnp.bfloat16)
    nt = (((1,), (1,)), ((), ()))

    def tok(t, carry):
        base = pl.multiple_of(t * SUBLANES, SUBLANES)
        xlo = x_ref[pl.ds(base, ROW_SUB), :]
        xhi = x_ref[pl.ds(base + ROW_SUB, ROW_SUB), :]
        for k in range(N_PICKS):
            r = pl.multiple_of(e_ref[t, k], ROW_SUB)
            lo, hi = _unpack_row(tab_ref[pl.ds(r, ROW_SUB), :])
            pbuf[k * SUBLANES:k * SUBLANES + ROW_SUB, :] = lo * xlo + hi * xhi
        r = pbuf[pl.ds(0, N_PICKS, stride=SUBLANES), :]
        for s in range(1, ROW_SUB):
            r = r + pbuf[pl.ds(s, N_PICKS, stride=SUBLANES), :]
        r_hi = r.astype(jnp.bfloat16)
        r_lo = (r - r_hi.astype(jnp.float32)).astype(jnp.bfloat16)
        a = (lax.dot_general(ones, r_hi, nt, preferred_element_type=jnp.float32)
             + lax.dot_general(ones, r_lo, nt, preferred_element_type=jnp.float32))
        abuf[pl.ds(t, 1), :] = a[0:1, :]
        return carry

    lax.fori_loop(0, tt, tok, 0)
    w_ref[...] = g_ref[...] * _gelu_tanh(abuf[...])


def _peer_u(e, h2_rows, g, tab):
    n = e.shape[0]
    tt = min(TT_PEER, n)
    return pl.pallas_call(
        _peer_u_kernel,
        grid=(n // tt,),
        in_specs=[pl.BlockSpec((tt, N_PICKS), lambda i: (i, 0), memory_space=pltpu.SMEM),
                  pl.BlockSpec((tt * SUBLANES, LANES), lambda i: (i, 0)),
                  pl.BlockSpec((tt, N_PICKS), lambda i: (i, 0)),
                  pl.BlockSpec(memory_space=pltpu.VMEM)],
        out_specs=pl.BlockSpec((tt, N_PICKS), lambda i: (i, 0)),
        out_shape=jax.ShapeDtypeStruct((n, N_PICKS), jnp.float32),
        scratch_shapes=[pltpu.VMEM((N_PICKS * SUBLANES, LANES), jnp.float32),
                        pltpu.VMEM((tt, N_PICKS), jnp.float32)],
        compiler_params=_cparams(("arbitrary",)),
        name="peer_u",
    )(e, h2_rows, g, tab)


N_ACC = 4


def _peer_v_kernel(e_ref, w_ref, x_ref, gate_ref, tab_ref, y_ref):
    tt = e_ref.shape[0]
    g_lo = gate_ref[0:ROW_SUB, :]
    g_hi = gate_ref[ROW_SUB:, :]

    def tok(t, carry):
        base = pl.multiple_of(t * SUBLANES, SUBLANES)
        acc_lo = [jnp.zeros((ROW_SUB, LANES), jnp.float32) for _ in range(N_ACC)]
        acc_hi = [jnp.zeros((ROW_SUB, LANES), jnp.float32) for _ in range(N_ACC)]
        for k in range(N_PICKS):
            r = pl.multiple_of(e_ref[t, k], ROW_SUB)
            wk = w_ref[t, k]
            lo, hi = _unpack_row(tab_ref[pl.ds(r, ROW_SUB), :])
            acc_lo[k % N_ACC] = acc_lo[k % N_ACC] + wk * lo
            acc_hi[k % N_ACC] = acc_hi[k % N_ACC] + wk * hi
        s_lo = (acc_lo[0] + acc_lo[1]) + (acc_lo[2] + acc_lo[3])
        s_hi = (acc_hi[0] + acc_hi[1]) + (acc_hi[2] + acc_hi[3])
        y_ref[pl.ds(base, ROW_SUB), :] = x_ref[pl.ds(base, ROW_SUB), :] + g_lo * s_lo
        y_ref[pl.ds(base + ROW_SUB, ROW_SUB), :] = x_ref[pl.ds(base + ROW_SUB, ROW_SUB), :] + g_hi * s_hi
        return carry

    lax.fori_loop(0, tt, tok, 0)


def _peer_v(e, w, x1_rows, gate2_rows, tab, seq):
    n = e.shape[0]
    tt = min(TT_PEER, n)
    return pl.pallas_call(
        _peer_v_kernel,
        grid=(n // tt,),
        in_specs=[pl.BlockSpec((tt, N_PICKS), lambda i: (i, 0), memory_space=pltpu.SMEM),
                  pl.BlockSpec((tt, N_PICKS), lambda i: (i, 0), memory_space=pltpu.SMEM),
                  pl.BlockSpec((tt * SUBLANES, LANES), lambda i: (i, 0)),
                  pl.BlockSpec((None, SUBLANES, LANES), lambda i: ((i * tt) // seq, 0, 0)),
                  pl.BlockSpec(memory_space=pltpu.VMEM)],
        out_specs=pl.BlockSpec((tt * SUBLANES, LANES), lambda i: (i, 0)),
        out_shape=jax.ShapeDtypeStruct((n * SUBLANES, LANES), jnp.float32),
        compiler_params=_cparams(("arbitrary",)),
        name="peer_v",
    )(e, w, x1_rows, gate2_rows, tab)


def _pad_heads(w, used, lead=0):
    r = w.shape[0]
    w3 = w.reshape(r, N_HEADS, used)
    out = jnp.zeros((r, N_HEADS, HEAD_SLOT), w.dtype).at[:, :, lead:lead + used].set(w3)
    return out.reshape(r, N_HEADS * HEAD_SLOT)


def _pack_table(t):
    tb = t.astype(jnp.bfloat16)
    lo = lax.bitcast_convert_type(tb[:, :ROW_WORDS], jnp.uint16).astype(jnp.uint32)
    hi = lax.bitcast_convert_type(tb[:, ROW_WORDS:], jnp.uint16).astype(jnp.uint32)
    words = lax.bitcast_convert_type(lo | (hi << 16), jnp.int32)
    return words.reshape(t.shape[0] * ROW_SUB, LANES)


def _prep_weights(g_norm1, w_in, conv_w, w_conv_out, g_q_lora, w_uq, g_kv_lora, w_ukv, g_qnorm,
                  g_knorm, w_attn_out, w_out, g_norm2, peer_wq, peer_k1, peer_k2, peer_u, peer_v):
    bf = jnp.bfloat16
    o1 = 3 * D_CONV
    o2 = o1 + Q_LORA
    o3 = o2 + KV_LORA
    o4 = o3 + QK_ROPE
    o5 = o4 + D_MODEL
    kr_slot = jnp.zeros((D_MODEL, HEAD_SLOT), w_in.dtype).at[:, QK_NOPE:QK_HEAD].set(w_in[:, o3:o4])
    w_in_r = jnp.concatenate([w_in[:, :o3], kr_slot, w_in[:, o4:]], axis=1).astype(bf)
    w_ukv3 = w_ukv.reshape(KV_LORA, N_HEADS, QK_NOPE + V_HEAD)
    w_uk = _pad_heads(w_ukv3[:, :, :QK_NOPE].reshape(KV_LORA, -1), QK_NOPE)
    w_uv = _pad_heads(w_ukv3[:, :, QK_NOPE:].reshape(KV_LORA, -1), V_HEAD)
    pad_gain = lambda g: jnp.zeros((1, HEAD_SLOT), jnp.float32).at[0, :QK_HEAD].set(g)
    w_ao = jnp.zeros((N_HEADS, HEAD_SLOT, D_MODEL), w_attn_out.dtype).at[:, :V_HEAD].set(
        w_attn_out.reshape(N_HEADS, V_HEAD, D_MODEL)).reshape(N_HEADS * HEAD_SLOT, D_MODEL)
    half = PEER_DK // 2
    kcat = jnp.zeros((PEER_HEADS, 2 * N_KEYS, PEER_DK), jnp.float32)
    kcat = kcat.at[:, :N_KEYS, :half].set(peer_k1).at[:, N_KEYS:, half:].set(peer_k2)
    return {
        "g_norm1": g_norm1.reshape(1, -1), "w_in": w_in_r, "conv_w": conv_w,
        "w_conv_out": w_conv_out.astype(bf), "g_q_lora": g_q_lora.reshape(1, -1),
        "w_uq": _pad_heads(w_uq, QK_HEAD).astype(bf), "g_kv_lora": g_kv_lora.reshape(1, -1),
        "w_uk": w_uk.astype(bf), "w_uv": w_uv.astype(bf),
        "g_qnorm": pad_gain(g_qnorm), "g_knorm": pad_gain(g_knorm),
        "w_attn_out": w_ao.astype(bf), "w_out": w_out.astype(bf), "g_norm2": g_norm2.reshape(1, -1),
        "peer_wq": peer_wq.astype(bf), "kcat": kcat.astype(bf),
        "tab_u": _pack_table(peer_u), "tab_v": _pack_table(peer_v),
    }


def _rope_tables(seq):
    pos = jnp.arange(seq, dtype=jnp.float32)
    inv = ROPE_THETA ** (-jnp.arange(0, QK_ROPE, 2, dtype=jnp.float32) / QK_ROPE)
    ang = pos[:, None] * inv[None, :]
    cos, sin = jnp.cos(ang), jnp.sin(ang)
    t1 = slice(QK_NOPE, QK_NOPE + HALF_ROPE)
    t2 = slice(QK_NOPE + HALF_ROPE, QK_HEAD)
    cos_t = jnp.ones((seq, HEAD_SLOT), jnp.float32).at[:, t1].set(cos).at[:, t2].set(cos)
    sin_a = jnp.zeros((seq, HEAD_SLOT), jnp.float32).at[:, t1].set(-sin)
    sin_b = jnp.zeros((seq, HEAD_SLOT), jnp.float32).at[:, t2].set(sin)
    return cos_t, sin_a, sin_b


def _encoder_layer(x, c, w_ada, b_ada, wts):
    bsz, seq, d = x.shape
    n = bsz * seq
    ada = _ada(c, w_ada, b_ada)[:, None, :]
    shift1, scale1, gate1, shift2, scale2, gate2 = jnp.split(ada, 6, axis=-1)
    q, k, v, a, sgb = _inproj(x, scale1, shift1, wts, _rope_tables(seq))
    o = _attention(q, k, v)
    x1, h2, e, g = _mix(x, a, sgb, o, gate1, scale2, shift2, wts)
    e = e.reshape(n, N_PICKS)
    w = _peer_u(e, h2.reshape(n * SUBLANES, LANES), g.reshape(n, N_PICKS), wts["tab_u"])
    y = _peer_v(e, w, x1.reshape(n * SUBLANES, LANES), gate2.reshape(bsz, SUBLANES, LANES),
                wts["tab_v"], seq)
    return y.reshape(bsz, seq, d)


def kernel(x_prompt, x_sample, c_prompt, c_sample, w_ada, b_ada, g_norm1, w_in, conv_w, w_conv_out, g_q_lora, w_uq, g_kv_lora, w_ukv, g_qnorm, g_knorm, w_attn_out, w_out, g_norm2, peer_wq, peer_k1, peer_k2, peer_u, peer_v):
    wts = _prep_weights(g_norm1[0], w_in[0], conv_w[0], w_conv_out[0], g_q_lora[0], w_uq[0],
                        g_kv_lora[0], w_ukv[0], g_qnorm[0], g_knorm[0], w_attn_out[0], w_out[0],
                        g_norm2[0], peer_wq[0], peer_k1[0], peer_k2[0], peer_u[0], peer_v[0])
    return (_encoder_layer(x_prompt, c_prompt, w_ada[0], b_ada[0], wts),
            _encoder_layer(x_sample, c_sample, w_ada[0], b_ada[0], wts))
```

```python
import functools
import math

import jax
import jax.numpy as jnp
from jax import lax
from jax.experimental import pallas as pl
from jax.experimental.pallas import tpu as pltpu

D_MODEL = 1024
D_CONV = 512
N_HEADS = 8
QK_NOPE = 64
QK_ROPE = 32
V_HEAD = 64
Q_LORA = 256
KV_LORA = 128
QK_HEAD = QK_NOPE + QK_ROPE
ROPE_THETA = 10000.0
PEER_HEADS = 8
N_KEYS = 128
PEER_DK = 128
PEER_TOPK = 16
EPS = 1e-6

LANES = 128
SUBLANES = 8
HEAD_SLOT = LANES
HALF_ROPE = QK_ROPE // 2
N_PICKS = PEER_HEADS * PEER_TOPK
ROW_WORDS = D_MODEL // 2
ROW_SUB = ROW_WORDS // LANES
VMEM_LIMIT = 56 * 1024 * 1024

C_BG, C_CG, C_HC = 0, 512, 1024
C_CQ = 1536
C_CKV = C_CQ + Q_LORA
C_KR = C_CKV + KV_LORA
C_GA = C_KR + HEAD_SLOT
C_GB = C_GA + D_MODEL
IN_COLS_R = C_GB + D_MODEL

TS_IN = 256
TQ = 512
TK = 512
TS_MIX = 256
TT_PEER = 128
TOK_UNROLL_U = 2
TOK_UNROLL_V = 4
RED_TOKENS = 8
BF16_ROWS = 16

_NEG_INF = float("-inf")


def _cparams(sem):
    return pltpu.CompilerParams(dimension_semantics=sem, vmem_limit_bytes=VMEM_LIMIT)


def _ada_kernel(c_ref, w_ref, b_ref, o_ref):
    o_ref[...] = jnp.dot(c_ref[...], w_ref[...], preferred_element_type=jnp.float32,
                         precision=lax.Precision.HIGHEST) + b_ref[...]


def _ada(c, w_ada, b_ada):
    bsz = c.shape[0]
    rows = -(-bsz // SUBLANES) * SUBLANES
    c_p = jnp.zeros((rows, D_MODEL), jnp.float32).at[:bsz].set(c)
    out = pl.pallas_call(
        _ada_kernel,
        grid=(6,),
        in_specs=[pl.BlockSpec((rows, D_MODEL), lambda j: (0, 0)),
                  pl.BlockSpec((D_MODEL, D_MODEL), lambda j: (0, j)),
                  pl.BlockSpec((1, D_MODEL), lambda j: (0, j))],
        out_specs=pl.BlockSpec((rows, D_MODEL), lambda j: (0, j)),
        out_shape=jax.ShapeDtypeStruct((rows, 6 * D_MODEL), jnp.float32),
        compiler_params=_cparams(("arbitrary",)),
        name="ada",
    )(c_p, w_ada, b_ada.reshape(1, -1))
    return out[:bsz]


def _adaln(x, g, scale, shift):
    ms = jnp.mean(x * x, axis=-1, keepdims=True)
    return (x * lax.rsqrt(ms + EPS) * g) * (1.0 + scale) + shift


def _rms_rows(x, g, n):
    ms = jnp.sum(x * x, axis=-1, keepdims=True) * (1.0 / n)
    return x * lax.rsqrt(ms + EPS) * g


def _rope(x, cos, sin_a, sin_b):
    return (x * cos + pltpu.roll(x, LANES - HALF_ROPE, axis=1) * sin_a
            + pltpu.roll(x, HALF_ROPE, axis=1) * sin_b)


def _inproj_kernel(x_ref, xp_ref, xn_ref, scale_ref, shift_ref, g1_ref, w_in_ref, conv_w_ref,
                   w_co_ref, gql_ref, w_uq_ref, gkvl_ref, w_uk_ref, w_uv_ref, gqn_ref, gkn_ref,
                   cos_ref, sa_ref, sb_ref,
                   q_ref, k_ref, v_ref, a_ref, sgb_ref):
    i = pl.program_id(1)
    n_i = pl.num_programs(1)
    g1 = g1_ref[...]
    scale = scale_ref[...]
    shift = shift_ref[...]
    ts = x_ref.shape[0]

    h = _adaln(x_ref[...], g1, scale, shift).astype(jnp.bfloat16)

    def proj(lo, width):
        return jnp.dot(h, w_in_ref[:, lo:lo + width], preferred_element_type=jnp.float32)

    z = proj(C_CG, D_CONV) * proj(C_HC, D_CONV)

    def halo_z(xh_ref):
        hh = _adaln(xh_ref[...], g1, scale, shift).astype(jnp.bfloat16)
        zc = jnp.dot(hh, w_in_ref[:, C_CG:C_CG + 2 * D_CONV], preferred_element_type=jnp.float32)
        return zc[:, :D_CONV] * zc[:, D_CONV:]

    z_prev = halo_z(xp_ref)[SUBLANES - 1:SUBLANES, :]
    z_next = halo_z(xn_ref)[0:1, :]
    z_prev = jnp.where(i == 0, 0.0, z_prev)
    z_next = jnp.where(i == n_i - 1, 0.0, z_next)
    row = lax.broadcasted_iota(jnp.int32, (ts, D_CONV), 0)
    z_up = jnp.where(row == 0, z_prev, pltpu.roll(z, 1, axis=0))
    z_dn = jnp.where(row == ts - 1, z_next, pltpu.roll(z, ts - 1, axis=0))
    cw = conv_w_ref[...]
    y = z_up * cw[0:1, :] + z * cw[1:2, :] + z_dn * cw[2:3, :]
    out_a = jnp.dot((proj(C_BG, D_CONV) * y).astype(jnp.bfloat16), w_co_ref[...],
                    preferred_element_type=jnp.float32)
    a_ref[...] = jax.nn.sigmoid(proj(C_GA, D_MODEL)) * out_a
    sgb_ref[...] = jax.nn.sigmoid(proj(C_GB, D_MODEL))

    cos = cos_ref[...]
    sin_a = sa_ref[...]
    sin_b = sb_ref[...]
    cq = _rms_rows(proj(C_CQ, Q_LORA), gql_ref[...], Q_LORA).astype(jnp.bfloat16)
    qf = jnp.dot(cq, w_uq_ref[...], preferred_element_type=jnp.float32)
    ckv = _rms_rows(proj(C_CKV, KV_LORA), gkvl_ref[...], KV_LORA).astype(jnp.bfloat16)
    kf = jnp.dot(ckv, w_uk_ref[...], preferred_element_type=jnp.float32)
    v_ref[...] = jnp.dot(ckv, w_uv_ref[...], preferred_element_type=jnp.float32).astype(jnp.bfloat16)
    kr = proj(C_KR, HEAD_SLOT)
    gqn = gqn_ref[...]
    gkn = gkn_ref[...]
    q_scale = QK_HEAD ** -0.5
    for hd in range(N_HEADS):
        sl = slice(hd * HEAD_SLOT, (hd + 1) * HEAD_SLOT)
        qh = _rope(_rms_rows(qf[:, sl], gqn, QK_HEAD), cos, sin_a, sin_b)
        q_ref[:, sl] = (qh * q_scale).astype(jnp.bfloat16)
        kh = _rope(_rms_rows(kf[:, sl] + kr, gkn, QK_HEAD), cos, sin_a, sin_b)
        k_ref[:, sl] = kh.astype(jnp.bfloat16)


def _inproj(x, scale1, shift1, wts, rope):
    bsz, seq, _ = x.shape
    ts = min(TS_IN, seq)
    n_i = seq // ts
    nb8 = seq // SUBLANES
    per8 = ts // SUBLANES
    tile = lambda b, i: (b, i, 0)
    per_b = lambda b, i: (b, 0, 0)
    full2 = lambda b, i: (0, 0)
    hw = N_HEADS * HEAD_SLOT
    in_specs = [
        pl.BlockSpec((None, ts, D_MODEL), tile),
        pl.BlockSpec((None, SUBLANES, D_MODEL), lambda b, i: (b, jnp.maximum(i * per8 - 1, 0), 0)),
        pl.BlockSpec((None, SUBLANES, D_MODEL), lambda b, i: (b, jnp.minimum((i + 1) * per8, nb8 - 1), 0)),
        pl.BlockSpec((None, 1, D_MODEL), per_b),
        pl.BlockSpec((None, 1, D_MODEL), per_b),
        pl.BlockSpec((1, D_MODEL), full2),
        pl.BlockSpec((D_MODEL, IN_COLS_R), full2),
        pl.BlockSpec((3, D_CONV), full2),
        pl.BlockSpec((D_CONV, D_MODEL), full2),
        pl.BlockSpec((1, Q_LORA), full2),
        pl.BlockSpec((Q_LORA, hw), full2),
        pl.BlockSpec((1, KV_LORA), full2),
        pl.BlockSpec((KV_LORA, hw), full2),
        pl.BlockSpec((KV_LORA, hw), full2),
        pl.BlockSpec((1, HEAD_SLOT), full2),
        pl.BlockSpec((1, HEAD_SLOT), full2),
        pl.BlockSpec((ts, HEAD_SLOT), lambda b, i: (i, 0)),
        pl.BlockSpec((ts, HEAD_SLOT), lambda b, i: (i, 0)),
        pl.BlockSpec((ts, HEAD_SLOT), lambda b, i: (i, 0)),
    ]
    out_specs = [pl.BlockSpec((None, ts, hw), tile)] * 3 + [pl.BlockSpec((None, ts, D_MODEL), tile)] * 2
    out_shape = ([jax.ShapeDtypeStruct((bsz, seq, hw), jnp.bfloat16)] * 3
                 + [jax.ShapeDtypeStruct((bsz, seq, D_MODEL), jnp.float32)] * 2)
    return pl.pallas_call(
        _inproj_kernel,
        grid=(bsz, n_i),
        in_specs=in_specs,
        out_specs=out_specs,
        out_shape=out_shape,
        compiler_params=_cparams(("parallel", "parallel")),
        name="inproj",
    )(x, x, x, scale1, shift1, wts["g_norm1"], wts["w_in"], wts["conv_w"], wts["w_conv_out"],
      wts["g_q_lora"], wts["w_uq"], wts["g_kv_lora"], wts["w_uk"], wts["w_uv"],
      wts["g_qnorm"], wts["g_knorm"], rope[0], rope[1], rope[2])


def _attn_kernel(q_ref, k_ref, v_ref, o_ref):
    q = q_ref[...]
    tq = q.shape[0]
    seq = k_ref.shape[0]
    tk = min(TK, seq)

    def body(j, carry):
        m, l, acc = carry
        off = pl.multiple_of(j * tk, tk)
        kc = k_ref[pl.ds(off, tk), :]
        vc = v_ref[pl.ds(off, tk), :]
        s = lax.dot_general(q, kc, (((1,), (1,)), ((), ())), preferred_element_type=jnp.float32)
        m_new = jnp.maximum(m, jnp.max(s, axis=-1, keepdims=True))
        p = jnp.exp(s - m_new)
        alpha = jnp.exp(m - m_new)
        l = alpha * l + jnp.sum(p, axis=-1, keepdims=True)
        acc = alpha * acc + jnp.dot(p.astype(jnp.bfloat16), vc, preferred_element_type=jnp.float32)
        return m_new, l, acc

    m0 = jnp.full((tq, 1), _NEG_INF, jnp.float32)
    l0 = jnp.zeros((tq, 1), jnp.float32)
    acc0 = jnp.zeros((tq, HEAD_SLOT), jnp.float32)
    _, l, acc = lax.fori_loop(0, seq // tk, body, (m0, l0, acc0))
    o_ref[...] = (acc / l).astype(jnp.bfloat16)


def _attention(q, k, v):
    bsz, seq, hw = q.shape
    tq = min(TQ, seq)
    return pl.pallas_call(
        _attn_kernel,
        grid=(bsz, N_HEADS, seq // tq),
        in_specs=[pl.BlockSpec((None, tq, HEAD_SLOT), lambda b, h, i: (b, i, h)),
                  pl.BlockSpec((None, seq, HEAD_SLOT), lambda b, h, i: (b, 0, h)),
                  pl.BlockSpec((None, seq, HEAD_SLOT), lambda b, h, i: (b, 0, h))],
        out_specs=pl.BlockSpec((None, tq, HEAD_SLOT), lambda b, h, i: (b, i, h)),
        out_shape=jax.ShapeDtypeStruct((bsz, seq, hw), jnp.bfloat16),
        compiler_params=_cparams(("parallel", "parallel", "arbitrary")),
        name="attn",
    )(q, k, v)


def _topk_rows(s, iota):
    vals, idxs = [], []
    big = jnp.float32(1e9)
    for _ in range(PEER_TOPK):
        m = jnp.max(s, axis=0, keepdims=True)
        am = jnp.min(jnp.where(s == m, iota, big), axis=0, keepdims=True)
        vals.append(m)
        idxs.append(am)
        s = jnp.where(iota == am, _NEG_INF, s)
    return jnp.concatenate(vals, axis=0), jnp.concatenate(idxs, axis=0)


def _take_rows(tab, idx):
    out = jnp.zeros_like(tab)
    for a in range(tab.shape[0]):
        out = out + jnp.where(idx == jnp.float32(a), tab[a:a + 1, :], 0.0)
    return out


def _mix_kernel(x_ref, a_ref, sgb_ref, o_ref, gate1_ref, scale2_ref, shift2_ref, g2_ref,
                w_ao_ref, w_out_ref, w_pq_ref, kcat_ref,
                x1_ref, h2_ref, e_ref, g_ref):
    ts = x_ref.shape[0]
    out_b = jnp.dot(o_ref[...], w_ao_ref[...], preferred_element_type=jnp.float32)
    merged = (a_ref[...] + sgb_ref[...] * out_b).astype(jnp.bfloat16)
    x1 = x_ref[...] + gate1_ref[...] * jnp.dot(merged, w_out_ref[...],
                                              preferred_element_type=jnp.float32)
    x1_ref[...] = x1
    h2 = _adaln(x1, g2_ref[...], scale2_ref[...], shift2_ref[...])
    h2_ref[...] = h2
    qp = jnp.dot(h2.astype(jnp.bfloat16), w_pq_ref[...],
                 preferred_element_type=jnp.float32).astype(jnp.bfloat16)

    iota_k = lax.broadcasted_iota(jnp.int32, (N_KEYS, ts), 0).astype(jnp.float32)
    iota_c = lax.broadcasted_iota(jnp.int32, (PEER_TOPK * PEER_TOPK, ts), 0).astype(jnp.float32)
    e_rows, g_rows = [], []
    for hd in range(PEER_HEADS):
        qh = qp[:, hd * PEER_DK:(hd + 1) * PEER_DK]
        st = lax.dot_general(kcat_ref[hd], qh, (((1,), (1,)), ((), ())),
                             preferred_element_type=jnp.float32)
        v1, i1 = _topk_rows(st[:N_KEYS], iota_k)
        v2, i2 = _topk_rows(st[N_KEYS:], iota_k)
        comb = jnp.concatenate([v1[a:a + 1, :] + v2 for a in range(PEER_TOPK)], axis=0)
        sc, pos = _topk_rows(comb, iota_c)
        ia = jnp.floor(pos * (1.0 / PEER_TOPK))
        ib = pos - ia * PEER_TOPK
        e_rows.append(_take_rows(i1, ia) * N_KEYS + _take_rows(i2, ib))
        p = jnp.exp(sc - sc[0:1, :])
        g_rows.append(p / jnp.sum(p, axis=0, keepdims=True))
    e_all = jnp.concatenate(e_rows, axis=0)
    g_all = jnp.concatenate(g_rows, axis=0)
    for c in range(ts // LANES):
        cs = slice(c * LANES, (c + 1) * LANES)
        e_ref[cs, :] = (e_all[:, cs].T * ROW_SUB).astype(jnp.int32)
        g_ref[cs, :] = g_all[:, cs].T


def _mix(x, a, sgb, o, gate1, scale2, shift2, wts):
    bsz, seq, _ = x.shape
    ts = min(TS_MIX, seq)
    tile = lambda b, i: (b, i, 0)
    per_b = lambda b, i: (b, 0, 0)
    full2 = lambda b, i: (0, 0)
    hw = N_HEADS * HEAD_SLOT
    big = pl.BlockSpec((None, ts, D_MODEL), tile)
    vec = pl.BlockSpec((None, 1, D_MODEL), per_b)
    picks = pl.BlockSpec((None, ts, N_PICKS), tile)
    return pl.pallas_call(
        _mix_kernel,
        grid=(bsz, seq // ts),
        in_specs=[big, big, big, pl.BlockSpec((None, ts, hw), tile), vec, vec, vec,
                  pl.BlockSpec((1, D_MODEL), full2),
                  pl.BlockSpec((hw, D_MODEL), full2),
                  pl.BlockSpec((D_MODEL, D_MODEL), full2),
                  pl.BlockSpec((D_MODEL, PEER_HEADS * PEER_DK), full2),
                  pl.BlockSpec((PEER_HEADS, 2 * N_KEYS, PEER_DK), lambda b, i: (0, 0, 0))],
        out_specs=[big, big, picks, picks],
        out_shape=[jax.ShapeDtypeStruct((bsz, seq, D_MODEL), jnp.float32),
                   jax.ShapeDtypeStruct((bsz, seq, D_MODEL), jnp.float32),
                   jax.ShapeDtypeStruct((bsz, seq, N_PICKS), jnp.int32),
                   jax.ShapeDtypeStruct((bsz, seq, N_PICKS), jnp.float32)],
        compiler_params=_cparams(("parallel", "parallel")),
        name="mix",
    )(x, a, sgb, o, gate1, scale2, shift2, wts["g_norm2"], wts["w_attn_out"], wts["w_out"],
      wts["peer_wq"], wts["kcat"])


def _unpack_row(w):
    lo = lax.bitcast_convert_type(lax.shift_left(w, 16), jnp.float32)
    hi = lax.bitcast_convert_type(jnp.bitwise_and(w, jnp.int32(-65536)), jnp.float32)
    return lo, hi


def _gelu_tanh(x):
    c = math.sqrt(2.0 / math.pi)
    return 0.5 * x * (1.0 + jnp.tanh(c * (x + 0.044715 * (x * x * x))))


def _gather_rows(e_row, tab_ref, pb):
    for k in range(N_PICKS):
        r = pl.multiple_of(e_row[k], ROW_SUB)
        pb[k * ROW_SUB:(k + 1) * ROW_SUB, :] = tab_ref[pl.ds(r, ROW_SUB), :]


def _pick_block(pb, s):
    return pb[pl.ds(s, N_PICKS, stride=ROW_SUB), :]


def _peer_u_kernel(e_ref, x_ref, g_ref, tab_ref, w_ref, pbuf0, pbuf1, rbuf, abuf):
    tt = g_ref.shape[0]
    pbufs = (pbuf0, pbuf1)

    def tok(i, carry):
        for u in range(TOK_UNROLL_U):
            t = i * TOK_UNROLL_U + u
            pb = pbufs[u % 2]
            x_blk = x_ref[pl.ds(pl.multiple_of(t * SUBLANES, SUBLANES), SUBLANES), :]
            _gather_rows(e_ref.at[t], tab_ref, pb)
            acc = None
            for s in range(ROW_SUB):
                lo, hi = _unpack_row(_pick_block(pb, s))
                term = lo * x_blk[s:s + 1, :] + hi * x_blk[ROW_SUB + s:ROW_SUB + s + 1, :]
                acc = term if acc is None else acc + term
            rbuf[pl.ds(pl.multiple_of(t * N_PICKS, N_PICKS), N_PICKS), :] = acc
        return carry

    lax.fori_loop(0, tt // TOK_UNROLL_U, tok, 0)

    ones2 = jnp.ones((2 * LANES, LANES), jnp.bfloat16)
    lane = lax.broadcasted_iota(jnp.int32, (N_PICKS, tt), 1)
    abuf[...] = jnp.zeros_like(abuf)
    rows = RED_TOKENS * N_PICKS

    def red(i, carry):
        rb = rbuf[pl.ds(pl.multiple_of(i * rows, rows), rows), :]
        r_hi = rb.astype(jnp.bfloat16)
        r_lo = (rb - r_hi.astype(jnp.float32)).astype(jnp.bfloat16)
        res = jnp.dot(jnp.concatenate([r_hi, r_lo], axis=1), ones2, preferred_element_type=jnp.float32)
        at = abuf[...]
        for u in range(RED_TOKENS):
            at = jnp.where(lane == i * RED_TOKENS + u, res[u * N_PICKS:(u + 1) * N_PICKS, :], at)
        abuf[...] = at
        return carry

    lax.fori_loop(0, tt // RED_TOKENS, red, 0)
    w_ref[...] = g_ref[...] * _gelu_tanh(abuf[...].T)


def _peer_u(e, h2_rows, g, tab):
    n = e.shape[0]
    tt = TT_PEER
    return pl.pallas_call(
        _peer_u_kernel,
        grid=(n // tt,),
        in_specs=[pl.BlockSpec((tt, N_PICKS), lambda i: (i, 0), memory_space=pltpu.SMEM),
                  pl.BlockSpec((tt * SUBLANES, LANES), lambda i: (i, 0)),
                  pl.BlockSpec((tt, N_PICKS), lambda i: (i, 0)),
                  pl.BlockSpec(memory_space=pltpu.VMEM)],
        out_specs=pl.BlockSpec((tt, N_PICKS), lambda i: (i, 0)),
        out_shape=jax.ShapeDtypeStruct((n, N_PICKS), jnp.float32),
        scratch_shapes=[pltpu.VMEM((N_PICKS * ROW_SUB, LANES), jnp.int32),
                        pltpu.VMEM((N_PICKS * ROW_SUB, LANES), jnp.int32),
                        pltpu.VMEM((tt * N_PICKS, LANES), jnp.float32),
                        pltpu.VMEM((N_PICKS, tt), jnp.float32)],
        compiler_params=_cparams(("arbitrary",)),
        name="peer_u",
    )(e, h2_rows, g, tab)


def _peer_v_kernel(e_ref, w_ref, x_ref, gate_ref, tab_ref, y_ref, pbuf0, pbuf1, whi_ref, wlo_ref, acc_ref):
    tt = x_ref.shape[0]
    pbufs = (pbuf0, pbuf1)
    w = w_ref[...]
    w_hi = w.astype(jnp.bfloat16)
    whi_ref[...] = w_hi
    wlo_ref[...] = (w - w_hi.astype(jnp.float32)).astype(jnp.bfloat16)
    sub = lax.broadcasted_iota(jnp.int32, (BF16_ROWS, D_MODEL), 0)
    acc_ref[...] = jnp.zeros_like(acc_ref)

    def process(pb, t):
        b0 = pl.multiple_of(lax.shift_left(lax.shift_right_logical(t, 4), 4), BF16_ROWS)
        lhs = jnp.concatenate([whi_ref[pl.ds(b0, BF16_ROWS), :], wlo_ref[pl.ds(b0, BF16_ROWS), :]], axis=0)
        los, his = [], []
        for s in range(ROW_SUB):
            lo, hi = _unpack_row(_pick_block(pb, s))
            los.append(lo.astype(jnp.bfloat16))
            his.append(hi.astype(jnp.bfloat16))
        rhs = jnp.concatenate(los + his, axis=1)
        res = jnp.dot(lhs, rhs, preferred_element_type=jnp.float32)
        r = res[:BF16_ROWS, :] + res[BF16_ROWS:, :]
        mask = sub == jnp.bitwise_and(t, BF16_ROWS - 1)
        acc_ref[pl.ds(b0, BF16_ROWS), :] = jnp.where(mask, r, acc_ref[pl.ds(b0, BF16_ROWS), :])

    _gather_rows(e_ref.at[0], tab_ref, pbuf0)

    def step(i, carry):
        t0 = i * TOK_UNROLL_V
        for u in range(TOK_UNROLL_V):
            process(pbufs[u % 2], t0 + u)
            _gather_rows(e_ref.at[jnp.minimum(t0 + u + 1, tt - 1)], tab_ref, pbufs[(u + 1) % 2])
        return carry

    lax.fori_loop(0, tt // TOK_UNROLL_V, step, 0)
    y_ref[...] = x_ref[...] + gate_ref[...] * acc_ref[...]


def _peer_v(e, w, x1, gate2, tab, seq):
    n = e.shape[0]
    tt = TT_PEER
    return pl.pallas_call(
        _peer_v_kernel,
        grid=(n // tt,),
        in_specs=[pl.BlockSpec((tt, N_PICKS), lambda i: (i, 0), memory_space=pltpu.SMEM),
                  pl.BlockSpec((tt, N_PICKS), lambda i: (i, 0)),
                  pl.BlockSpec((tt, D_MODEL), lambda i: (i, 0)),
                  pl.BlockSpec((None, 1, D_MODEL), lambda i: ((i * tt) // seq, 0, 0)),
                  pl.BlockSpec(memory_space=pltpu.VMEM)],
        out_specs=pl.BlockSpec((tt, D_MODEL), lambda i: (i, 0)),
        out_shape=jax.ShapeDtypeStruct((n, D_MODEL), jnp.float32),
        scratch_shapes=[pltpu.VMEM((N_PICKS * ROW_SUB, LANES), jnp.int32),
                        pltpu.VMEM((N_PICKS * ROW_SUB, LANES), jnp.int32),
                        pltpu.VMEM((tt, N_PICKS), jnp.bfloat16),
                        pltpu.VMEM((tt, N_PICKS), jnp.bfloat16),
                        pltpu.VMEM((tt, D_MODEL), jnp.float32)],
        compiler_params=_cparams(("arbitrary",)),
        name="peer_v",
    )(e, w, x1, gate2, tab)


def _pad_heads(w, used, lead=0):
    r = w.shape[0]
    w3 = w.reshape(r, N_HEADS, used)
    out = jnp.zeros((r, N_HEADS, HEAD_SLOT), w.dtype).at[:, :, lead:lead + used].set(w3)
    return out.reshape(r, N_HEADS * HEAD_SLOT)


def _pack_table(t):
    tb = t.astype(jnp.bfloat16)
    lo = lax.bitcast_convert_type(tb[:, :ROW_WORDS], jnp.uint16).astype(jnp.uint32)
    hi = lax.bitcast_convert_type(tb[:, ROW_WORDS:], jnp.uint16).astype(jnp.uint32)
    words = lax.bitcast_convert_type(lo | (hi << 16), jnp.int32)
    return words.reshape(t.shape[0] * ROW_SUB, LANES)


def _prep_weights(g_norm1, w_in, conv_w, w_conv_out, g_q_lora, w_uq, g_kv_lora, w_ukv, g_qnorm,
                  g_knorm, w_attn_out, w_out, g_norm2, peer_wq, peer_k1, peer_k2, peer_u, peer_v):
    bf = jnp.bfloat16
    o1 = 3 * D_CONV
    o2 = o1 + Q_LORA
    o3 = o2 + KV_LORA
    o4 = o3 + QK_ROPE
    o5 = o4 + D_MODEL
    kr_slot = jnp.zeros((D_MODEL, HEAD_SLOT), w_in.dtype).at[:, QK_NOPE:QK_HEAD].set(w_in[:, o3:o4])
    w_in_r = jnp.concatenate([w_in[:, :o3], kr_slot, w_in[:, o4:]], axis=1).astype(bf)
    w_ukv3 = w_ukv.reshape(KV_LORA, N_HEADS, QK_NOPE + V_HEAD)
    w_uk = _pad_heads(w_ukv3[:, :, :QK_NOPE].reshape(KV_LORA, -1), QK_NOPE)
    w_uv = _pad_heads(w_ukv3[:, :, QK_NOPE:].reshape(KV_LORA, -1), V_HEAD)
    pad_gain = lambda g: jnp.zeros((1, HEAD_SLOT), jnp.float32).at[0, :QK_HEAD].set(g)
    w_ao = jnp.zeros((N_HEADS, HEAD_SLOT, D_MODEL), w_attn_out.dtype).at[:, :V_HEAD].set(
        w_attn_out.reshape(N_HEADS, V_HEAD, D_MODEL)).reshape(N_HEADS * HEAD_SLOT, D_MODEL)
    half = PEER_DK // 2
    kcat = jnp.zeros((PEER_HEADS, 2 * N_KEYS, PEER_DK), jnp.float32)
    kcat = kcat.at[:, :N_KEYS, :half].set(peer_k1).at[:, N_KEYS:, half:].set(peer_k2)
    return {
        "g_norm1": g_norm1.reshape(1, -1), "w_in": w_in_r, "conv_w": conv_w,
        "w_conv_out": w_conv_out.astype(bf), "g_q_lora": g_q_lora.reshape(1, -1),
        "w_uq": _pad_heads(w_uq, QK_HEAD).astype(bf), "g_kv_lora": g_kv_lora.reshape(1, -1),
        "w_uk": w_uk.astype(bf), "w_uv": w_uv.astype(bf),
        "g_qnorm": pad_gain(g_qnorm), "g_knorm": pad_gain(g_knorm),
        "w_attn_out": w_ao.astype(bf), "w_out": w_out.astype(bf), "g_norm2": g_norm2.reshape(1, -1),
        "peer_wq": peer_wq.astype(bf), "kcat": kcat.astype(bf),
        "tab_u": _pack_table(peer_u), "tab_v": _pack_table(peer_v),
    }


def _rope_tables(seq):
    pos = jnp.arange(seq, dtype=jnp.float32)
    inv = ROPE_THETA ** (-jnp.arange(0, QK_ROPE, 2, dtype=jnp.float32) / QK_ROPE)
    ang = pos[:, None] * inv[None, :]
    cos, sin = jnp.cos(ang), jnp.sin(ang)
    t1 = slice(QK_NOPE, QK_NOPE + HALF_ROPE)
    t2 = slice(QK_NOPE + HALF_ROPE, QK_HEAD)
    cos_t = jnp.ones((seq, HEAD_SLOT), jnp.float32).at[:, t1].set(cos).at[:, t2].set(cos)
    sin_a = jnp.zeros((seq, HEAD_SLOT), jnp.float32).at[:, t1].set(-sin)
    sin_b = jnp.zeros((seq, HEAD_SLOT), jnp.float32).at[:, t2].set(sin)
    return cos_t, sin_a, sin_b


def _encoder_layer(x, c, w_ada, b_ada, wts):
    bsz, seq, d = x.shape
    n = bsz * seq
    ada = _ada(c, w_ada, b_ada)[:, None, :]
    shift1, scale1, gate1, shift2, scale2, gate2 = jnp.split(ada, 6, axis=-1)
    q, k, v, a, sgb = _inproj(x, scale1, shift1, wts, _rope_tables(seq))
    o = _attention(q, k, v)
    x1, h2, e, g = _mix(x, a, sgb, o, gate1, scale2, shift2, wts)
    e = e.reshape(n, N_PICKS)
    w = _peer_u(e, h2.reshape(n * SUBLANES, LANES), g.reshape(n, N_PICKS), wts["tab_u"])
    y = _peer_v(e, w, x1.reshape(n, d), gate2, wts["tab_v"], seq)
    return y.reshape(bsz, seq, d)


def kernel(x_prompt, x_sample, c_prompt, c_sample, w_ada, b_ada, g_norm1, w_in, conv_w, w_conv_out, g_q_lora, w_uq, g_kv_lora, w_ukv, g_qnorm, g_knorm, w_attn_out, w_out, g_norm2, peer_wq, peer_k1, peer_k2, peer_u, peer_v):
    wts = _prep_weights(g_norm1[0], w_in[0], conv_w[0], w_conv_out[0], g_q_lora[0], w_uq[0],
                        g_kv_lora[0], w_ukv[0], g_qnorm[0], g_knorm[0], w_attn_out[0], w_out[0],
                        g_norm2[0], peer_wq[0], peer_k1[0], peer_k2[0], peer_u[0], peer_v[0])
    return (_encoder_layer(x_prompt, c_prompt, w_ada[0], b_ada[0], wts),
            _encoder_layer(x_sample, c_sample, w_ada[0], b_ada[0], wts))
```

```python
import functools
import math

import jax
import jax.numpy as jnp
import numpy as np
from jax import lax
from jax.experimental import pallas as pl
from jax.experimental.pallas import tpu as pltpu

D_MODEL = 1024
D_CONV = 512
N_HEADS = 8
QK_NOPE = 64
QK_ROPE = 32
V_HEAD = 64
Q_LORA = 256
KV_LORA = 128
QK_HEAD = QK_NOPE + QK_ROPE
ROPE_THETA = 10000.0
PEER_HEADS = 8
N_KEYS = 128
PEER_DK = 128
PEER_TOPK = 16
EPS = 1e-6

LANES = 128
SUBLANES = 8
HEAD_SLOT = LANES
HALF_ROPE = QK_ROPE // 2
N_PICKS = PEER_HEADS * PEER_TOPK
ROW_WORDS = D_MODEL // 2
ROW_SUB = ROW_WORDS // LANES
VMEM_LIMIT = 56 * 1024 * 1024

C_BG, C_CG, C_HC = 0, 512, 1024
C_CQ = 1536
C_CKV = C_CQ + Q_LORA
C_KR = C_CKV + KV_LORA
C_GA = C_KR + HEAD_SLOT
C_GB = C_GA + D_MODEL
IN_COLS_R = C_GB + D_MODEL

TS_IN = 256
TQ = 512
TK = 2048
KV_UNROLL = 2
TS_MIX = 256
TT_PEER = 128
TOK_UNROLL_U = 2
TOK_UNROLL_V = 4
RED_TOKENS = 8
BF16_ROWS = 16

_NEG_INF = float("-inf")


def _cparams(sem):
    return pltpu.CompilerParams(dimension_semantics=sem, vmem_limit_bytes=VMEM_LIMIT)


def _ada_kernel(c_ref, w_ref, b_ref, o_ref):
    o_ref[...] = jnp.dot(c_ref[...], w_ref[...], preferred_element_type=jnp.float32,
                         precision=lax.Precision.HIGHEST) + b_ref[...]


def _ada(c, w_ada, b_ada):
    bsz = c.shape[0]
    rows = -(-bsz // SUBLANES) * SUBLANES
    c_p = jnp.zeros((rows, D_MODEL), jnp.float32).at[:bsz].set(c)
    out = pl.pallas_call(
        _ada_kernel,
        grid=(6,),
        in_specs=[pl.BlockSpec((rows, D_MODEL), lambda j: (0, 0)),
                  pl.BlockSpec((D_MODEL, D_MODEL), lambda j: (0, j)),
                  pl.BlockSpec((1, D_MODEL), lambda j: (0, j))],
        out_specs=pl.BlockSpec((rows, D_MODEL), lambda j: (0, j)),
        out_shape=jax.ShapeDtypeStruct((rows, 6 * D_MODEL), jnp.float32),
        compiler_params=_cparams(("arbitrary",)),
        name="ada",
    )(c_p, w_ada, b_ada.reshape(1, -1))
    return out[:bsz]


def _adaln(x, g, scale, shift):
    ms = jnp.mean(x * x, axis=-1, keepdims=True)
    return (x * lax.rsqrt(ms + EPS) * g) * (1.0 + scale) + shift


def _rms_rows(x, g, n):
    ms = jnp.sum(x * x, axis=-1, keepdims=True) * (1.0 / n)
    return x * lax.rsqrt(ms + EPS) * g


def _rope(x, cos, sin_a, sin_b):
    return (x * cos + pltpu.roll(x, LANES - HALF_ROPE, axis=1) * sin_a
            + pltpu.roll(x, HALF_ROPE, axis=1) * sin_b)


def _inproj_kernel(x_ref, xp_ref, xn_ref, scale_ref, shift_ref, g1_ref, w_in_ref, conv_w_ref,
                   w_co_ref, gql_ref, w_uq_ref, gkvl_ref, w_uk_ref, w_uv_ref, gqn_ref, gkn_ref,
                   cos_ref, sa_ref, sb_ref,
                   q_ref, k_ref, v_ref, a_ref, sgb_ref):
    i = pl.program_id(1)
    n_i = pl.num_programs(1)
    g1 = g1_ref[...]
    scale = scale_ref[...]
    shift = shift_ref[...]
    ts = x_ref.shape[0]

    h = _adaln(x_ref[...], g1, scale, shift).astype(jnp.bfloat16)

    def proj(lo, width):
        return jnp.dot(h, w_in_ref[:, lo:lo + width], preferred_element_type=jnp.float32)

    z = proj(C_CG, D_CONV) * proj(C_HC, D_CONV)

    def halo_z(xh_ref):
        hh = _adaln(xh_ref[...], g1, scale, shift).astype(jnp.bfloat16)
        zc = jnp.dot(hh, w_in_ref[:, C_CG:C_CG + 2 * D_CONV], preferred_element_type=jnp.float32)
        return zc[:, :D_CONV] * zc[:, D_CONV:]

    z_prev = halo_z(xp_ref)[SUBLANES - 1:SUBLANES, :]
    z_next = halo_z(xn_ref)[0:1, :]
    z_prev = jnp.where(i == 0, 0.0, z_prev)
    z_next = jnp.where(i == n_i - 1, 0.0, z_next)
    row = lax.broadcasted_iota(jnp.int32, (ts, D_CONV), 0)
    z_up = jnp.where(row == 0, z_prev, pltpu.roll(z, 1, axis=0))
    z_dn = jnp.where(row == ts - 1, z_next, pltpu.roll(z, ts - 1, axis=0))
    cw = conv_w_ref[...]
    y = z_up * cw[0:1, :] + z * cw[1:2, :] + z_dn * cw[2:3, :]
    out_a = jnp.dot((proj(C_BG, D_CONV) * y).astype(jnp.bfloat16), w_co_ref[...],
                    preferred_element_type=jnp.float32)
    a_ref[...] = jax.nn.sigmoid(proj(C_GA, D_MODEL)) * out_a
    sgb_ref[...] = jax.nn.sigmoid(proj(C_GB, D_MODEL))

    cos = cos_ref[...]
    sin_a = sa_ref[...]
    sin_b = sb_ref[...]
    cq = _rms_rows(proj(C_CQ, Q_LORA), gql_ref[...], Q_LORA).astype(jnp.bfloat16)
    qf = jnp.dot(cq, w_uq_ref[...], preferred_element_type=jnp.float32)
    ckv = _rms_rows(proj(C_CKV, KV_LORA), gkvl_ref[...], KV_LORA).astype(jnp.bfloat16)
    kf = jnp.dot(ckv, w_uk_ref[...], preferred_element_type=jnp.float32)
    lane = lax.broadcasted_iota(jnp.int32, (1, N_HEADS * HEAD_SLOT), 1)
    ones_lane = (jnp.bitwise_and(lane, HEAD_SLOT - 1) == V_HEAD).astype(jnp.float32)
    v_ref[...] = (jnp.dot(ckv, w_uv_ref[...], preferred_element_type=jnp.float32)
                  + ones_lane).astype(jnp.bfloat16)
    kr = proj(C_KR, HEAD_SLOT)
    gqn = gqn_ref[...]
    gkn = gkn_ref[...]
    q_scale = QK_HEAD ** -0.5 * math.log2(math.e)
    for hd in range(N_HEADS):
        sl = slice(hd * HEAD_SLOT, (hd + 1) * HEAD_SLOT)
        qh = _rope(_rms_rows(qf[:, sl], gqn, QK_HEAD), cos, sin_a, sin_b)
        q_ref[:, sl] = (qh * q_scale).astype(jnp.bfloat16)
        kh = _rope(_rms_rows(kf[:, sl] + kr, gkn, QK_HEAD), cos, sin_a, sin_b)
        k_ref[:, sl] = kh.astype(jnp.bfloat16)


def _inproj(x, scale1, shift1, wts, rope):
    bsz, seq, _ = x.shape
    ts = min(TS_IN, seq)
    n_i = seq // ts
    nb8 = seq // SUBLANES
    per8 = ts // SUBLANES
    tile = lambda b, i: (b, i, 0)
    per_b = lambda b, i: (b, 0, 0)
    full2 = lambda b, i: (0, 0)
    hw = N_HEADS * HEAD_SLOT
    in_specs = [
        pl.BlockSpec((None, ts, D_MODEL), tile),
        pl.BlockSpec((None, SUBLANES, D_MODEL), lambda b, i: (b, jnp.maximum(i * per8 - 1, 0), 0)),
        pl.BlockSpec((None, SUBLANES, D_MODEL), lambda b, i: (b, jnp.minimum((i + 1) * per8, nb8 - 1), 0)),
        pl.BlockSpec((None, 1, D_MODEL), per_b),
        pl.BlockSpec((None, 1, D_MODEL), per_b),
        pl.BlockSpec((1, D_MODEL), full2),
        pl.BlockSpec((D_MODEL, IN_COLS_R), full2),
        pl.BlockSpec((3, D_CONV), full2),
        pl.BlockSpec((D_CONV, D_MODEL), full2),
        pl.BlockSpec((1, Q_LORA), full2),
        pl.BlockSpec((Q_LORA, hw), full2),
        pl.BlockSpec((1, KV_LORA), full2),
        pl.BlockSpec((KV_LORA, hw), full2),
        pl.BlockSpec((KV_LORA, hw), full2),
        pl.BlockSpec((1, HEAD_SLOT), full2),
        pl.BlockSpec((1, HEAD_SLOT), full2),
        pl.BlockSpec((ts, HEAD_SLOT), lambda b, i: (i, 0)),
        pl.BlockSpec((ts, HEAD_SLOT), lambda b, i: (i, 0)),
        pl.BlockSpec((ts, HEAD_SLOT), lambda b, i: (i, 0)),
    ]
    out_specs = [pl.BlockSpec((None, ts, hw), tile)] * 3 + [pl.BlockSpec((None, ts, D_MODEL), tile)] * 2
    out_shape = ([jax.ShapeDtypeStruct((bsz, seq, hw), jnp.bfloat16)] * 3
                 + [jax.ShapeDtypeStruct((bsz, seq, D_MODEL), jnp.float32)] * 2)
    return pl.pallas_call(
        _inproj_kernel,
        grid=(bsz, n_i),
        in_specs=in_specs,
        out_specs=out_specs,
        out_shape=out_shape,
        compiler_params=_cparams(("parallel", "parallel")),
        name="inproj",
    )(x, x, x, scale1, shift1, wts["g_norm1"], wts["w_in"], wts["conv_w"], wts["w_conv_out"],
      wts["g_q_lora"], wts["w_uq"], wts["g_kv_lora"], wts["w_uk"], wts["w_uv"],
      wts["g_qnorm"], wts["g_knorm"], rope[0], rope[1], rope[2])


def _attn_kernel(q_ref, k_ref, v_ref, o_ref):
    q = q_ref[...]
    tq = q.shape[0]
    seq = k_ref.shape[0]
    tk = min(TK, seq)

    n_chunks = seq // tk
    unroll = KV_UNROLL if n_chunks % KV_UNROLL == 0 else 1

    def chunk(off, carry):
        m, acc = carry
        kc = k_ref[pl.ds(off, tk), :]
        vc = v_ref[pl.ds(off, tk), :]
        s = lax.dot_general(q, kc, (((1,), (1,)), ((), ())), preferred_element_type=jnp.float32)
        m_new = jnp.maximum(m, jnp.max(s, axis=-1, keepdims=True))
        p = jnp.exp2(s - m_new).astype(jnp.bfloat16)
        alpha = jnp.exp2(m - m_new)
        acc = alpha * acc + jnp.dot(p, vc, preferred_element_type=jnp.float32)
        return m_new, acc

    def body(j, carry):
        for u in range(unroll):
            carry = chunk(pl.multiple_of((j * unroll + u) * tk, tk), carry)
        return carry

    m0 = jnp.full((tq, 1), _NEG_INF, jnp.float32)
    acc0 = jnp.zeros((tq, HEAD_SLOT), jnp.float32)
    _, acc = lax.fori_loop(0, n_chunks // unroll, body, (m0, acc0))
    o_ref[...] = (acc / acc[:, V_HEAD:V_HEAD + 1]).astype(jnp.bfloat16)


def _attention(q, k, v):
    bsz, seq, hw = q.shape
    tq = min(TQ, seq)
    return pl.pallas_call(
        _attn_kernel,
        grid=(bsz, N_HEADS, seq // tq),
        in_specs=[pl.BlockSpec((None, tq, HEAD_SLOT), lambda b, h, i: (b, i, h)),
                  pl.BlockSpec((None, seq, HEAD_SLOT), lambda b, h, i: (b, 0, h)),
                  pl.BlockSpec((None, seq, HEAD_SLOT), lambda b, h, i: (b, 0, h))],
        out_specs=pl.BlockSpec((None, tq, HEAD_SLOT), lambda b, h, i: (b, i, h)),
        out_shape=jax.ShapeDtypeStruct((bsz, seq, hw), jnp.bfloat16),
        compiler_params=_cparams(("parallel", "parallel", "arbitrary")),
        name="attn",
    )(q, k, v)


def _topk_rows(s, iota):
    vals, idxs = [], []
    big = jnp.float32(1e9)
    for _ in range(PEER_TOPK):
        m = jnp.max(s, axis=0, keepdims=True)
        am = jnp.min(jnp.where(s == m, iota, big), axis=0, keepdims=True)
        vals.append(m)
        idxs.append(am)
        s = jnp.where(iota == am, _NEG_INF, s)
    return jnp.concatenate(vals, axis=0), jnp.concatenate(idxs, axis=0)


def _take_rows(tab, idx):
    out = jnp.zeros_like(tab)
    for a in range(tab.shape[0]):
        out = out + jnp.where(idx == jnp.float32(a), tab[a:a + 1, :], 0.0)
    return out


_CAND = [(a, b) for a in range(PEER_TOPK) for b in range(PEER_TOPK) if (a + 1) * (b + 1) <= PEER_TOPK]
N_CAND = -(-len(_CAND) // SUBLANES) * SUBLANES
_PAD_LABEL = float(PEER_TOPK * PEER_TOPK)


def _candidate_tables():
    sel = np.zeros((2, N_CAND, PEER_TOPK), np.float32)
    lab = np.full((N_CAND, 1), _PAD_LABEL, np.float32)
    for r, (a, b) in enumerate(_CAND):
        sel[0, r, a] = 1.0
        sel[1, r, b] = 1.0
        lab[r, 0] = a * PEER_TOPK + b
    return sel, lab


def _mix_kernel(x_ref, a_ref, sgb_ref, o_ref, gate1_ref, scale2_ref, shift2_ref, g2_ref,
                w_ao_ref, w_out_ref, w_pq_ref, kcat_ref, sel_ref, lab_ref,
                x1_ref, h2_ref, e_ref, g_ref):
    ts = x_ref.shape[0]
    out_b = jnp.dot(o_ref[...], w_ao_ref[...], preferred_element_type=jnp.float32)
    merged = (a_ref[...] + sgb_ref[...] * out_b).astype(jnp.bfloat16)
    x1 = x_ref[...] + gate1_ref[...] * jnp.dot(merged, w_out_ref[...],
                                              preferred_element_type=jnp.float32)
    x1_ref[...] = x1
    h2 = _adaln(x1, g2_ref[...], scale2_ref[...], shift2_ref[...])
    h2_ref[...] = h2
    qp = jnp.dot(h2.astype(jnp.bfloat16), w_pq_ref[...],
                 preferred_element_type=jnp.float32).astype(jnp.bfloat16)

    iota_k = lax.broadcasted_iota(jnp.int32, (N_KEYS, ts), 0).astype(jnp.float32)
    lab = lab_ref[...]
    valid = lab < _PAD_LABEL
    sel_a = sel_ref[0]
    sel_b = sel_ref[1]
    exact = dict(preferred_element_type=jnp.float32, precision=lax.Precision.HIGHEST)
    e_rows, g_rows = [], []
    for hd in range(PEER_HEADS):
        qh = qp[:, hd * PEER_DK:(hd + 1) * PEER_DK]
        st = lax.dot_general(kcat_ref[hd], qh, (((1,), (1,)), ((), ())),
                             preferred_element_type=jnp.float32)
        v1, i1 = _topk_rows(st[:N_KEYS], iota_k)
        v2, i2 = _topk_rows(st[N_KEYS:], iota_k)
        comb = jnp.dot(sel_a, v1, **exact) + jnp.dot(sel_b, v2, **exact)
        sc, pos = _topk_rows(jnp.where(valid, comb, _NEG_INF), lab)
        ia = jnp.floor(pos * (1.0 / PEER_TOPK))
        ib = pos - ia * PEER_TOPK
        e_rows.append(_take_rows(i1, ia) * N_KEYS + _take_rows(i2, ib))
        p = jnp.exp(sc - sc[0:1, :])
        g_rows.append(p / jnp.sum(p, axis=0, keepdims=True))
    e_all = jnp.concatenate(e_rows, axis=0)
    g_all = jnp.concatenate(g_rows, axis=0)
    for c in range(ts // LANES):
        cs = slice(c * LANES, (c + 1) * LANES)
        e_ref[cs, :] = (e_all[:, cs].T * ROW_SUB).astype(jnp.int32)
        g_ref[cs, :] = g_all[:, cs].T


def _mix(x, a, sgb, o, gate1, scale2, shift2, wts):
    bsz, seq, _ = x.shape
    ts = min(TS_MIX, seq)
    tile = lambda b, i: (b, i, 0)
    per_b = lambda b, i: (b, 0, 0)
    full2 = lambda b, i: (0, 0)
    hw = N_HEADS * HEAD_SLOT
    big = pl.BlockSpec((None, ts, D_MODEL), tile)
    vec = pl.BlockSpec((None, 1, D_MODEL), per_b)
    picks = pl.BlockSpec((None, ts, N_PICKS), tile)
    sel, lab = _candidate_tables()
    return pl.pallas_call(
        _mix_kernel,
        grid=(bsz, seq // ts),
        in_specs=[big, big, big, pl.BlockSpec((None, ts, hw), tile), vec, vec, vec,
                  pl.BlockSpec((1, D_MODEL), full2),
                  pl.BlockSpec((hw, D_MODEL), full2),
                  pl.BlockSpec((D_MODEL, D_MODEL), full2),
                  pl.BlockSpec((D_MODEL, PEER_HEADS * PEER_DK), full2),
                  pl.BlockSpec((PEER_HEADS, 2 * N_KEYS, PEER_DK), lambda b, i: (0, 0, 0)),
                  pl.BlockSpec((2, N_CAND, PEER_TOPK), lambda b, i: (0, 0, 0)),
                  pl.BlockSpec((N_CAND, ts), full2)],
        out_specs=[big, big, picks, picks],
        out_shape=[jax.ShapeDtypeStruct((bsz, seq, D_MODEL), jnp.float32),
                   jax.ShapeDtypeStruct((bsz, seq, D_MODEL), jnp.float32),
                   jax.ShapeDtypeStruct((bsz, seq, N_PICKS), jnp.int32),
                   jax.ShapeDtypeStruct((bsz, seq, N_PICKS), jnp.float32)],
        compiler_params=_cparams(("parallel", "parallel")),
        name="mix",
    )(x, a, sgb, o, gate1, scale2, shift2, wts["g_norm2"], wts["w_attn_out"], wts["w_out"],
      wts["peer_wq"], wts["kcat"], jnp.asarray(sel), jnp.broadcast_to(jnp.asarray(lab), (N_CAND, ts)))


def _unpack_row(w):
    lo = lax.bitcast_convert_type(lax.shift_left(w, 16), jnp.float32)
    hi = lax.bitcast_convert_type(jnp.bitwise_and(w, jnp.int32(-65536)), jnp.float32)
    return lo, hi


def _gelu_tanh(x):
    c = math.sqrt(2.0 / math.pi)
    return 0.5 * x * (1.0 + jnp.tanh(c * (x + 0.044715 * (x * x * x))))


def _gather_rows(e_row, tab_ref, pb):
    for k in range(N_PICKS):
        r = pl.multiple_of(e_row[k], ROW_SUB)
        pb[k * ROW_SUB:(k + 1) * ROW_SUB, :] = tab_ref[pl.ds(r, ROW_SUB), :]


def _pick_block(pb, s):
    return pb[pl.ds(s, N_PICKS, stride=ROW_SUB), :]


def _peer_u_kernel(e_ref, x_ref, g_ref, tab_ref, w_ref, pbuf0, pbuf1, rbuf, abuf):
    tt = g_ref.shape[0]
    pbufs = (pbuf0, pbuf1)

    def tok(i, carry):
        for u in range(TOK_UNROLL_U):
            t = i * TOK_UNROLL_U + u
            pb = pbufs[u % 2]
            x_blk = x_ref[pl.ds(pl.multiple_of(t * SUBLANES, SUBLANES), SUBLANES), :]
            _gather_rows(e_ref.at[t], tab_ref, pb)
            acc = None
            for s in range(ROW_SUB):
                lo, hi = _unpack_row(_pick_block(pb, s))
                term = lo * x_blk[s:s + 1, :] + hi * x_blk[ROW_SUB + s:ROW_SUB + s + 1, :]
                acc = term if acc is None else acc + term
            rbuf[pl.ds(pl.multiple_of(t * N_PICKS, N_PICKS), N_PICKS), :] = acc
        return carry

    lax.fori_loop(0, tt // TOK_UNROLL_U, tok, 0)

    ones2 = jnp.ones((2 * LANES, LANES), jnp.bfloat16)
    lane = lax.broadcasted_iota(jnp.int32, (N_PICKS, tt), 1)
    abuf[...] = jnp.zeros_like(abuf)
    rows = RED_TOKENS * N_PICKS

    def red(i, carry):
        rb = rbuf[pl.ds(pl.multiple_of(i * rows, rows), rows), :]
        r_hi = rb.astype(jnp.bfloat16)
        r_lo = (rb - r_hi.astype(jnp.float32)).astype(jnp.bfloat16)
        res = jnp.dot(jnp.concatenate([r_hi, r_lo], axis=1), ones2, preferred_element_type=jnp.float32)
        at = abuf[...]
        for u in range(RED_TOKENS):
            at = jnp.where(lane == i * RED_TOKENS + u, res[u * N_PICKS:(u + 1) * N_PICKS, :], at)
        abuf[...] = at
        return carry

    lax.fori_loop(0, tt // RED_TOKENS, red, 0)
    w_ref[...] = g_ref[...] * _gelu_tanh(abuf[...].T)


def _peer_u(e, h2_rows, g, tab):
    n = e.shape[0]
    tt = TT_PEER
    return pl.pallas_call(
        _peer_u_kernel,
        grid=(n // tt,),
        in_specs=[pl.BlockSpec((tt, N_PICKS), lambda i: (i, 0), memory_space=pltpu.SMEM),
                  pl.BlockSpec((tt * SUBLANES, LANES), lambda i: (i, 0)),
                  pl.BlockSpec((tt, N_PICKS), lambda i: (i, 0)),
                  pl.BlockSpec(memory_space=pltpu.VMEM)],
        out_specs=pl.BlockSpec((tt, N_PICKS), lambda i: (i, 0)),
        out_shape=jax.ShapeDtypeStruct((n, N_PICKS), jnp.float32),
        scratch_shapes=[pltpu.VMEM((N_PICKS * ROW_SUB, LANES), jnp.int32),
                        pltpu.VMEM((N_PICKS * ROW_SUB, LANES), jnp.int32),
                        pltpu.VMEM((tt * N_PICKS, LANES), jnp.float32),
                        pltpu.VMEM((N_PICKS, tt), jnp.float32)],
        compiler_params=_cparams(("arbitrary",)),
        name="peer_u",
    )(e, h2_rows, g, tab)


def _peer_v_kernel(e_ref, w_ref, x_ref, gate_ref, tab_ref, y_ref, pbuf0, pbuf1, whi_ref, wlo_ref, acc_ref):
    tt = x_ref.shape[0]
    pbufs = (pbuf0, pbuf1)
    w = w_ref[...]
    w_hi = w.astype(jnp.bfloat16)
    whi_ref[...] = w_hi
    wlo_ref[...] = (w - w_hi.astype(jnp.float32)).astype(jnp.bfloat16)
    sub = lax.broadcasted_iota(jnp.int32, (BF16_ROWS, D_MODEL), 0)
    acc_ref[...] = jnp.zeros_like(acc_ref)

    def process(pb, t):
        b0 = pl.multiple_of(lax.shift_left(lax.shift_right_logical(t, 4), 4), BF16_ROWS)
        lhs = jnp.concatenate([whi_ref[pl.ds(b0, BF16_ROWS), :], wlo_ref[pl.ds(b0, BF16_ROWS), :]], axis=0)
        los, his = [], []
        for s in range(ROW_SUB):
            lo, hi = _unpack_row(_pick_block(pb, s))
            los.append(lo.astype(jnp.bfloat16))
            his.append(hi.astype(jnp.bfloat16))
        rhs = jnp.concatenate(los + his, axis=1)
        res = jnp.dot(lhs, rhs, preferred_element_type=jnp.float32)
        r = res[:BF16_ROWS, :] + res[BF16_ROWS:, :]
        mask = sub == jnp.bitwise_and(t, BF16_ROWS - 1)
        acc_ref[pl.ds(b0, BF16_ROWS), :] = jnp.where(mask, r, acc_ref[pl.ds(b0, BF16_ROWS), :])

    _gather_rows(e_ref.at[0], tab_ref, pbuf0)

    def step(i, carry):
        t0 = i * TOK_UNROLL_V
        for u in range(TOK_UNROLL_V):
            process(pbufs[u % 2], t0 + u)
            _gather_rows(e_ref.at[jnp.minimum(t0 + u + 1, tt - 1)], tab_ref, pbufs[(u + 1) % 2])
        return carry

    lax.fori_loop(0, tt // TOK_UNROLL_V, step, 0)
    y_ref[...] = x_ref[...] + gate_ref[...] * acc_ref[...]


def _peer_v(e, w, x1, gate2, tab, seq):
    n = e.shape[0]
    tt = TT_PEER
    return pl.pallas_call(
        _peer_v_kernel,
        grid=(n // tt,),
        in_specs=[pl.BlockSpec((tt, N_PICKS), lambda i: (i, 0), memory_space=pltpu.SMEM),
                  pl.BlockSpec((tt, N_PICKS), lambda i: (i, 0)),
                  pl.BlockSpec((tt, D_MODEL), lambda i: (i, 0)),
                  pl.BlockSpec((None, 1, D_MODEL), lambda i: ((i * tt) // seq, 0, 0)),
                  pl.BlockSpec(memory_space=pltpu.VMEM)],
        out_specs=pl.BlockSpec((tt, D_MODEL), lambda i: (i, 0)),
        out_shape=jax.ShapeDtypeStruct((n, D_MODEL), jnp.float32),
        scratch_shapes=[pltpu.VMEM((N_PICKS * ROW_SUB, LANES), jnp.int32),
                        pltpu.VMEM((N_PICKS * ROW_SUB, LANES), jnp.int32),
                        pltpu.VMEM((tt, N_PICKS), jnp.bfloat16),
                        pltpu.VMEM((tt, N_PICKS), jnp.bfloat16),
                        pltpu.VMEM((tt, D_MODEL), jnp.float32)],
        compiler_params=_cparams(("arbitrary",)),
        name="peer_v",
    )(e, w, x1, gate2, tab)


def _pad_heads(w, used, lead=0):
    r = w.shape[0]
    w3 = w.reshape(r, N_HEADS, used)
    out = jnp.zeros((r, N_HEADS, HEAD_SLOT), w.dtype).at[:, :, lead:lead + used].set(w3)
    return out.reshape(r, N_HEADS * HEAD_SLOT)


def _pack_table(t):
    tb = t.astype(jnp.bfloat16)
    lo = lax.bitcast_convert_type(tb[:, :ROW_WORDS], jnp.uint16).astype(jnp.uint32)
    hi = lax.bitcast_convert_type(tb[:, ROW_WORDS:], jnp.uint16).astype(jnp.uint32)
    words = lax.bitcast_convert_type(lo | (hi << 16), jnp.int32)
    return words.reshape(t.shape[0] * ROW_SUB, LANES)


def _prep_weights(g_norm1, w_in, conv_w, w_conv_out, g_q_lora, w_uq, g_kv_lora, w_ukv, g_qnorm,
                  g_knorm, w_attn_out, w_out, g_norm2, peer_wq, peer_k1, peer_k2, peer_u, peer_v):
    bf = jnp.bfloat16
    o1 = 3 * D_CONV
    o2 = o1 + Q_LORA
    o3 = o2 + KV_LORA
    o4 = o3 + QK_ROPE
    o5 = o4 + D_MODEL
    kr_slot = jnp.zeros((D_MODEL, HEAD_SLOT), w_in.dtype).at[:, QK_NOPE:QK_HEAD].set(w_in[:, o3:o4])
    w_in_r = jnp.concatenate([w_in[:, :o3], kr_slot, w_in[:, o4:]], axis=1).astype(bf)
    w_ukv3 = w_ukv.reshape(KV_LORA, N_HEADS, QK_NOPE + V_HEAD)
    w_uk = _pad_heads(w_ukv3[:, :, :QK_NOPE].reshape(KV_LORA, -1), QK_NOPE)
    w_uv = _pad_heads(w_ukv3[:, :, QK_NOPE:].reshape(KV_LORA, -1), V_HEAD)
    pad_gain = lambda g: jnp.zeros((1, HEAD_SLOT), jnp.float32).at[0, :QK_HEAD].set(g)
    w_ao = jnp.zeros((N_HEADS, HEAD_SLOT, D_MODEL), w_attn_out.dtype).at[:, :V_HEAD].set(
        w_attn_out.reshape(N_HEADS, V_HEAD, D_MODEL)).reshape(N_HEADS * HEAD_SLOT, D_MODEL)
    half = PEER_DK // 2
    kcat = jnp.zeros((PEER_HEADS, 2 * N_KEYS, PEER_DK), jnp.float32)
    kcat = kcat.at[:, :N_KEYS, :half].set(peer_k1).at[:, N_KEYS:, half:].set(peer_k2)
    return {
        "g_norm1": g_norm1.reshape(1, -1), "w_in": w_in_r, "conv_w": conv_w,
        "w_conv_out": w_conv_out.astype(bf), "g_q_lora": g_q_lora.reshape(1, -1),
        "w_uq": _pad_heads(w_uq, QK_HEAD).astype(bf), "g_kv_lora": g_kv_lora.reshape(1, -1),
        "w_uk": w_uk.astype(bf), "w_uv": w_uv.astype(bf),
        "g_qnorm": pad_gain(g_qnorm), "g_knorm": pad_gain(g_knorm),
        "w_attn_out": w_ao.astype(bf), "w_out": w_out.astype(bf), "g_norm2": g_norm2.reshape(1, -1),
        "peer_wq": peer_wq.astype(bf), "kcat": kcat.astype(bf),
        "tab_u": _pack_table(peer_u), "tab_v": _pack_table(peer_v),
    }


def _rope_tables(seq):
    pos = jnp.arange(seq, dtype=jnp.float32)
    inv = ROPE_THETA ** (-jnp.arange(0, QK_ROPE, 2, dtype=jnp.float32) / QK_ROPE)
    ang = pos[:, None] * inv[None, :]
    cos, sin = jnp.cos(ang), jnp.sin(ang)
    t1 = slice(QK_NOPE, QK_NOPE + HALF_ROPE)
    t2 = slice(QK_NOPE + HALF_ROPE, QK_HEAD)
    cos_t = jnp.ones((seq, HEAD_SLOT), jnp.float32).at[:, t1].set(cos).at[:, t2].set(cos)
    sin_a = jnp.zeros((seq, HEAD_SLOT), jnp.float32).at[:, t1].set(-sin)
    sin_b = jnp.zeros((seq, HEAD_SLOT), jnp.float32).at[:, t2].set(sin)
    return cos_t, sin_a, sin_b


def _encoder_layer(x, c, w_ada, b_ada, wts):
    bsz, seq, d = x.shape
    n = bsz * seq
    ada = _ada(c, w_ada, b_ada)[:, None, :]
    shift1, scale1, gate1, shift2, scale2, gate2 = jnp.split(ada, 6, axis=-1)
    q, k, v, a, sgb = _inproj(x, scale1, shift1, wts, _rope_tables(seq))
    o = _attention(q, k, v)
    x1, h2, e, g = _mix(x, a, sgb, o, gate1, scale2, shift2, wts)
    e = e.reshape(n, N_PICKS)
    w = _peer_u(e, h2.reshape(n * SUBLANES, LANES), g.reshape(n, N_PICKS), wts["tab_u"])
    y = _peer_v(e, w, x1.reshape(n, d), gate2, wts["tab_v"], seq)
    return y.reshape(bsz, seq, d)


def kernel(x_prompt, x_sample, c_prompt, c_sample, w_ada, b_ada, g_norm1, w_in, conv_w, w_conv_out, g_q_lora, w_uq, g_kv_lora, w_ukv, g_qnorm, g_knorm, w_attn_out, w_out, g_norm2, peer_wq, peer_k1, peer_k2, peer_u, peer_v):
    wts = _prep_weights(g_norm1[0], w_in[0], conv_w[0], w_conv_out[0], g_q_lora[0], w_uq[0],
                        g_kv_lora[0], w_ukv[0], g_qnorm[0], g_knorm[0], w_attn_out[0], w_out[0],
                        g_norm2[0], peer_wq[0], peer_k1[0], peer_k2[0], peer_u[0], peer_v[0])
    return (_encoder_layer(x_prompt, c_prompt, w_ada[0], b_ada[0], wts),
            _encoder_layer(x_sample, c_sample, w_ada[0], b_ada[0], wts))
```

```python
import functools
import math

import jax
import jax.numpy as jnp
import numpy as np
from jax import lax
from jax.experimental import pallas as pl
from jax.experimental.pallas import tpu as pltpu

D_MODEL = 1024
D_CONV = 512
N_HEADS = 8
QK_NOPE = 64
QK_ROPE = 32
V_HEAD = 64
Q_LORA = 256
KV_LORA = 128
QK_HEAD = QK_NOPE + QK_ROPE
ROPE_THETA = 10000.0
PEER_HEADS = 8
N_KEYS = 128
PEER_DK = 128
PEER_TOPK = 16
EPS = 1e-6

LANES = 128
SUBLANES = 8
HEAD_SLOT = LANES
HALF_ROPE = QK_ROPE // 2
N_PICKS = PEER_HEADS * PEER_TOPK
ROW_WORDS = D_MODEL // 2
ROW_SUB = ROW_WORDS // LANES
VMEM_LIMIT = 56 * 1024 * 1024

C_BG, C_CG, C_HC = 0, 512, 1024
C_CQ = 1536
C_CKV = C_CQ + Q_LORA
C_KR = C_CKV + KV_LORA
C_GA = C_KR + HEAD_SLOT
C_GB = C_GA + D_MODEL
IN_COLS_R = C_GB + D_MODEL

TS_IN = 512
TQ = 512
TK = 2048
KV_UNROLL = 2
TS_MIX = 256
TT_PEER = 128
TOK_UNROLL_U = 2
TOK_UNROLL_V = 4
RED_TOKENS = 8
BF16_ROWS = 16
IDX_GROUP = 8

_NEG_INF = float("-inf")


def _cparams(sem):
    return pltpu.CompilerParams(dimension_semantics=sem, vmem_limit_bytes=VMEM_LIMIT)


def _ada_kernel(c_ref, w_ref, b_ref, o_ref):
    o_ref[...] = jnp.dot(c_ref[...], w_ref[...], preferred_element_type=jnp.float32,
                         precision=lax.Precision.HIGHEST) + b_ref[...]


def _ada(c, w_ada, b_ada):
    bsz = c.shape[0]
    rows = -(-bsz // SUBLANES) * SUBLANES
    c_p = jnp.zeros((rows, D_MODEL), jnp.float32).at[:bsz].set(c)
    out = pl.pallas_call(
        _ada_kernel,
        grid=(6,),
        in_specs=[pl.BlockSpec((rows, D_MODEL), lambda j: (0, 0)),
                  pl.BlockSpec((D_MODEL, D_MODEL), lambda j: (0, j)),
                  pl.BlockSpec((1, D_MODEL), lambda j: (0, j))],
        out_specs=pl.BlockSpec((rows, D_MODEL), lambda j: (0, j)),
        out_shape=jax.ShapeDtypeStruct((rows, 6 * D_MODEL), jnp.float32),
        compiler_params=_cparams(("arbitrary",)),
        name="ada",
    )(c_p, w_ada, b_ada.reshape(1, -1))
    return out[:bsz]


def _adaln(x, g, scale, shift):
    ms = jnp.mean(x * x, axis=-1, keepdims=True)
    return (x * lax.rsqrt(ms + EPS) * g) * (1.0 + scale) + shift


def _rms_rows(x, g, n):
    ms = jnp.sum(x * x, axis=-1, keepdims=True) * (1.0 / n)
    return x * lax.rsqrt(ms + EPS) * g


def _rope(x, cos, sin_a, sin_b):
    return (x * cos + pltpu.roll(x, LANES - HALF_ROPE, axis=1) * sin_a
            + pltpu.roll(x, HALF_ROPE, axis=1) * sin_b)


def _inproj_kernel(x_ref, xp_ref, xn_ref, scale_ref, shift_ref, g1_ref, w_in_ref, conv_w_ref,
                   w_co_ref, gql_ref, w_uq_ref, gkvl_ref, w_uk_ref, w_uv_ref, gqn_ref, gkn_ref,
                   cos_ref, sa_ref, sb_ref,
                   q_ref, k_ref, v_ref, a_ref, sgb_ref):
    i = pl.program_id(1)
    n_i = pl.num_programs(1)
    g1 = g1_ref[...]
    scale = scale_ref[...]
    shift = shift_ref[...]
    ts = x_ref.shape[0]

    h = _adaln(x_ref[...], g1, scale, shift).astype(jnp.bfloat16)

    def proj(lo, width):
        return jnp.dot(h, w_in_ref[:, lo:lo + width], preferred_element_type=jnp.float32)

    z = proj(C_CG, D_CONV) * proj(C_HC, D_CONV)

    def halo_z(xh_ref):
        hh = _adaln(xh_ref[...], g1, scale, shift).astype(jnp.bfloat16)
        zc = jnp.dot(hh, w_in_ref[:, C_CG:C_CG + 2 * D_CONV], preferred_element_type=jnp.float32)
        return zc[:, :D_CONV] * zc[:, D_CONV:]

    z_prev = halo_z(xp_ref)[SUBLANES - 1:SUBLANES, :]
    z_next = halo_z(xn_ref)[0:1, :]
    z_prev = jnp.where(i == 0, 0.0, z_prev)
    z_next = jnp.where(i == n_i - 1, 0.0, z_next)
    row = lax.broadcasted_iota(jnp.int32, (ts, D_CONV), 0)
    z_up = jnp.where(row == 0, z_prev, pltpu.roll(z, 1, axis=0))
    z_dn = jnp.where(row == ts - 1, z_next, pltpu.roll(z, ts - 1, axis=0))
    cw = conv_w_ref[...]
    y = z_up * cw[0:1, :] + z * cw[1:2, :] + z_dn * cw[2:3, :]
    out_a = jnp.dot((proj(C_BG, D_CONV) * y).astype(jnp.bfloat16), w_co_ref[...],
                    preferred_element_type=jnp.float32)
    a_ref[...] = jax.nn.sigmoid(proj(C_GA, D_MODEL)) * out_a
    sgb_ref[...] = jax.nn.sigmoid(proj(C_GB, D_MODEL))

    cos = cos_ref[...]
    sin_a = sa_ref[...]
    sin_b = sb_ref[...]
    cq = _rms_rows(proj(C_CQ, Q_LORA), gql_ref[...], Q_LORA).astype(jnp.bfloat16)
    qf = jnp.dot(cq, w_uq_ref[...], preferred_element_type=jnp.float32)
    ckv = _rms_rows(proj(C_CKV, KV_LORA), gkvl_ref[...], KV_LORA).astype(jnp.bfloat16)
    kf = jnp.dot(ckv, w_uk_ref[...], preferred_element_type=jnp.float32)
    lane = lax.broadcasted_iota(jnp.int32, (1, N_HEADS * HEAD_SLOT), 1)
    ones_lane = (jnp.bitwise_and(lane, HEAD_SLOT - 1) == V_HEAD).astype(jnp.float32)
    v_ref[...] = (jnp.dot(ckv, w_uv_ref[...], preferred_element_type=jnp.float32)
                  + ones_lane).astype(jnp.bfloat16)
    kr = proj(C_KR, HEAD_SLOT)
    gqn = gqn_ref[...]
    gkn = gkn_ref[...]
    q_scale = QK_HEAD ** -0.5 * math.log2(math.e)
    for hd in range(N_HEADS):
        sl = slice(hd * HEAD_SLOT, (hd + 1) * HEAD_SLOT)
        qh = _rope(_rms_rows(qf[:, sl], gqn, QK_HEAD), cos, sin_a, sin_b)
        q_ref[:, sl] = (qh * q_scale).astype(jnp.bfloat16)
        kh = _rope(_rms_rows(kf[:, sl] + kr, gkn, QK_HEAD), cos, sin_a, sin_b)
        k_ref[:, sl] = kh.astype(jnp.bfloat16)


def _inproj(x, scale1, shift1, wts, rope):
    bsz, seq, _ = x.shape
    ts = min(TS_IN, seq)
    n_i = seq // ts
    nb8 = seq // SUBLANES
    per8 = ts // SUBLANES
    tile = lambda b, i: (b, i, 0)
    per_b = lambda b, i: (b, 0, 0)
    full2 = lambda b, i: (0, 0)
    hw = N_HEADS * HEAD_SLOT
    in_specs = [
        pl.BlockSpec((None, ts, D_MODEL), tile),
        pl.BlockSpec((None, SUBLANES, D_MODEL), lambda b, i: (b, jnp.maximum(i * per8 - 1, 0), 0)),
        pl.BlockSpec((None, SUBLANES, D_MODEL), lambda b, i: (b, jnp.minimum((i + 1) * per8, nb8 - 1), 0)),
        pl.BlockSpec((None, 1, D_MODEL), per_b),
        pl.BlockSpec((None, 1, D_MODEL), per_b),
        pl.BlockSpec((1, D_MODEL), full2),
        pl.BlockSpec((D_MODEL, IN_COLS_R), full2),
        pl.BlockSpec((3, D_CONV), full2),
        pl.BlockSpec((D_CONV, D_MODEL), full2),
        pl.BlockSpec((1, Q_LORA), full2),
        pl.BlockSpec((Q_LORA, hw), full2),
        pl.BlockSpec((1, KV_LORA), full2),
        pl.BlockSpec((KV_LORA, hw), full2),
        pl.BlockSpec((KV_LORA, hw), full2),
        pl.BlockSpec((1, HEAD_SLOT), full2),
        pl.BlockSpec((1, HEAD_SLOT), full2),
        pl.BlockSpec((ts, HEAD_SLOT), lambda b, i: (i, 0)),
        pl.BlockSpec((ts, HEAD_SLOT), lambda b, i: (i, 0)),
        pl.BlockSpec((ts, HEAD_SLOT), lambda b, i: (i, 0)),
    ]
    out_specs = [pl.BlockSpec((None, ts, hw), tile)] * 3 + [pl.BlockSpec((None, ts, D_MODEL), tile)] * 2
    out_shape = ([jax.ShapeDtypeStruct((bsz, seq, hw), jnp.bfloat16)] * 3
                 + [jax.ShapeDtypeStruct((bsz, seq, D_MODEL), jnp.float32)] * 2)
    return pl.pallas_call(
        _inproj_kernel,
        grid=(bsz, n_i),
        in_specs=in_specs,
        out_specs=out_specs,
        out_shape=out_shape,
        compiler_params=_cparams(("parallel", "parallel")),
        name="inproj",
    )(x, x, x, scale1, shift1, wts["g_norm1"], wts["w_in"], wts["conv_w"], wts["w_conv_out"],
      wts["g_q_lora"], wts["w_uq"], wts["g_kv_lora"], wts["w_uk"], wts["w_uv"],
      wts["g_qnorm"], wts["g_knorm"], rope[0], rope[1], rope[2])


def _attn_kernel(q_ref, k_ref, v_ref, o_ref):
    q = q_ref[...]
    tq = q.shape[0]
    seq = k_ref.shape[0]
    tk = min(TK, seq)

    n_chunks = seq // tk
    unroll = KV_UNROLL if n_chunks % KV_UNROLL == 0 else 1

    def chunk(off, carry):
        m, acc = carry
        kc = k_ref[pl.ds(off, tk), :]
        vc = v_ref[pl.ds(off, tk), :]
        s = lax.dot_general(q, kc, (((1,), (1,)), ((), ())), preferred_element_type=jnp.float32)
        m_new = jnp.maximum(m, jnp.max(s, axis=-1, keepdims=True))
        p = jnp.exp2(s - m_new).astype(jnp.bfloat16)
        alpha = jnp.exp2(m - m_new)
        acc = alpha * acc + jnp.dot(p, vc, preferred_element_type=jnp.float32)
        return m_new, acc

    def body(j, carry):
        for u in range(unroll):
            carry = chunk(pl.multiple_of((j * unroll + u) * tk, tk), carry)
        return carry

    m0 = jnp.full((tq, 1), _NEG_INF, jnp.float32)
    acc0 = jnp.zeros((tq, HEAD_SLOT), jnp.float32)
    _, acc = lax.fori_loop(0, n_chunks // unroll, body, (m0, acc0))
    o_ref[...] = (acc / acc[:, V_HEAD:V_HEAD + 1]).astype(jnp.bfloat16)


def _attention(q, k, v):
    bsz, seq, hw = q.shape
    tq = min(TQ, seq)
    return pl.pallas_call(
        _attn_kernel,
        grid=(bsz, N_HEADS, seq // tq),
        in_specs=[pl.BlockSpec((None, tq, HEAD_SLOT), lambda b, h, i: (b, i, h)),
                  pl.BlockSpec((None, seq, HEAD_SLOT), lambda b, h, i: (b, 0, h)),
                  pl.BlockSpec((None, seq, HEAD_SLOT), lambda b, h, i: (b, 0, h))],
        out_specs=pl.BlockSpec((None, tq, HEAD_SLOT), lambda b, h, i: (b, i, h)),
        out_shape=jax.ShapeDtypeStruct((bsz, seq, hw), jnp.bfloat16),
        compiler_params=_cparams(("parallel", "parallel", "arbitrary")),
        name="attn",
    )(q, k, v)


def _topk_rows(s, iota):
    vals, idxs = [], []
    big = jnp.float32(1e9)
    for _ in range(PEER_TOPK):
        m = jnp.max(s, axis=0, keepdims=True)
        am = jnp.min(jnp.where(s == m, iota, big), axis=0, keepdims=True)
        vals.append(m)
        idxs.append(am)
        s = jnp.where(iota == am, _NEG_INF, s)
    return jnp.concatenate(vals, axis=0), jnp.concatenate(idxs, axis=0)


def _take_rows(tab, idx):
    out = jnp.zeros_like(tab)
    for a in range(tab.shape[0]):
        out = out + jnp.where(idx == jnp.float32(a), tab[a:a + 1, :], 0.0)
    return out


_CAND = [(a, b) for a in range(PEER_TOPK) for b in range(PEER_TOPK) if (a + 1) * (b + 1) <= PEER_TOPK]
N_CAND = -(-len(_CAND) // SUBLANES) * SUBLANES
_PAD_LABEL = float(PEER_TOPK * PEER_TOPK)


def _candidate_tables():
    sel = np.zeros((2, N_CAND, PEER_TOPK), np.float32)
    lab = np.full((N_CAND, 1), _PAD_LABEL, np.float32)
    for r, (a, b) in enumerate(_CAND):
        sel[0, r, a] = 1.0
        sel[1, r, b] = 1.0
        lab[r, 0] = a * PEER_TOPK + b
    return sel, lab


def _mix_kernel(x_ref, a_ref, sgb_ref, o_ref, gate1_ref, scale2_ref, shift2_ref, g2_ref,
                w_ao_ref, w_out_ref, w_pq_ref, kcat_ref, sel_ref, lab_ref,
                x1_ref, h2_ref, e_ref, g_ref):
    ts = x_ref.shape[0]
    out_b = jnp.dot(o_ref[...], w_ao_ref[...], preferred_element_type=jnp.float32)
    merged = (a_ref[...] + sgb_ref[...] * out_b).astype(jnp.bfloat16)
    x1 = x_ref[...] + gate1_ref[...] * jnp.dot(merged, w_out_ref[...],
                                              preferred_element_type=jnp.float32)
    x1_ref[...] = x1
    h2 = _adaln(x1, g2_ref[...], scale2_ref[...], shift2_ref[...])
    h2_ref[...] = h2
    qp = jnp.dot(h2.astype(jnp.bfloat16), w_pq_ref[...],
                 preferred_element_type=jnp.float32).astype(jnp.bfloat16)

    iota_k = lax.broadcasted_iota(jnp.int32, (N_KEYS, ts), 0).astype(jnp.float32)
    lab = lab_ref[...]
    valid = lab < _PAD_LABEL
    sel_a = sel_ref[0]
    sel_b = sel_ref[1]
    exact = dict(preferred_element_type=jnp.float32, precision=lax.Precision.HIGHEST)
    e_rows, g_rows = [], []
    for hd in range(PEER_HEADS):
        qh = qp[:, hd * PEER_DK:(hd + 1) * PEER_DK]
        st = lax.dot_general(kcat_ref[hd], qh, (((1,), (1,)), ((), ())),
                             preferred_element_type=jnp.float32)
        v1, i1 = _topk_rows(st[:N_KEYS], iota_k)
        v2, i2 = _topk_rows(st[N_KEYS:], iota_k)
        comb = jnp.dot(sel_a, v1, **exact) + jnp.dot(sel_b, v2, **exact)
        sc, pos = _topk_rows(jnp.where(valid, comb, _NEG_INF), lab)
        ia = jnp.floor(pos * (1.0 / PEER_TOPK))
        ib = pos - ia * PEER_TOPK
        e_rows.append(_take_rows(i1, ia) * N_KEYS + _take_rows(i2, ib))
        p = jnp.exp(sc - sc[0:1, :])
        g_rows.append(p / jnp.sum(p, axis=0, keepdims=True))
    e_all = jnp.concatenate(e_rows, axis=0)
    g_all = jnp.concatenate(g_rows, axis=0)
    for c in range(ts // LANES):
        cs = slice(c * LANES, (c + 1) * LANES)
        e_ref[cs, :] = (e_all[:, cs].T * ROW_SUB).astype(jnp.int32)
        g_ref[cs, :] = g_all[:, cs].T


def _mix(x, a, sgb, o, gate1, scale2, shift2, wts):
    bsz, seq, _ = x.shape
    ts = min(TS_MIX, seq)
    tile = lambda b, i: (b, i, 0)
    per_b = lambda b, i: (b, 0, 0)
    full2 = lambda b, i: (0, 0)
    hw = N_HEADS * HEAD_SLOT
    big = pl.BlockSpec((None, ts, D_MODEL), tile)
    vec = pl.BlockSpec((None, 1, D_MODEL), per_b)
    picks = pl.BlockSpec((None, ts, N_PICKS), tile)
    sel, lab = _candidate_tables()
    return pl.pallas_call(
        _mix_kernel,
        grid=(bsz, seq // ts),
        in_specs=[big, big, big, pl.BlockSpec((None, ts, hw), tile), vec, vec, vec,
                  pl.BlockSpec((1, D_MODEL), full2),
                  pl.BlockSpec((hw, D_MODEL), full2),
                  pl.BlockSpec((D_MODEL, D_MODEL), full2),
                  pl.BlockSpec((D_MODEL, PEER_HEADS * PEER_DK), full2),
                  pl.BlockSpec((PEER_HEADS, 2 * N_KEYS, PEER_DK), lambda b, i: (0, 0, 0)),
                  pl.BlockSpec((2, N_CAND, PEER_TOPK), lambda b, i: (0, 0, 0)),
                  pl.BlockSpec((N_CAND, ts), full2)],
        out_specs=[big, big, picks, picks],
        out_shape=[jax.ShapeDtypeStruct((bsz, seq, D_MODEL), jnp.float32),
                   jax.ShapeDtypeStruct((bsz, seq, D_MODEL), jnp.float32),
                   jax.ShapeDtypeStruct((bsz, seq, N_PICKS), jnp.int32),
                   jax.ShapeDtypeStruct((bsz, seq, N_PICKS), jnp.float32)],
        compiler_params=_cparams(("parallel", "parallel")),
        name="mix",
    )(x, a, sgb, o, gate1, scale2, shift2, wts["g_norm2"], wts["w_attn_out"], wts["w_out"],
      wts["peer_wq"], wts["kcat"], jnp.asarray(sel), jnp.broadcast_to(jnp.asarray(lab), (N_CAND, ts)))


def _unpack_row(w):
    lo = lax.bitcast_convert_type(lax.shift_left(w, 16), jnp.float32)
    hi = lax.bitcast_convert_type(jnp.bitwise_and(w, jnp.int32(-65536)), jnp.float32)
    return lo, hi


def _gelu_tanh(x):
    c = math.sqrt(2.0 / math.pi)
    return 0.5 * x * (1.0 + jnp.tanh(c * (x + 0.044715 * (x * x * x))))


def _gather_rows(e_ref, t, tab_ref, pb):
    for j in range(N_PICKS // IDX_GROUP):
        row = e_ref.at[pl.ds(pl.multiple_of(t * N_PICKS + j * IDX_GROUP, IDX_GROUP), IDX_GROUP)]
        for c in range(IDX_GROUP):
            k = j * IDX_GROUP + c
            r = pl.multiple_of(row[c], ROW_SUB)
            pb[k * ROW_SUB:(k + 1) * ROW_SUB, :] = tab_ref[pl.ds(r, ROW_SUB), :]


def _pick_block(pb, s):
    return pb[pl.ds(s, N_PICKS, stride=ROW_SUB), :]


def _peer_u_kernel(e_ref, x_ref, g_ref, tab_ref, w_ref, pbuf0, pbuf1, rbuf, abuf):
    tt = g_ref.shape[0]
    pbufs = (pbuf0, pbuf1)

    def tok(i, carry):
        for u in range(TOK_UNROLL_U):
            t = i * TOK_UNROLL_U + u
            pb = pbufs[u % 2]
            x_blk = x_ref[pl.ds(pl.multiple_of(t * SUBLANES, SUBLANES), SUBLANES), :]
            _gather_rows(e_ref, t, tab_ref, pb)
            acc = None
            for s in range(ROW_SUB):
                lo, hi = _unpack_row(_pick_block(pb, s))
                term = lo * x_blk[s:s + 1, :] + hi * x_blk[ROW_SUB + s:ROW_SUB + s + 1, :]
                acc = term if acc is None else acc + term
            rbuf[pl.ds(pl.multiple_of(t * N_PICKS, N_PICKS), N_PICKS), :] = acc
        return carry

    lax.fori_loop(0, tt // TOK_UNROLL_U, tok, 0)

    ones2 = jnp.ones((2 * LANES, LANES), jnp.bfloat16)
    lane = lax.broadcasted_iota(jnp.int32, (N_PICKS, tt), 1)
    abuf[...] = jnp.zeros_like(abuf)
    rows = RED_TOKENS * N_PICKS

    def red(i, carry):
        rb = rbuf[pl.ds(pl.multiple_of(i * rows, rows), rows), :]
        r_hi = rb.astype(jnp.bfloat16)
        r_lo = (rb - r_hi.astype(jnp.float32)).astype(jnp.bfloat16)
        res = jnp.dot(jnp.concatenate([r_hi, r_lo], axis=1), ones2, preferred_element_type=jnp.float32)
        at = abuf[...]
        for u in range(RED_TOKENS):
            at = jnp.where(lane == i * RED_TOKENS + u, res[u * N_PICKS:(u + 1) * N_PICKS, :], at)
        abuf[...] = at
        return carry

    lax.fori_loop(0, tt // RED_TOKENS, red, 0)
    w_ref[...] = g_ref[...] * _gelu_tanh(abuf[...].T)


def _peer_u(e, h2_rows, g, tab):
    n = e.shape[0] // N_PICKS
    tt = TT_PEER
    return pl.pallas_call(
        _peer_u_kernel,
        grid=(n // tt,),
        in_specs=[pl.BlockSpec((tt * N_PICKS,), lambda i: (i,), memory_space=pltpu.SMEM),
                  pl.BlockSpec((tt * SUBLANES, LANES), lambda i: (i, 0)),
                  pl.BlockSpec((tt, N_PICKS), lambda i: (i, 0)),
                  pl.BlockSpec(memory_space=pltpu.VMEM)],
        out_specs=pl.BlockSpec((tt, N_PICKS), lambda i: (i, 0)),
        out_shape=jax.ShapeDtypeStruct((n, N_PICKS), jnp.float32),
        scratch_shapes=[pltpu.VMEM((N_PICKS * ROW_SUB, LANES), jnp.int32),
                        pltpu.VMEM((N_PICKS * ROW_SUB, LANES), jnp.int32),
                        pltpu.VMEM((tt * N_PICKS, LANES), jnp.float32),
                        pltpu.VMEM((N_PICKS, tt), jnp.float32)],
        compiler_params=_cparams(("arbitrary",)),
        name="peer_u",
    )(e, h2_rows, g, tab)


def _peer_v_kernel(e_ref, w_ref, x_ref, gate_ref, tab_ref, y_ref, pbuf0, pbuf1, whi_ref, wlo_ref, acc_ref):
    tt = x_ref.shape[0]
    pbufs = (pbuf0, pbuf1)
    w = w_ref[...]
    w_hi = w.astype(jnp.bfloat16)
    whi_ref[...] = w_hi
    wlo_ref[...] = (w - w_hi.astype(jnp.float32)).astype(jnp.bfloat16)
    sub = lax.broadcasted_iota(jnp.int32, (BF16_ROWS, D_MODEL), 0)
    acc_ref[...] = jnp.zeros_like(acc_ref)

    def process(pb, t):
        b0 = pl.multiple_of(lax.shift_left(lax.shift_right_logical(t, 4), 4), BF16_ROWS)
        lhs = jnp.concatenate([whi_ref[pl.ds(b0, BF16_ROWS), :], wlo_ref[pl.ds(b0, BF16_ROWS), :]], axis=0)
        los, his = [], []
        for s in range(ROW_SUB):
            lo, hi = _unpack_row(_pick_block(pb, s))
            los.append(lo.astype(jnp.bfloat16))
            his.append(hi.astype(jnp.bfloat16))
        rhs = jnp.concatenate(los + his, axis=1)
        res = jnp.dot(lhs, rhs, preferred_element_type=jnp.float32)
        r = res[:BF16_ROWS, :] + res[BF16_ROWS:, :]
        mask = sub == jnp.bitwise_and(t, BF16_ROWS - 1)
        acc_ref[pl.ds(b0, BF16_ROWS), :] = jnp.where(mask, r, acc_ref[pl.ds(b0, BF16_ROWS), :])

    _gather_rows(e_ref, 0, tab_ref, pbuf0)

    def step(i, carry):
        t0 = i * TOK_UNROLL_V
        for u in range(TOK_UNROLL_V):
            process(pbufs[u % 2], t0 + u)
            _gather_rows(e_ref, jnp.minimum(t0 + u + 1, tt - 1), tab_ref, pbufs[(u + 1) % 2])
        return carry

    lax.fori_loop(0, tt // TOK_UNROLL_V, step, 0)
    y_ref[...] = x_ref[...] + gate_ref[...] * acc_ref[...]


def _peer_v(e, w, x1, gate2, tab, seq):
    n = e.shape[0] // N_PICKS
    tt = TT_PEER
    return pl.pallas_call(
        _peer_v_kernel,
        grid=(n // tt,),
        in_specs=[pl.BlockSpec((tt * N_PICKS,), lambda i: (i,), memory_space=pltpu.SMEM),
                  pl.BlockSpec((tt, N_PICKS), lambda i: (i, 0)),
                  pl.BlockSpec((tt, D_MODEL), lambda i: (i, 0)),
                  pl.BlockSpec((None, 1, D_MODEL), lambda i: ((i * tt) // seq, 0, 0)),
                  pl.BlockSpec(memory_space=pltpu.VMEM)],
        out_specs=pl.BlockSpec((tt, D_MODEL), lambda i: (i, 0)),
        out_shape=jax.ShapeDtypeStruct((n, D_MODEL), jnp.float32),
        scratch_shapes=[pltpu.VMEM((N_PICKS * ROW_SUB, LANES), jnp.int32),
                        pltpu.VMEM((N_PICKS * ROW_SUB, LANES), jnp.int32),
                        pltpu.VMEM((tt, N_PICKS), jnp.bfloat16),
                        pltpu.VMEM((tt, N_PICKS), jnp.bfloat16),
                        pltpu.VMEM((tt, D_MODEL), jnp.float32)],
        compiler_params=_cparams(("arbitrary",)),
        name="peer_v",
    )(e, w, x1, gate2, tab)


def _pad_heads(w, used, lead=0):
    r = w.shape[0]
    w3 = w.reshape(r, N_HEADS, used)
    out = jnp.zeros((r, N_HEADS, HEAD_SLOT), w.dtype).at[:, :, lead:lead + used].set(w3)
    return out.reshape(r, N_HEADS * HEAD_SLOT)


def _pack_table(t):
    tb = t.astype(jnp.bfloat16)
    lo = lax.bitcast_convert_type(tb[:, :ROW_WORDS], jnp.uint16).astype(jnp.uint32)
    hi = lax.bitcast_convert_type(tb[:, ROW_WORDS:], jnp.uint16).astype(jnp.uint32)
    words = lax.bitcast_convert_type(lo | (hi << 16), jnp.int32)
    return words.reshape(t.shape[0] * ROW_SUB, LANES)


def _prep_weights(g_norm1, w_in, conv_w, w_conv_out, g_q_lora, w_uq, g_kv_lora, w_ukv, g_qnorm,
                  g_knorm, w_attn_out, w_out, g_norm2, peer_wq, peer_k1, peer_k2, peer_u, peer_v):
    bf = jnp.bfloat16
    o1 = 3 * D_CONV
    o2 = o1 + Q_LORA
    o3 = o2 + KV_LORA
    o4 = o3 + QK_ROPE
    o5 = o4 + D_MODEL
    kr_slot = jnp.zeros((D_MODEL, HEAD_SLOT), w_in.dtype).at[:, QK_NOPE:QK_HEAD].set(w_in[:, o3:o4])
    w_in_r = jnp.concatenate([w_in[:, :o3], kr_slot, w_in[:, o4:]], axis=1).astype(bf)
    w_ukv3 = w_ukv.reshape(KV_LORA, N_HEADS, QK_NOPE + V_HEAD)
    w_uk = _pad_heads(w_ukv3[:, :, :QK_NOPE].reshape(KV_LORA, -1), QK_NOPE)
    w_uv = _pad_heads(w_ukv3[:, :, QK_NOPE:].reshape(KV_LORA, -1), V_HEAD)
    pad_gain = lambda g: jnp.zeros((1, HEAD_SLOT), jnp.float32).at[0, :QK_HEAD].set(g)
    w_ao = jnp.zeros((N_HEADS, HEAD_SLOT, D_MODEL), w_attn_out.dtype).at[:, :V_HEAD].set(
        w_attn_out.reshape(N_HEADS, V_HEAD, D_MODEL)).reshape(N_HEADS * HEAD_SLOT, D_MODEL)
    half = PEER_DK // 2
    kcat = jnp.zeros((PEER_HEADS, 2 * N_KEYS, PEER_DK), jnp.float32)
    kcat = kcat.at[:, :N_KEYS, :half].set(peer_k1).at[:, N_KEYS:, half:].set(peer_k2)
    return {
        "g_norm1": g_norm1.reshape(1, -1), "w_in": w_in_r, "conv_w": conv_w,
        "w_conv_out": w_conv_out.astype(bf), "g_q_lora": g_q_lora.reshape(1, -1),
        "w_uq": _pad_heads(w_uq, QK_HEAD).astype(bf), "g_kv_lora": g_kv_lora.reshape(1, -1),
        "w_uk": w_uk.astype(bf), "w_uv": w_uv.astype(bf),
        "g_qnorm": pad_gain(g_qnorm), "g_knorm": pad_gain(g_knorm),
        "w_attn_out": w_ao.astype(bf), "w_out": w_out.astype(bf), "g_norm2": g_norm2.reshape(1, -1),
        "peer_wq": peer_wq.astype(bf), "kcat": kcat.astype(bf),
        "tab_u": _pack_table(peer_u), "tab_v": _pack_table(peer_v),
    }


def _rope_tables(seq):
    pos = jnp.arange(seq, dtype=jnp.float32)
    inv = ROPE_THETA ** (-jnp.arange(0, QK_ROPE, 2, dtype=jnp.float32) / QK_ROPE)
    ang = pos[:, None] * inv[None, :]
    cos, sin = jnp.cos(ang), jnp.sin(ang)
    t1 = slice(QK_NOPE, QK_NOPE + HALF_ROPE)
    t2 = slice(QK_NOPE + HALF_ROPE, QK_HEAD)
    cos_t = jnp.ones((seq, HEAD_SLOT), jnp.float32).at[:, t1].set(cos).at[:, t2].set(cos)
    sin_a = jnp.zeros((seq, HEAD_SLOT), jnp.float32).at[:, t1].set(-sin)
    sin_b = jnp.zeros((seq, HEAD_SLOT), jnp.float32).at[:, t2].set(sin)
    return cos_t, sin_a, sin_b


def _encoder_layer(x, c, w_ada, b_ada, wts):
    bsz, seq, d = x.shape
    n = bsz * seq
    ada = _ada(c, w_ada, b_ada)[:, None, :]
    shift1, scale1, gate1, shift2, scale2, gate2 = jnp.split(ada, 6, axis=-1)
    q, k, v, a, sgb = _inproj(x, scale1, shift1, wts, _rope_tables(seq))
    o = _attention(q, k, v)
    x1, h2, e, g = _mix(x, a, sgb, o, gate1, scale2, shift2, wts)
    e = e.reshape(n * N_PICKS)
    w = _peer_u(e, h2.reshape(n * SUBLANES, LANES), g.reshape(n, N_PICKS), wts["tab_u"])
    y = _peer_v(e, w, x1.reshape(n, d), gate2, wts["tab_v"], seq)
    return y.reshape(bsz, seq, d)


def kernel(x_prompt, x_sample, c_prompt, c_sample, w_ada, b_ada, g_norm1, w_in, conv_w, w_conv_out, g_q_lora, w_uq, g_kv_lora, w_ukv, g_qnorm, g_knorm, w_attn_out, w_out, g_norm2, peer_wq, peer_k1, peer_k2, peer_u, peer_v):
    wts = _prep_weights(g_norm1[0], w_in[0], conv_w[0], w_conv_out[0], g_q_lora[0], w_uq[0],
                        g_kv_lora[0], w_ukv[0], g_qnorm[0], g_knorm[0], w_attn_out[0], w_out[0],
                        g_norm2[0], peer_wq[0], peer_k1[0], peer_k2[0], peer_u[0], peer_v[0])
    return (_encoder_layer(x_prompt, c_prompt, w_ada[0], b_ada[0], wts),
            _encoder_layer(x_sample, c_sample, w_ada[0], b_ada[0], wts))
```

```python
import dataclasses
import functools
import math

import jax
import jax.numpy as jnp
import numpy as np
from jax import lax
from jax.experimental import pallas as pl
from jax.experimental.pallas import tpu as pltpu
from jax.experimental.pallas import tpu_sc as plsc

D_MODEL = 1024
D_CONV = 512
N_HEADS = 8
QK_NOPE = 64
QK_ROPE = 32
V_HEAD = 64
Q_LORA = 256
KV_LORA = 128
QK_HEAD = QK_NOPE + QK_ROPE
ROPE_THETA = 10000.0
PEER_HEADS = 8
N_KEYS = 128
PEER_DK = 128
PEER_TOPK = 16
EPS = 1e-6

LANES = 128
SUBLANES = 8
HEAD_SLOT = LANES
HALF_ROPE = QK_ROPE // 2
N_PICKS = PEER_HEADS * PEER_TOPK
ROW_WORDS = D_MODEL // 2
ROW_SUB = ROW_WORDS // LANES
VMEM_LIMIT = 56 * 1024 * 1024

C_BG, C_CG, C_HC = 0, 512, 1024
C_CQ = 1536
C_CKV = C_CQ + Q_LORA
C_KR = C_CKV + KV_LORA
C_GA = C_KR + HEAD_SLOT
C_GB = C_GA + D_MODEL
IN_COLS_R = C_GB + D_MODEL

TS_IN = 512
TQ = 512
TK = 2048
KV_UNROLL = 2
TS_MIX = 256
TT_PEER = 128
TOK_UNROLL_U = 2
TOK_UNROLL_V = 4
RED_TOKENS = 8
BF16_ROWS = 16
IDX_GROUP = 8

_NEG_INF = float("-inf")


def _cparams(sem):
    return pltpu.CompilerParams(dimension_semantics=sem, vmem_limit_bytes=VMEM_LIMIT)


def _ada_kernel(c_ref, w_ref, b_ref, o_ref):
    o_ref[...] = jnp.dot(c_ref[...], w_ref[...], preferred_element_type=jnp.float32,
                         precision=lax.Precision.HIGHEST) + b_ref[...]


def _ada(c, w_ada, b_ada):
    bsz = c.shape[0]
    rows = -(-bsz // SUBLANES) * SUBLANES
    c_p = jnp.zeros((rows, D_MODEL), jnp.float32).at[:bsz].set(c)
    out = pl.pallas_call(
        _ada_kernel,
        grid=(6,),
        in_specs=[pl.BlockSpec((rows, D_MODEL), lambda j: (0, 0)),
                  pl.BlockSpec((D_MODEL, D_MODEL), lambda j: (0, j)),
                  pl.BlockSpec((1, D_MODEL), lambda j: (0, j))],
        out_specs=pl.BlockSpec((rows, D_MODEL), lambda j: (0, j)),
        out_shape=jax.ShapeDtypeStruct((rows, 6 * D_MODEL), jnp.float32),
        compiler_params=_cparams(("arbitrary",)),
        name="ada",
    )(c_p, w_ada, b_ada.reshape(1, -1))
    return out[:bsz]


def _adaln(x, g, scale, shift):
    ms = jnp.mean(x * x, axis=-1, keepdims=True)
    return (x * lax.rsqrt(ms + EPS) * g) * (1.0 + scale) + shift


def _rms_rows(x, g, n):
    ms = jnp.sum(x * x, axis=-1, keepdims=True) * (1.0 / n)
    return x * lax.rsqrt(ms + EPS) * g


def _rope(x, cos, sin_a, sin_b):
    return (x * cos + pltpu.roll(x, LANES - HALF_ROPE, axis=1) * sin_a
            + pltpu.roll(x, HALF_ROPE, axis=1) * sin_b)


def _inproj_kernel(x_ref, xp_ref, xn_ref, scale_ref, shift_ref, g1_ref, w_in_ref, conv_w_ref,
                   w_co_ref, gql_ref, w_uq_ref, gkvl_ref, w_uk_ref, w_uv_ref, gqn_ref, gkn_ref,
                   cos_ref, sa_ref, sb_ref,
                   q_ref, k_ref, v_ref, a_ref, sgb_ref):
    i = pl.program_id(1)
    n_i = pl.num_programs(1)
    g1 = g1_ref[...]
    scale = scale_ref[...]
    shift = shift_ref[...]
    ts = x_ref.shape[0]

    h = _adaln(x_ref[...], g1, scale, shift).astype(jnp.bfloat16)

    def proj(lo, width):
        return jnp.dot(h, w_in_ref[:, lo:lo + width], preferred_element_type=jnp.float32)

    z = proj(C_CG, D_CONV) * proj(C_HC, D_CONV)

    def halo_z(xh_ref):
        hh = _adaln(xh_ref[...], g1, scale, shift).astype(jnp.bfloat16)
        zc = jnp.dot(hh, w_in_ref[:, C_CG:C_CG + 2 * D_CONV], preferred_element_type=jnp.float32)
        return zc[:, :D_CONV] * zc[:, D_CONV:]

    z_prev = halo_z(xp_ref)[SUBLANES - 1:SUBLANES, :]
    z_next = halo_z(xn_ref)[0:1, :]
    z_prev = jnp.where(i == 0, 0.0, z_prev)
    z_next = jnp.where(i == n_i - 1, 0.0, z_next)
    row = lax.broadcasted_iota(jnp.int32, (ts, D_CONV), 0)
    z_up = jnp.where(row == 0, z_prev, pltpu.roll(z, 1, axis=0))
    z_dn = jnp.where(row == ts - 1, z_next, pltpu.roll(z, ts - 1, axis=0))
    cw = conv_w_ref[...]
    y = z_up * cw[0:1, :] + z * cw[1:2, :] + z_dn * cw[2:3, :]
    out_a = jnp.dot((proj(C_BG, D_CONV) * y).astype(jnp.bfloat16), w_co_ref[...],
                    preferred_element_type=jnp.float32)
    a_ref[...] = jax.nn.sigmoid(proj(C_GA, D_MODEL)) * out_a
    sgb_ref[...] = jax.nn.sigmoid(proj(C_GB, D_MODEL))

    cos = cos_ref[...]
    sin_a = sa_ref[...]
    sin_b = sb_ref[...]
    cq = _rms_rows(proj(C_CQ, Q_LORA), gql_ref[...], Q_LORA).astype(jnp.bfloat16)
    qf = jnp.dot(cq, w_uq_ref[...], preferred_element_type=jnp.float32)
    ckv = _rms_rows(proj(C_CKV, KV_LORA), gkvl_ref[...], KV_LORA).astype(jnp.bfloat16)
    kf = jnp.dot(ckv, w_uk_ref[...], preferred_element_type=jnp.float32)
    lane = lax.broadcasted_iota(jnp.int32, (1, N_HEADS * HEAD_SLOT), 1)
    ones_lane = (jnp.bitwise_and(lane, HEAD_SLOT - 1) == V_HEAD).astype(jnp.float32)
    v_ref[...] = (jnp.dot(ckv, w_uv_ref[...], preferred_element_type=jnp.float32)
                  + ones_lane).astype(jnp.bfloat16)
    kr = proj(C_KR, HEAD_SLOT)
    gqn = gqn_ref[...]
    gkn = gkn_ref[...]
    q_scale = QK_HEAD ** -0.5 * math.log2(math.e)
    for hd in range(N_HEADS):
        sl = slice(hd * HEAD_SLOT, (hd + 1) * HEAD_SLOT)
        qh = _rope(_rms_rows(qf[:, sl], gqn, QK_HEAD), cos, sin_a, sin_b)
        q_ref[:, sl] = (qh * q_scale).astype(jnp.bfloat16)
        kh = _rope(_rms_rows(kf[:, sl] + kr, gkn, QK_HEAD), cos, sin_a, sin_b)
        k_ref[:, sl] = kh.astype(jnp.bfloat16)


def _inproj(x, scale1, shift1, wts, rope):
    bsz, seq, _ = x.shape
    ts = min(TS_IN, seq)
    n_i = seq // ts
    nb8 = seq // SUBLANES
    per8 = ts // SUBLANES
    tile = lambda b, i: (b, i, 0)
    per_b = lambda b, i: (b, 0, 0)
    full2 = lambda b, i: (0, 0)
    hw = N_HEADS * HEAD_SLOT
    in_specs = [
        pl.BlockSpec((None, ts, D_MODEL), tile),
        pl.BlockSpec((None, SUBLANES, D_MODEL), lambda b, i: (b, jnp.maximum(i * per8 - 1, 0), 0)),
        pl.BlockSpec((None, SUBLANES, D_MODEL), lambda b, i: (b, jnp.minimum((i + 1) * per8, nb8 - 1), 0)),
        pl.BlockSpec((None, 1, D_MODEL), per_b),
        pl.BlockSpec((None, 1, D_MODEL), per_b),
        pl.BlockSpec((1, D_MODEL), full2),
        pl.BlockSpec((D_MODEL, IN_COLS_R), full2),
        pl.BlockSpec((3, D_CONV), full2),
        pl.BlockSpec((D_CONV, D_MODEL), full2),
        pl.BlockSpec((1, Q_LORA), full2),
        pl.BlockSpec((Q_LORA, hw), full2),
        pl.BlockSpec((1, KV_LORA), full2),
        pl.BlockSpec((KV_LORA, hw), full2),
        pl.BlockSpec((KV_LORA, hw), full2),
        pl.BlockSpec((1, HEAD_SLOT), full2),
        pl.BlockSpec((1, HEAD_SLOT), full2),
        pl.BlockSpec((ts, HEAD_SLOT), lambda b, i: (i, 0)),
        pl.BlockSpec((ts, HEAD_SLOT), lambda b, i: (i, 0)),
        pl.BlockSpec((ts, HEAD_SLOT), lambda b, i: (i, 0)),
    ]
    out_specs = [pl.BlockSpec((None, ts, hw), tile)] * 3 + [pl.BlockSpec((None, ts, D_MODEL), tile)] * 2
    out_shape = ([jax.ShapeDtypeStruct((bsz, seq, hw), jnp.bfloat16)] * 3
                 + [jax.ShapeDtypeStruct((bsz, seq, D_MODEL), jnp.float32)] * 2)
    return pl.pallas_call(
        _inproj_kernel,
        grid=(bsz, n_i),
        in_specs=in_specs,
        out_specs=out_specs,
        out_shape=out_shape,
        compiler_params=_cparams(("parallel", "parallel")),
        name="inproj",
    )(x, x, x, scale1, shift1, wts["g_norm1"], wts["w_in"], wts["conv_w"], wts["w_conv_out"],
      wts["g_q_lora"], wts["w_uq"], wts["g_kv_lora"], wts["w_uk"], wts["w_uv"],
      wts["g_qnorm"], wts["g_knorm"], rope[0], rope[1], rope[2])


def _attn_kernel(q_ref, k_ref, v_ref, o_ref):
    q = q_ref[...]
    tq = q.shape[0]
    seq = k_ref.shape[0]
    tk = min(TK, seq)

    n_chunks = seq // tk
    unroll = KV_UNROLL if n_chunks % KV_UNROLL == 0 else 1

    def chunk(off, carry):
        m, acc = carry
        kc = k_ref[pl.ds(off, tk), :]
        vc = v_ref[pl.ds(off, tk), :]
        s = lax.dot_general(q, kc, (((1,), (1,)), ((), ())), preferred_element_type=jnp.float32)
        m_new = jnp.maximum(m, jnp.max(s, axis=-1, keepdims=True))
        p = jnp.exp2(s - m_new).astype(jnp.bfloat16)
        alpha = jnp.exp2(m - m_new)
        acc = alpha * acc + jnp.dot(p, vc, preferred_element_type=jnp.float32)
        return m_new, acc

    def body(j, carry):
        for u in range(unroll):
            carry = chunk(pl.multiple_of((j * unroll + u) * tk, tk), carry)
        return carry

    m0 = jnp.full((tq, 1), _NEG_INF, jnp.float32)
    acc0 = jnp.zeros((tq, HEAD_SLOT), jnp.float32)
    _, acc = lax.fori_loop(0, n_chunks // unroll, body, (m0, acc0))
    o_ref[...] = (acc / acc[:, V_HEAD:V_HEAD + 1]).astype(jnp.bfloat16)


def _attention(q, k, v):
    bsz, seq, hw = q.shape
    tq = min(TQ, seq)
    return pl.pallas_call(
        _attn_kernel,
        grid=(bsz, N_HEADS, seq // tq),
        in_specs=[pl.BlockSpec((None, tq, HEAD_SLOT), lambda b, h, i: (b, i, h)),
                  pl.BlockSpec((None, seq, HEAD_SLOT), lambda b, h, i: (b, 0, h)),
                  pl.BlockSpec((None, seq, HEAD_SLOT), lambda b, h, i: (b, 0, h))],
        out_specs=pl.BlockSpec((None, tq, HEAD_SLOT), lambda b, h, i: (b, i, h)),
        out_shape=jax.ShapeDtypeStruct((bsz, seq, hw), jnp.bfloat16),
        compiler_params=_cparams(("parallel", "parallel", "arbitrary")),
        name="attn",
    )(q, k, v)


def _topk_rows(s, iota):
    vals, idxs = [], []
    big = jnp.float32(1e9)
    for _ in range(PEER_TOPK):
        m = jnp.max(s, axis=0, keepdims=True)
        am = jnp.min(jnp.where(s == m, iota, big), axis=0, keepdims=True)
        vals.append(m)
        idxs.append(am)
        s = jnp.where(iota == am, _NEG_INF, s)
    return jnp.concatenate(vals, axis=0), jnp.concatenate(idxs, axis=0)


def _take_rows(tab, idx):
    out = jnp.zeros_like(tab)
    for a in range(tab.shape[0]):
        out = out + jnp.where(idx == jnp.float32(a), tab[a:a + 1, :], 0.0)
    return out


_CAND = [(a, b) for a in range(PEER_TOPK) for b in range(PEER_TOPK) if (a + 1) * (b + 1) <= PEER_TOPK]
N_CAND = -(-len(_CAND) // SUBLANES) * SUBLANES
_PAD_LABEL = float(PEER_TOPK * PEER_TOPK)


def _candidate_tables():
    sel = np.zeros((2, N_CAND, PEER_TOPK), np.float32)
    lab = np.full((N_CAND, 1), _PAD_LABEL, np.float32)
    for r, (a, b) in enumerate(_CAND):
        sel[0, r, a] = 1.0
        sel[1, r, b] = 1.0
        lab[r, 0] = a * PEER_TOPK + b
    return sel, lab


def _mix_kernel(x_ref, a_ref, sgb_ref, o_ref, gate1_ref, scale2_ref, shift2_ref, g2_ref,
                w_ao_ref, w_out_ref, w_pq_ref, kcat_ref, sel_ref, lab_ref,
                x1_ref, h2_ref, e_ref, g_ref):
    ts = x_ref.shape[0]
    out_b = jnp.dot(o_ref[...], w_ao_ref[...], preferred_element_type=jnp.float32)
    merged = (a_ref[...] + sgb_ref[...] * out_b).astype(jnp.bfloat16)
    x1 = x_ref[...] + gate1_ref[...] * jnp.dot(merged, w_out_ref[...],
                                              preferred_element_type=jnp.float32)
    x1_ref[...] = x1
    h2 = _adaln(x1, g2_ref[...], scale2_ref[...], shift2_ref[...])
    h2_ref[...] = h2
    qp = jnp.dot(h2.astype(jnp.bfloat16), w_pq_ref[...],
                 preferred_element_type=jnp.float32).astype(jnp.bfloat16)

    iota_k = lax.broadcasted_iota(jnp.int32, (N_KEYS, ts), 0).astype(jnp.float32)
    lab = lab_ref[...]
    valid = lab < _PAD_LABEL
    sel_a = sel_ref[0]
    sel_b = sel_ref[1]
    exact = dict(preferred_element_type=jnp.float32, precision=lax.Precision.HIGHEST)
    e_rows, g_rows = [], []
    for hd in range(PEER_HEADS):
        qh = qp[:, hd * PEER_DK:(hd + 1) * PEER_DK]
        st = lax.dot_general(kcat_ref[hd], qh, (((1,), (1,)), ((), ())),
                             preferred_element_type=jnp.float32)
        v1, i1 = _topk_rows(st[:N_KEYS], iota_k)
        v2, i2 = _topk_rows(st[N_KEYS:], iota_k)
        comb = jnp.dot(sel_a, v1, **exact) + jnp.dot(sel_b, v2, **exact)
        sc, pos = _topk_rows(jnp.where(valid, comb, _NEG_INF), lab)
        ia = jnp.floor(pos * (1.0 / PEER_TOPK))
        ib = pos - ia * PEER_TOPK
        e_rows.append(_take_rows(i1, ia) * N_KEYS + _take_rows(i2, ib))
        p = jnp.exp(sc - sc[0:1, :])
        g_rows.append(p / jnp.sum(p, axis=0, keepdims=True))
    e_all = jnp.concatenate(e_rows, axis=0)
    g_all = jnp.concatenate(g_rows, axis=0)
    for c in range(ts // LANES):
        cs = slice(c * LANES, (c + 1) * LANES)
        e_ref[cs, :] = (e_all[:, cs].T * ROW_SUB).astype(jnp.int32)
        g_ref[cs, :] = g_all[:, cs].T


def _mix(x, a, sgb, o, gate1, scale2, shift2, wts):
    bsz, seq, _ = x.shape
    ts = min(TS_MIX, seq)
    tile = lambda b, i: (b, i, 0)
    per_b = lambda b, i: (b, 0, 0)
    full2 = lambda b, i: (0, 0)
    hw = N_HEADS * HEAD_SLOT
    big = pl.BlockSpec((None, ts, D_MODEL), tile)
    vec = pl.BlockSpec((None, 1, D_MODEL), per_b)
    picks = pl.BlockSpec((None, ts, N_PICKS), tile)
    sel, lab = _candidate_tables()
    return pl.pallas_call(
        _mix_kernel,
        grid=(bsz, seq // ts),
        in_specs=[big, big, big, pl.BlockSpec((None, ts, hw), tile), vec, vec, vec,
                  pl.BlockSpec((1, D_MODEL), full2),
                  pl.BlockSpec((hw, D_MODEL), full2),
                  pl.BlockSpec((D_MODEL, D_MODEL), full2),
                  pl.BlockSpec((D_MODEL, PEER_HEADS * PEER_DK), full2),
                  pl.BlockSpec((PEER_HEADS, 2 * N_KEYS, PEER_DK), lambda b, i: (0, 0, 0)),
                  pl.BlockSpec((2, N_CAND, PEER_TOPK), lambda b, i: (0, 0, 0)),
                  pl.BlockSpec((N_CAND, ts), full2)],
        out_specs=[big, big, picks, picks],
        out_shape=[jax.ShapeDtypeStruct((bsz, seq, D_MODEL), jnp.float32),
                   jax.ShapeDtypeStruct((bsz, seq, D_MODEL), jnp.float32),
                   jax.ShapeDtypeStruct((bsz, seq, N_PICKS), jnp.int32),
                   jax.ShapeDtypeStruct((bsz, seq, N_PICKS), jnp.float32)],
        compiler_params=_cparams(("parallel", "parallel")),
        name="mix",
    )(x, a, sgb, o, gate1, scale2, shift2, wts["g_norm2"], wts["w_attn_out"], wts["w_out"],
      wts["peer_wq"], wts["kcat"], jnp.asarray(sel), jnp.broadcast_to(jnp.asarray(lab), (N_CAND, ts)))


def _unpack_row(w):
    lo = lax.bitcast_convert_type(lax.shift_left(w, 16), jnp.float32)
    hi = lax.bitcast_convert_type(jnp.bitwise_and(w, jnp.int32(-65536)), jnp.float32)
    return lo, hi


def _gelu_tanh(x):
    c = math.sqrt(2.0 / math.pi)
    return 0.5 * x * (1.0 + jnp.tanh(c * (x + 0.044715 * (x * x * x))))


def _gather_rows(e_ref, t, tab_ref, pb):
    for j in range(N_PICKS // IDX_GROUP):
        row = e_ref.at[pl.ds(pl.multiple_of(t * N_PICKS + j * IDX_GROUP, IDX_GROUP), IDX_GROUP)]
        for c in range(IDX_GROUP):
            k = j * IDX_GROUP + c
            r = pl.multiple_of(row[c], ROW_SUB)
            pb[k * ROW_SUB:(k + 1) * ROW_SUB, :] = tab_ref[pl.ds(r, ROW_SUB), :]


def _pick_block(pb, s):
    return pb[pl.ds(s, N_PICKS, stride=ROW_SUB), :]


def _peer_u_kernel(e_ref, x_ref, g_ref, tab_ref, w_ref, pbuf0, pbuf1, rbuf, abuf):
    tt = g_ref.shape[0]
    pbufs = (pbuf0, pbuf1)

    def tok(i, carry):
        for u in range(TOK_UNROLL_U):
            t = i * TOK_UNROLL_U + u
            pb = pbufs[u % 2]
            x_blk = x_ref[pl.ds(pl.multiple_of(t * SUBLANES, SUBLANES), SUBLANES), :]
            _gather_rows(e_ref, t, tab_ref, pb)
            acc = None
            for s in range(ROW_SUB):
                lo, hi = _unpack_row(_pick_block(pb, s))
                term = lo * x_blk[s:s + 1, :] + hi * x_blk[ROW_SUB + s:ROW_SUB + s + 1, :]
                acc = term if acc is None else acc + term
            rbuf[pl.ds(pl.multiple_of(t * N_PICKS, N_PICKS), N_PICKS), :] = acc
        return carry

    lax.fori_loop(0, tt // TOK_UNROLL_U, tok, 0)

    ones2 = jnp.ones((2 * LANES, LANES), jnp.bfloat16)
    lane = lax.broadcasted_iota(jnp.int32, (N_PICKS, tt), 1)
    abuf[...] = jnp.zeros_like(abuf)
    rows = RED_TOKENS * N_PICKS

    def red(i, carry):
        rb = rbuf[pl.ds(pl.multiple_of(i * rows, rows), rows), :]
        r_hi = rb.astype(jnp.bfloat16)
        r_lo = (rb - r_hi.astype(jnp.float32)).astype(jnp.bfloat16)
        res = jnp.dot(jnp.concatenate([r_hi, r_lo], axis=1), ones2, preferred_element_type=jnp.float32)
        at = abuf[...]
        for u in range(RED_TOKENS):
            at = jnp.where(lane == i * RED_TOKENS + u, res[u * N_PICKS:(u + 1) * N_PICKS, :], at)
        abuf[...] = at
        return carry

    lax.fori_loop(0, tt // RED_TOKENS, red, 0)
    w_ref[...] = g_ref[...] * _gelu_tanh(abuf[...].T)


def _peer_u(e, h2_rows, g, tab, n):
    tt = TT_PEER
    return pl.pallas_call(
        _peer_u_kernel,
        grid=(n // tt,),
        in_specs=[pl.BlockSpec((tt * N_PICKS,), lambda i: (i,), memory_space=pltpu.SMEM),
                  pl.BlockSpec((tt * SUBLANES, LANES), lambda i: (i, 0)),
                  pl.BlockSpec((tt, N_PICKS), lambda i: (i, 0)),
                  pl.BlockSpec(memory_space=pltpu.VMEM)],
        out_specs=pl.BlockSpec((tt, N_PICKS), lambda i: (i, 0)),
        out_shape=jax.ShapeDtypeStruct((n, N_PICKS), jnp.float32),
        scratch_shapes=[pltpu.VMEM((N_PICKS * ROW_SUB, LANES), jnp.int32),
                        pltpu.VMEM((N_PICKS * ROW_SUB, LANES), jnp.int32),
                        pltpu.VMEM((tt * N_PICKS, LANES), jnp.float32),
                        pltpu.VMEM((N_PICKS, tt), jnp.float32)],
        compiler_params=_cparams(("arbitrary",)),
        name="peer_u",
    )(e, h2_rows, g, tab)


def _peer_v_kernel(e_ref, w_ref, x_ref, gate_ref, tab_ref, y_ref, pbuf0, pbuf1, whi_ref, wlo_ref, acc_ref):
    tt = x_ref.shape[0]
    pbufs = (pbuf0, pbuf1)
    w = w_ref[...]
    w_hi = w.astype(jnp.bfloat16)
    whi_ref[...] = w_hi
    wlo_ref[...] = (w - w_hi.astype(jnp.float32)).astype(jnp.bfloat16)
    sub = lax.broadcasted_iota(jnp.int32, (BF16_ROWS, D_MODEL), 0)
    acc_ref[...] = jnp.zeros_like(acc_ref)

    def process(pb, t):
        b0 = pl.multiple_of(lax.shift_left(lax.shift_right_logical(t, 4), 4), BF16_ROWS)
        lhs = jnp.concatenate([whi_ref[pl.ds(b0, BF16_ROWS), :], wlo_ref[pl.ds(b0, BF16_ROWS), :]], axis=0)
        los, his = [], []
        for s in range(ROW_SUB):
            lo, hi = _unpack_row(_pick_block(pb, s))
            los.append(lo.astype(jnp.bfloat16))
            his.append(hi.astype(jnp.bfloat16))
        rhs = jnp.concatenate(los + his, axis=1)
        res = jnp.dot(lhs, rhs, preferred_element_type=jnp.float32)
        r = res[:BF16_ROWS, :] + res[BF16_ROWS:, :]
        mask = sub == jnp.bitwise_and(t, BF16_ROWS - 1)
        acc_ref[pl.ds(b0, BF16_ROWS), :] = jnp.where(mask, r, acc_ref[pl.ds(b0, BF16_ROWS), :])

    _gather_rows(e_ref, 0, tab_ref, pbuf0)

    def step(i, carry):
        t0 = i * TOK_UNROLL_V
        for u in range(TOK_UNROLL_V):
            process(pbufs[u % 2], t0 + u)
            _gather_rows(e_ref, jnp.minimum(t0 + u + 1, tt - 1), tab_ref, pbufs[(u + 1) % 2])
        return carry

    lax.fori_loop(0, tt // TOK_UNROLL_V, step, 0)
    y_ref[...] = x_ref[...] + gate_ref[...] * acc_ref[...]


def _peer_v(e, w, x1, gate2, tab, seq, n_tc):
    n = x1.shape[0]
    tt = TT_PEER
    return pl.pallas_call(
        _peer_v_kernel,
        grid=(n_tc // tt,),
        in_specs=[pl.BlockSpec((tt * N_PICKS,), lambda i: (i,), memory_space=pltpu.SMEM),
                  pl.BlockSpec((tt, N_PICKS), lambda i: (i, 0)),
                  pl.BlockSpec((tt, D_MODEL), lambda i: (i, 0)),
                  pl.BlockSpec((None, 1, D_MODEL), lambda i: ((i * tt) // seq, 0, 0)),
                  pl.BlockSpec(memory_space=pltpu.VMEM)],
        out_specs=pl.BlockSpec((tt, D_MODEL), lambda i: (i, 0)),
        out_shape=jax.ShapeDtypeStruct((n, D_MODEL), jnp.float32),
        scratch_shapes=[pltpu.VMEM((N_PICKS * ROW_SUB, LANES), jnp.int32),
                        pltpu.VMEM((N_PICKS * ROW_SUB, LANES), jnp.int32),
                        pltpu.VMEM((tt, N_PICKS), jnp.bfloat16),
                        pltpu.VMEM((tt, N_PICKS), jnp.bfloat16),
                        pltpu.VMEM((tt, D_MODEL), jnp.float32)],
        compiler_params=_cparams(("arbitrary",)),
        name="peer_v",
    )(e, w, x1, gate2, tab)


SC_CORES = 2
SC_SUBCORES = 16
SC_LANES = 16
SC_WORKERS = SC_CORES * SC_SUBCORES
SC_HALF = N_PICKS // 2
SC_CHUNKS = ROW_WORDS // SC_LANES
SC_ACC_CHUNKS = 8
SC_SHARE_NUM, SC_SHARE_DEN = 3, 8
SC_ALIGN = 1024


def _sc_params():
    cp = pltpu.CompilerParams()
    if "needs_layout_passes" in pltpu.CompilerParams.__dataclass_fields__:
        cp = dataclasses.replace(cp, needs_layout_passes=False)
    return cp


def _gelu_tanh_via_exp(x):
    c = math.sqrt(2.0 / math.pi)
    z = c * (x + 0.044715 * (x * x * x))
    t = jnp.exp(-2.0 * jnp.abs(z))
    return 0.5 * x * (1.0 + jnp.sign(z) * (1.0 - t) / (1.0 + t))


def _peer_sc(e2, g, h2, tab_u, tab_v, off):
    n_sc = e2.shape[0]
    per_w = n_sc // SC_WORKERS
    mesh = plsc.VectorSubcoreMesh(core_axis_name="c", subcore_axis_name="s")
    ln = SC_LANES

    @functools.partial(
        pl.kernel, mesh=mesh,
        out_type=jax.ShapeDtypeStruct((n_sc, D_MODEL), jnp.float32),
        scratch_types=[pltpu.VMEM((SC_HALF,), jnp.int32), pltpu.VMEM((SC_HALF,), jnp.int32),
                       pltpu.VMEM((SC_HALF, ROW_WORDS), jnp.int32), pltpu.VMEM((SC_HALF, ROW_WORDS), jnp.int32),
                       pltpu.VMEM((D_MODEL,), jnp.float32), pltpu.VMEM((N_PICKS,), jnp.float32),
                       pltpu.VMEM((N_PICKS,), jnp.float32), pltpu.VMEM((D_MODEL,), jnp.float32),
                       pltpu.SemaphoreType.DMA, pltpu.SemaphoreType.DMA],
        compiler_params=_sc_params(),
    )
    def body(e_hbm, g_hbm, x_hbm, tu_hbm, tv_hbm, out_hbm,
             idx_a, idx_b, buf_a, buf_b, x_v, g_v, w_v, o_v, sem_a, sem_b):
        wid = lax.axis_index("s") * SC_CORES + lax.axis_index("c")
        lane = lax.iota(jnp.int32, ln)
        zero = jnp.zeros((ln,), jnp.float32)

        def u_half(buf, k0):
            @pl.loop(0, SC_HALF // ln)
            def _(grp):
                def chunk(j, accs):
                    xl = x_v[pl.ds(j * ln, ln)]
                    xh = x_v[pl.ds(ROW_WORDS + j * ln, ln)]
                    new = []
                    for kk in range(ln):
                        lo, hi = _unpack_row(buf[grp * ln + kk, pl.ds(j * ln, ln)])
                        new.append(accs[kk] + lo * xl + hi * xh)
                    return tuple(new)

                accs = lax.fori_loop(0, SC_CHUNKS, chunk, tuple(zero for _ in range(ln)))
                a = zero
                for kk in range(ln):
                    a = jnp.where(lane == kk, jnp.sum(accs[kk]), a)
                sl = pl.ds(k0 + grp * ln, ln)
                w_v[sl] = g_v[sl] * _gelu_tanh_via_exp(a)

        def v_half(buf, k0):
            @pl.loop(0, SC_CHUNKS // SC_ACC_CHUNKS)
            def _(jb):
                def pick(kq, accs):
                    wk = plsc.load_gather(w_v, [jnp.full((ln,), k0, jnp.int32) + kq])
                    new = []
                    for c in range(SC_ACC_CHUNKS):
                        lo, hi = _unpack_row(buf[kq, pl.ds((jb * SC_ACC_CHUNKS + c) * ln, ln)])
                        new.append(accs[2 * c] + wk * lo)
                        new.append(accs[2 * c + 1] + wk * hi)
                    return tuple(new)

                accs = lax.fori_loop(0, SC_HALF, pick, tuple(zero for _ in range(2 * SC_ACC_CHUNKS)))
                for c in range(SC_ACC_CHUNKS):
                    col = (jb * SC_ACC_CHUNKS + c) * ln
                    o_v[pl.ds(col, ln)] = o_v[pl.ds(col, ln)] + accs[2 * c]
                    o_v[pl.ds(ROW_WORDS + col, ln)] = o_v[pl.ds(ROW_WORDS + col, ln)] + accs[2 * c + 1]

        @pl.loop(0, per_w)
        def _(i):
            tl = wid * per_w + i
            tg = off + tl
            pltpu.sync_copy(e_hbm.at[tl, 0], idx_a)
            pltpu.sync_copy(e_hbm.at[tl, 1], idx_b)
            pltpu.sync_copy(x_hbm.at[tg], x_v)
            pltpu.sync_copy(g_hbm.at[tg], g_v)
            u_a = pltpu.async_copy(tu_hbm.at[idx_a], buf_a, sem_a)
            u_b = pltpu.async_copy(tu_hbm.at[idx_b], buf_b, sem_b)

            @pl.loop(0, D_MODEL // ln)
            def _(c):
                o_v[pl.ds(c * ln, ln)] = zero

            u_a.wait()
            u_half(buf_a, 0)
            v_a = pltpu.async_copy(tv_hbm.at[idx_a], buf_a, sem_a)
            u_b.wait()
            u_half(buf_b, SC_HALF)
            v_b = pltpu.async_copy(tv_hbm.at[idx_b], buf_b, sem_b)
            v_a.wait()
            v_half(buf_a, 0)
            v_b.wait()
            v_half(buf_b, SC_HALF)
            pltpu.sync_copy(o_v, out_hbm.at[tl])

    return body(e2, g, h2, tab_u, tab_v)


def _finish_kernel(y_in_ref, x_ref, gate_ref, p_ref, y_ref):
    del y_in_ref
    y_ref[...] = x_ref[...] + gate_ref[...] * p_ref[...]


def _peer_finish(y, x1, gate2, peer_tail, seq, n_tc):
    n = x1.shape[0]
    tt = TT_PEER
    first = n_tc // tt
    row = lambda i: (first + i, 0)
    return pl.pallas_call(
        _finish_kernel,
        grid=((n - n_tc) // tt,),
        in_specs=[pl.BlockSpec(memory_space=pl.ANY),
                  pl.BlockSpec((tt, D_MODEL), row),
                  pl.BlockSpec((None, 1, D_MODEL), lambda i: (((first + i) * tt) // seq, 0, 0)),
                  pl.BlockSpec((tt, D_MODEL), lambda i: (i, 0))],
        out_specs=pl.BlockSpec((tt, D_MODEL), row),
        out_shape=jax.ShapeDtypeStruct((n, D_MODEL), jnp.float32),
        input_output_aliases={0: 0},
        compiler_params=_cparams(("arbitrary",)),
        name="peer_finish",
    )(y, x1, gate2, peer_tail)


def _pad_heads(w, used, lead=0):
    r = w.shape[0]
    w3 = w.reshape(r, N_HEADS, used)
    out = jnp.zeros((r, N_HEADS, HEAD_SLOT), w.dtype).at[:, :, lead:lead + used].set(w3)
    return out.reshape(r, N_HEADS * HEAD_SLOT)


def _pack_table(t):
    tb = t.astype(jnp.bfloat16)
    lo = lax.bitcast_convert_type(tb[:, :ROW_WORDS], jnp.uint16).astype(jnp.uint32)
    hi = lax.bitcast_convert_type(tb[:, ROW_WORDS:], jnp.uint16).astype(jnp.uint32)
    words = lax.bitcast_convert_type(lo | (hi << 16), jnp.int32)
    return words.reshape(t.shape[0] * ROW_SUB, LANES)


def _prep_weights(g_norm1, w_in, conv_w, w_conv_out, g_q_lora, w_uq, g_kv_lora, w_ukv, g_qnorm,
                  g_knorm, w_attn_out, w_out, g_norm2, peer_wq, peer_k1, peer_k2, peer_u, peer_v):
    bf = jnp.bfloat16
    o1 = 3 * D_CONV
    o2 = o1 + Q_LORA
    o3 = o2 + KV_LORA
    o4 = o3 + QK_ROPE
    o5 = o4 + D_MODEL
    kr_slot = jnp.zeros((D_MODEL, HEAD_SLOT), w_in.dtype).at[:, QK_NOPE:QK_HEAD].set(w_in[:, o3:o4])
    w_in_r = jnp.concatenate([w_in[:, :o3], kr_slot, w_in[:, o4:]], axis=1).astype(bf)
    w_ukv3 = w_ukv.reshape(KV_LORA, N_HEADS, QK_NOPE + V_HEAD)
    w_uk = _pad_heads(w_ukv3[:, :, :QK_NOPE].reshape(KV_LORA, -1), QK_NOPE)
    w_uv = _pad_heads(w_ukv3[:, :, QK_NOPE:].reshape(KV_LORA, -1), V_HEAD)
    pad_gain = lambda g: jnp.zeros((1, HEAD_SLOT), jnp.float32).at[0, :QK_HEAD].set(g)
    w_ao = jnp.zeros((N_HEADS, HEAD_SLOT, D_MODEL), w_attn_out.dtype).at[:, :V_HEAD].set(
        w_attn_out.reshape(N_HEADS, V_HEAD, D_MODEL)).reshape(N_HEADS * HEAD_SLOT, D_MODEL)
    half = PEER_DK // 2
    kcat = jnp.zeros((PEER_HEADS, 2 * N_KEYS, PEER_DK), jnp.float32)
    kcat = kcat.at[:, :N_KEYS, :half].set(peer_k1).at[:, N_KEYS:, half:].set(peer_k2)
    return {
        "g_norm1": g_norm1.reshape(1, -1), "w_in": w_in_r, "conv_w": conv_w,
        "w_conv_out": w_conv_out.astype(bf), "g_q_lora": g_q_lora.reshape(1, -1),
        "w_uq": _pad_heads(w_uq, QK_HEAD).astype(bf), "g_kv_lora": g_kv_lora.reshape(1, -1),
        "w_uk": w_uk.astype(bf), "w_uv": w_uv.astype(bf),
        "g_qnorm": pad_gain(g_qnorm), "g_knorm": pad_gain(g_knorm),
        "w_attn_out": w_ao.astype(bf), "w_out": w_out.astype(bf), "g_norm2": g_norm2.reshape(1, -1),
        "peer_wq": peer_wq.astype(bf), "kcat": kcat.astype(bf),
        "tab_u": _pack_table(peer_u), "tab_v": _pack_table(peer_v),
    }


def _rope_tables(seq):
    pos = jnp.arange(seq, dtype=jnp.float32)
    inv = ROPE_THETA ** (-jnp.arange(0, QK_ROPE, 2, dtype=jnp.float32) / QK_ROPE)
    ang = pos[:, None] * inv[None, :]
    cos, sin = jnp.cos(ang), jnp.sin(ang)
    t1 = slice(QK_NOPE, QK_NOPE + HALF_ROPE)
    t2 = slice(QK_NOPE + HALF_ROPE, QK_HEAD)
    cos_t = jnp.ones((seq, HEAD_SLOT), jnp.float32).at[:, t1].set(cos).at[:, t2].set(cos)
    sin_a = jnp.zeros((seq, HEAD_SLOT), jnp.float32).at[:, t1].set(-sin)
    sin_b = jnp.zeros((seq, HEAD_SLOT), jnp.float32).at[:, t2].set(sin)
    return cos_t, sin_a, sin_b


def _encoder_layer(x, c, w_ada, b_ada, wts):
    bsz, seq, d = x.shape
    n = bsz * seq
    ada = _ada(c, w_ada, b_ada)[:, None, :]
    shift1, scale1, gate1, shift2, scale2, gate2 = jnp.split(ada, 6, axis=-1)
    q, k, v, a, sgb = _inproj(x, scale1, shift1, wts, _rope_tables(seq))
    o = _attention(q, k, v)
    x1, h2, e, g = _mix(x, a, sgb, o, gate1, scale2, shift2, wts)
    n_sc = (n * SC_SHARE_NUM // SC_SHARE_DEN) // SC_ALIGN * SC_ALIGN
    n_tc = n - n_sc
    e = e.reshape(n, N_PICKS)
    g = g.reshape(n, N_PICKS)
    h2 = h2.reshape(n, d)
    x1 = x1.reshape(n, d)
    if n_sc:
        ids = lax.shift_right_logical(e[n_tc:], ROW_SUB.bit_length() - 1)
        peer_tail = _peer_sc(ids.reshape(n_sc, 2, SC_HALF), g, h2,
                             wts["tab_u"].reshape(-1, ROW_WORDS), wts["tab_v"].reshape(-1, ROW_WORDS), n_tc)
    e = e.reshape(n * N_PICKS)
    w = _peer_u(e, h2.reshape(n * SUBLANES, LANES), g, wts["tab_u"], n_tc)
    y = _peer_v(e, w, x1, gate2, wts["tab_v"], seq, n_tc)
    if n_sc:
        y = _peer_finish(y, x1, gate2, peer_tail, seq, n_tc)
    return y.reshape(bsz, seq, d)


def kernel(x_prompt, x_sample, c_prompt, c_sample, w_ada, b_ada, g_norm1, w_in, conv_w, w_conv_out, g_q_lora, w_uq, g_kv_lora, w_ukv, g_qnorm, g_knorm, w_attn_out, w_out, g_norm2, peer_wq, peer_k1, peer_k2, peer_u, peer_v):
    wts = _prep_weights(g_norm1[0], w_in[0], conv_w[0], w_conv_out[0], g_q_lora[0], w_uq[0],
                        g_kv_lora[0], w_ukv[0], g_qnorm[0], g_knorm[0], w_attn_out[0], w_out[0],
                        g_norm2[0], peer_wq[0], peer_k1[0], peer_k2[0], peer_u[0], peer_v[0])
    return (_encoder_layer(x_prompt, c_prompt, w_ada[0], b_ada[0], wts),
            _encoder_layer(x_sample, c_sample, w_ada[0], b_ada[0], wts))
```

```python
import dataclasses
import functools
import math

import jax
import jax.numpy as jnp
import numpy as np
from jax import lax
from jax.experimental import pallas as pl
from jax.experimental.pallas import tpu as pltpu
from jax.experimental.pallas import tpu_sc as plsc

D_MODEL = 1024
D_CONV = 512
N_HEADS = 8
QK_NOPE = 64
QK_ROPE = 32
V_HEAD = 64
Q_LORA = 256
KV_LORA = 128
QK_HEAD = QK_NOPE + QK_ROPE
ROPE_THETA = 10000.0
PEER_HEADS = 8
N_KEYS = 128
PEER_DK = 128
PEER_TOPK = 16
EPS = 1e-6

LANES = 128
SUBLANES = 8
HEAD_SLOT = LANES
HALF_ROPE = QK_ROPE // 2
N_PICKS = PEER_HEADS * PEER_TOPK
ROW_WORDS = D_MODEL // 2
ROW_SUB = ROW_WORDS // LANES
VMEM_LIMIT = 56 * 1024 * 1024

C_BG, C_CG, C_HC = 0, 512, 1024
C_CQ = 1536
C_CKV = C_CQ + Q_LORA
C_KR = C_CKV + KV_LORA
C_GA = C_KR + HEAD_SLOT
C_GB = C_GA + D_MODEL
IN_COLS_R = C_GB + D_MODEL

TS_IN = 512
TQ = 512
TK = 2048
KV_UNROLL = 2
TS_MIX = 256
TT_PEER = 128
TOK_UNROLL_U = 2
TOK_UNROLL_V = 4
RED_TOKENS = 8
BF16_ROWS = 16
IDX_GROUP = 8

_NEG_INF = float("-inf")


def _cparams(sem):
    return pltpu.CompilerParams(dimension_semantics=sem, vmem_limit_bytes=VMEM_LIMIT)


def _ada_kernel(c_ref, w_ref, b_ref, o_ref):
    o_ref[...] = jnp.dot(c_ref[...], w_ref[...], preferred_element_type=jnp.float32,
                         precision=lax.Precision.HIGHEST) + b_ref[...]


def _ada(c, w_ada, b_ada):
    bsz = c.shape[0]
    rows = -(-bsz // SUBLANES) * SUBLANES
    c_p = jnp.zeros((rows, D_MODEL), jnp.float32).at[:bsz].set(c)
    out = pl.pallas_call(
        _ada_kernel,
        grid=(6,),
        in_specs=[pl.BlockSpec((rows, D_MODEL), lambda j: (0, 0)),
                  pl.BlockSpec((D_MODEL, D_MODEL), lambda j: (0, j)),
                  pl.BlockSpec((1, D_MODEL), lambda j: (0, j))],
        out_specs=pl.BlockSpec((rows, D_MODEL), lambda j: (0, j)),
        out_shape=jax.ShapeDtypeStruct((rows, 6 * D_MODEL), jnp.float32),
        compiler_params=_cparams(("arbitrary",)),
        name="ada",
    )(c_p, w_ada, b_ada.reshape(1, -1))
    return out[:bsz]


def _adaln(x, g, scale, shift):
    ms = jnp.mean(x * x, axis=-1, keepdims=True)
    return (x * lax.rsqrt(ms + EPS) * g) * (1.0 + scale) + shift


def _rms_rows(x, g, n):
    ms = jnp.sum(x * x, axis=-1, keepdims=True) * (1.0 / n)
    return x * lax.rsqrt(ms + EPS) * g


def _rope(x, cos, sin_a, sin_b):
    return (x * cos + pltpu.roll(x, LANES - HALF_ROPE, axis=1) * sin_a
            + pltpu.roll(x, HALF_ROPE, axis=1) * sin_b)


def _inproj_kernel(x_ref, xp_ref, xn_ref, scale_ref, shift_ref, g1_ref, w_in_ref, conv_w_ref,
                   w_co_ref, gql_ref, w_uq_ref, gkvl_ref, w_uk_ref, w_uv_ref, gqn_ref, gkn_ref,
                   cos_ref, sa_ref, sb_ref, tab_u_sc_ref, tab_v_sc_ref,
                   q_ref, k_ref, v_ref, a_ref, sgb_ref):
    del tab_u_sc_ref, tab_v_sc_ref
    i = pl.program_id(1)
    n_i = pl.num_programs(1)
    g1 = g1_ref[...]
    scale = scale_ref[...]
    shift = shift_ref[...]
    ts = x_ref.shape[0]

    h = _adaln(x_ref[...], g1, scale, shift).astype(jnp.bfloat16)

    def proj(lo, width):
        return jnp.dot(h, w_in_ref[:, lo:lo + width], preferred_element_type=jnp.float32)

    z = proj(C_CG, D_CONV) * proj(C_HC, D_CONV)

    def halo_z(xh_ref):
        hh = _adaln(xh_ref[...], g1, scale, shift).astype(jnp.bfloat16)
        zc = jnp.dot(hh, w_in_ref[:, C_CG:C_CG + 2 * D_CONV], preferred_element_type=jnp.float32)
        return zc[:, :D_CONV] * zc[:, D_CONV:]

    z_prev = halo_z(xp_ref)[SUBLANES - 1:SUBLANES, :]
    z_next = halo_z(xn_ref)[0:1, :]
    z_prev = jnp.where(i == 0, 0.0, z_prev)
    z_next = jnp.where(i == n_i - 1, 0.0, z_next)
    row = lax.broadcasted_iota(jnp.int32, (ts, D_CONV), 0)
    z_up = jnp.where(row == 0, z_prev, pltpu.roll(z, 1, axis=0))
    z_dn = jnp.where(row == ts - 1, z_next, pltpu.roll(z, ts - 1, axis=0))
    cw = conv_w_ref[...]
    y = z_up * cw[0:1, :] + z * cw[1:2, :] + z_dn * cw[2:3, :]
    out_a = jnp.dot((proj(C_BG, D_CONV) * y).astype(jnp.bfloat16), w_co_ref[...],
                    preferred_element_type=jnp.float32)
    a_ref[...] = jax.nn.sigmoid(proj(C_GA, D_MODEL)) * out_a
    sgb_ref[...] = jax.nn.sigmoid(proj(C_GB, D_MODEL))

    cos = cos_ref[...]
    sin_a = sa_ref[...]
    sin_b = sb_ref[...]
    cq = _rms_rows(proj(C_CQ, Q_LORA), gql_ref[...], Q_LORA).astype(jnp.bfloat16)
    qf = jnp.dot(cq, w_uq_ref[...], preferred_element_type=jnp.float32)
    ckv = _rms_rows(proj(C_CKV, KV_LORA), gkvl_ref[...], KV_LORA).astype(jnp.bfloat16)
    kf = jnp.dot(ckv, w_uk_ref[...], preferred_element_type=jnp.float32)
    lane = lax.broadcasted_iota(jnp.int32, (1, N_HEADS * HEAD_SLOT), 1)
    ones_lane = (jnp.bitwise_and(lane, HEAD_SLOT - 1) == V_HEAD).astype(jnp.float32)
    v_ref[...] = (jnp.dot(ckv, w_uv_ref[...], preferred_element_type=jnp.float32)
                  + ones_lane).astype(jnp.bfloat16)
    kr = proj(C_KR, HEAD_SLOT)
    gqn = gqn_ref[...]
    gkn = gkn_ref[...]
    q_scale = QK_HEAD ** -0.5 * math.log2(math.e)
    for hd in range(N_HEADS):
        sl = slice(hd * HEAD_SLOT, (hd + 1) * HEAD_SLOT)
        qh = _rope(_rms_rows(qf[:, sl], gqn, QK_HEAD), cos, sin_a, sin_b)
        q_ref[:, sl] = (qh * q_scale).astype(jnp.bfloat16)
        kh = _rope(_rms_rows(kf[:, sl] + kr, gkn, QK_HEAD), cos, sin_a, sin_b)
        k_ref[:, sl] = kh.astype(jnp.bfloat16)


def _inproj(x, scale1, shift1, wts, rope):
    bsz, seq, _ = x.shape
    ts = min(TS_IN, seq)
    n_i = seq // ts
    nb8 = seq // SUBLANES
    per8 = ts // SUBLANES
    tile = lambda b, i: (b, i, 0)
    per_b = lambda b, i: (b, 0, 0)
    full2 = lambda b, i: (0, 0)
    hw = N_HEADS * HEAD_SLOT
    in_specs = [
        pl.BlockSpec((None, ts, D_MODEL), tile),
        pl.BlockSpec((None, SUBLANES, D_MODEL), lambda b, i: (b, jnp.maximum(i * per8 - 1, 0), 0)),
        pl.BlockSpec((None, SUBLANES, D_MODEL), lambda b, i: (b, jnp.minimum((i + 1) * per8, nb8 - 1), 0)),
        pl.BlockSpec((None, 1, D_MODEL), per_b),
        pl.BlockSpec((None, 1, D_MODEL), per_b),
        pl.BlockSpec((1, D_MODEL), full2),
        pl.BlockSpec((D_MODEL, IN_COLS_R), full2),
        pl.BlockSpec((3, D_CONV), full2),
        pl.BlockSpec((D_CONV, D_MODEL), full2),
        pl.BlockSpec((1, Q_LORA), full2),
        pl.BlockSpec((Q_LORA, hw), full2),
        pl.BlockSpec((1, KV_LORA), full2),
        pl.BlockSpec((KV_LORA, hw), full2),
        pl.BlockSpec((KV_LORA, hw), full2),
        pl.BlockSpec((1, HEAD_SLOT), full2),
        pl.BlockSpec((1, HEAD_SLOT), full2),
        pl.BlockSpec((ts, HEAD_SLOT), lambda b, i: (i, 0)),
        pl.BlockSpec((ts, HEAD_SLOT), lambda b, i: (i, 0)),
        pl.BlockSpec((ts, HEAD_SLOT), lambda b, i: (i, 0)),
        pl.BlockSpec(memory_space=pl.ANY),
        pl.BlockSpec(memory_space=pl.ANY),
    ]
    out_specs = [pl.BlockSpec((None, ts, hw), tile)] * 3 + [pl.BlockSpec((None, ts, D_MODEL), tile)] * 2
    out_shape = ([jax.ShapeDtypeStruct((bsz, seq, hw), jnp.bfloat16)] * 3
                 + [jax.ShapeDtypeStruct((bsz, seq, D_MODEL), jnp.float32)] * 2)
    return pl.pallas_call(
        _inproj_kernel,
        grid=(bsz, n_i),
        in_specs=in_specs,
        out_specs=out_specs,
        out_shape=out_shape,
        compiler_params=_cparams(("parallel", "parallel")),
        name="inproj",
    )(x, x, x, scale1, shift1, wts["g_norm1"], wts["w_in"], wts["conv_w"], wts["w_conv_out"],
      wts["g_q_lora"], wts["w_uq"], wts["g_kv_lora"], wts["w_uk"], wts["w_uv"],
      wts["g_qnorm"], wts["g_knorm"], rope[0], rope[1], rope[2], wts["tab_u_sc"], wts["tab_v_sc"])


def _attn_kernel(q_ref, k_ref, v_ref, o_ref):
    q = q_ref[...]
    tq = q.shape[0]
    seq = k_ref.shape[0]
    tk = min(TK, seq)

    n_chunks = seq // tk
    unroll = KV_UNROLL if n_chunks % KV_UNROLL == 0 else 1

    def chunk(off, carry):
        m, acc = carry
        kc = k_ref[pl.ds(off, tk), :]
        vc = v_ref[pl.ds(off, tk), :]
        s = lax.dot_general(q, kc, (((1,), (1,)), ((), ())), preferred_element_type=jnp.float32)
        m_new = jnp.maximum(m, jnp.max(s, axis=-1, keepdims=True))
        p = jnp.exp2(s - m_new).astype(jnp.bfloat16)
        alpha = jnp.exp2(m - m_new)
        acc = alpha * acc + jnp.dot(p, vc, preferred_element_type=jnp.float32)
        return m_new, acc

    def body(j, carry):
        for u in range(unroll):
            carry = chunk(pl.multiple_of((j * unroll + u) * tk, tk), carry)
        return carry

    m0 = jnp.full((tq, 1), _NEG_INF, jnp.float32)
    acc0 = jnp.zeros((tq, HEAD_SLOT), jnp.float32)
    _, acc = lax.fori_loop(0, n_chunks // unroll, body, (m0, acc0))
    o_ref[...] = (acc / acc[:, V_HEAD:V_HEAD + 1]).astype(jnp.bfloat16)


def _attention(q, k, v):
    bsz, seq, hw = q.shape
    tq = min(TQ, seq)
    return pl.pallas_call(
        _attn_kernel,
        grid=(bsz, N_HEADS, seq // tq),
        in_specs=[pl.BlockSpec((None, tq, HEAD_SLOT), lambda b, h, i: (b, i, h)),
                  pl.BlockSpec((None, seq, HEAD_SLOT), lambda b, h, i: (b, 0, h)),
                  pl.BlockSpec((None, seq, HEAD_SLOT), lambda b, h, i: (b, 0, h))],
        out_specs=pl.BlockSpec((None, tq, HEAD_SLOT), lambda b, h, i: (b, i, h)),
        out_shape=jax.ShapeDtypeStruct((bsz, seq, hw), jnp.bfloat16),
        compiler_params=_cparams(("parallel", "parallel", "arbitrary")),
        name="attn",
    )(q, k, v)


def _topk_rows(s, iota):
    vals, idxs = [], []
    big = jnp.float32(1e9)
    for _ in range(PEER_TOPK):
        m = jnp.max(s, axis=0, keepdims=True)
        am = jnp.min(jnp.where(s == m, iota, big), axis=0, keepdims=True)
        vals.append(m)
        idxs.append(am)
        s = jnp.where(iota == am, _NEG_INF, s)
    return jnp.concatenate(vals, axis=0), jnp.concatenate(idxs, axis=0)


def _take_rows(tab, idx):
    out = jnp.zeros_like(tab)
    for a in range(tab.shape[0]):
        out = out + jnp.where(idx == jnp.float32(a), tab[a:a + 1, :], 0.0)
    return out


_CAND = [(a, b) for a in range(PEER_TOPK) for b in range(PEER_TOPK) if (a + 1) * (b + 1) <= PEER_TOPK]
N_CAND = -(-len(_CAND) // SUBLANES) * SUBLANES
_PAD_LABEL = float(PEER_TOPK * PEER_TOPK)


def _candidate_tables():
    sel = np.zeros((2, N_CAND, PEER_TOPK), np.float32)
    lab = np.full((N_CAND, 1), _PAD_LABEL, np.float32)
    for r, (a, b) in enumerate(_CAND):
        sel[0, r, a] = 1.0
        sel[1, r, b] = 1.0
        lab[r, 0] = a * PEER_TOPK + b
    return sel, lab


def _mix_kernel(x_ref, a_ref, sgb_ref, o_ref, gate1_ref, scale2_ref, shift2_ref, g2_ref,
                w_ao_ref, w_out_ref, w_pq_ref, kcat_ref, sel_ref, lab_ref,
                x1_ref, h2_ref, e_ref, g_ref):
    ts = x_ref.shape[0]
    out_b = jnp.dot(o_ref[...], w_ao_ref[...], preferred_element_type=jnp.float32)
    merged = (a_ref[...] + sgb_ref[...] * out_b).astype(jnp.bfloat16)
    x1 = x_ref[...] + gate1_ref[...] * jnp.dot(merged, w_out_ref[...],
                                              preferred_element_type=jnp.float32)
    x1_ref[...] = x1
    h2 = _adaln(x1, g2_ref[...], scale2_ref[...], shift2_ref[...])
    h2_ref[...] = h2
    qp = jnp.dot(h2.astype(jnp.bfloat16), w_pq_ref[...],
                 preferred_element_type=jnp.float32).astype(jnp.bfloat16)

    iota_k = lax.broadcasted_iota(jnp.int32, (N_KEYS, ts), 0).astype(jnp.float32)
    lab = lab_ref[...]
    valid = lab < _PAD_LABEL
    sel_a = sel_ref[0]
    sel_b = sel_ref[1]
    exact = dict(preferred_element_type=jnp.float32, precision=lax.Precision.HIGHEST)
    e_rows, g_rows = [], []
    for hd in range(PEER_HEADS):
        qh = qp[:, hd * PEER_DK:(hd + 1) * PEER_DK]
        st = lax.dot_general(kcat_ref[hd], qh, (((1,), (1,)), ((), ())),
                             preferred_element_type=jnp.float32)
        v1, i1 = _topk_rows(st[:N_KEYS], iota_k)
        v2, i2 = _topk_rows(st[N_KEYS:], iota_k)
        comb = jnp.dot(sel_a, v1, **exact) + jnp.dot(sel_b, v2, **exact)
        sc, pos = _topk_rows(jnp.where(valid, comb, _NEG_INF), lab)
        ia = jnp.floor(pos * (1.0 / PEER_TOPK))
        ib = pos - ia * PEER_TOPK
        e_rows.append(_take_rows(i1, ia) * N_KEYS + _take_rows(i2, ib))
        p = jnp.exp(sc - sc[0:1, :])
        g_rows.append(p / jnp.sum(p, axis=0, keepdims=True))
    e_all = jnp.concatenate(e_rows, axis=0)
    g_all = jnp.concatenate(g_rows, axis=0)
    for c in range(ts // LANES):
        cs = slice(c * LANES, (c + 1) * LANES)
        e_ref[cs, :] = (e_all[:, cs].T * ROW_SUB).astype(jnp.int32)
        g_ref[cs, :] = g_all[:, cs].T


def _mix(x, a, sgb, o, gate1, scale2, shift2, wts):
    bsz, seq, _ = x.shape
    ts = min(TS_MIX, seq)
    tile = lambda b, i: (b, i, 0)
    per_b = lambda b, i: (b, 0, 0)
    full2 = lambda b, i: (0, 0)
    hw = N_HEADS * HEAD_SLOT
    big = pl.BlockSpec((None, ts, D_MODEL), tile)
    vec = pl.BlockSpec((None, 1, D_MODEL), per_b)
    picks = pl.BlockSpec((None, ts, N_PICKS), tile)
    sel, lab = _candidate_tables()
    return pl.pallas_call(
        _mix_kernel,
        grid=(bsz, seq // ts),
        in_specs=[big, big, big, pl.BlockSpec((None, ts, hw), tile), vec, vec, vec,
                  pl.BlockSpec((1, D_MODEL), full2),
                  pl.BlockSpec((hw, D_MODEL), full2),
                  pl.BlockSpec((D_MODEL, D_MODEL), full2),
                  pl.BlockSpec((D_MODEL, PEER_HEADS * PEER_DK), full2),
                  pl.BlockSpec((PEER_HEADS, 2 * N_KEYS, PEER_DK), lambda b, i: (0, 0, 0)),
                  pl.BlockSpec((2, N_CAND, PEER_TOPK), lambda b, i: (0, 0, 0)),
                  pl.BlockSpec((N_CAND, ts), full2)],
        out_specs=[big, big, picks, picks],
        out_shape=[jax.ShapeDtypeStruct((bsz, seq, D_MODEL), jnp.float32),
                   jax.ShapeDtypeStruct((bsz, seq, D_MODEL), jnp.float32),
                   jax.ShapeDtypeStruct((bsz, seq, N_PICKS), jnp.int32),
                   jax.ShapeDtypeStruct((bsz, seq, N_PICKS), jnp.float32)],
        compiler_params=_cparams(("parallel", "parallel")),
        name="mix",
    )(x, a, sgb, o, gate1, scale2, shift2, wts["g_norm2"], wts["w_attn_out"], wts["w_out"],
      wts["peer_wq"], wts["kcat"], jnp.asarray(sel), jnp.broadcast_to(jnp.asarray(lab), (N_CAND, ts)))


def _unpack_row(w):
    lo = lax.bitcast_convert_type(lax.shift_left(w, 16), jnp.float32)
    hi = lax.bitcast_convert_type(jnp.bitwise_and(w, jnp.int32(-65536)), jnp.float32)
    return lo, hi


def _gelu_tanh(x):
    c = math.sqrt(2.0 / math.pi)
    return 0.5 * x * (1.0 + jnp.tanh(c * (x + 0.044715 * (x * x * x))))


def _gather_rows(e_ref, t, tab_ref, pb):
    for j in range(N_PICKS // IDX_GROUP):
        row = e_ref.at[pl.ds(pl.multiple_of(t * N_PICKS + j * IDX_GROUP, IDX_GROUP), IDX_GROUP)]
        for c in range(IDX_GROUP):
            k = j * IDX_GROUP + c
            r = pl.multiple_of(row[c], ROW_SUB)
            pb[k * ROW_SUB:(k + 1) * ROW_SUB, :] = tab_ref[pl.ds(r, ROW_SUB), :]


def _pick_block(pb, s):
    return pb[pl.ds(s, N_PICKS, stride=ROW_SUB), :]


def _peer_u_kernel(e_ref, x_ref, g_ref, tab_ref, w_ref, pbuf0, pbuf1, rbuf, abuf):
    tt = g_ref.shape[0]
    pbufs = (pbuf0, pbuf1)

    def tok(i, carry):
        for u in range(TOK_UNROLL_U):
            t = i * TOK_UNROLL_U + u
            pb = pbufs[u % 2]
            x_blk = x_ref[pl.ds(pl.multiple_of(t * SUBLANES, SUBLANES), SUBLANES), :]
            _gather_rows(e_ref, t, tab_ref, pb)
            acc = None
            for s in range(ROW_SUB):
                lo, hi = _unpack_row(_pick_block(pb, s))
                term = lo * x_blk[s:s + 1, :] + hi * x_blk[ROW_SUB + s:ROW_SUB + s + 1, :]
                acc = term if acc is None else acc + term
            rbuf[pl.ds(pl.multiple_of(t * N_PICKS, N_PICKS), N_PICKS), :] = acc
        return carry

    lax.fori_loop(0, tt // TOK_UNROLL_U, tok, 0)

    ones2 = jnp.ones((2 * LANES, LANES), jnp.bfloat16)
    lane = lax.broadcasted_iota(jnp.int32, (N_PICKS, tt), 1)
    abuf[...] = jnp.zeros_like(abuf)
    rows = RED_TOKENS * N_PICKS

    def red(i, carry):
        rb = rbuf[pl.ds(pl.multiple_of(i * rows, rows), rows), :]
        r_hi = rb.astype(jnp.bfloat16)
        r_lo = (rb - r_hi.astype(jnp.float32)).astype(jnp.bfloat16)
        res = jnp.dot(jnp.concatenate([r_hi, r_lo], axis=1), ones2, preferred_element_type=jnp.float32)
        at = abuf[...]
        for u in range(RED_TOKENS):
            at = jnp.where(lane == i * RED_TOKENS + u, res[u * N_PICKS:(u + 1) * N_PICKS, :], at)
        abuf[...] = at
        return carry

    lax.fori_loop(0, tt // RED_TOKENS, red, 0)
    w_ref[...] = g_ref[...] * _gelu_tanh(abuf[...].T)


def _peer_u(e, h2_rows, g, tab, n):
    tt = TT_PEER
    return pl.pallas_call(
        _peer_u_kernel,
        grid=(n // tt,),
        in_specs=[pl.BlockSpec((tt * N_PICKS,), lambda i: (i,), memory_space=pltpu.SMEM),
                  pl.BlockSpec((tt * SUBLANES, LANES), lambda i: (i, 0)),
                  pl.BlockSpec((tt, N_PICKS), lambda i: (i, 0)),
                  pl.BlockSpec(memory_space=pltpu.VMEM)],
        out_specs=pl.BlockSpec((tt, N_PICKS), lambda i: (i, 0)),
        out_shape=jax.ShapeDtypeStruct((n, N_PICKS), jnp.float32),
        scratch_shapes=[pltpu.VMEM((N_PICKS * ROW_SUB, LANES), jnp.int32),
                        pltpu.VMEM((N_PICKS * ROW_SUB, LANES), jnp.int32),
                        pltpu.VMEM((tt * N_PICKS, LANES), jnp.float32),
                        pltpu.VMEM((N_PICKS, tt), jnp.float32)],
        compiler_params=_cparams(("arbitrary",)),
        name="peer_u",
    )(e, h2_rows, g, tab)


def _peer_v_kernel(e_ref, w_ref, x_ref, gate_ref, tab_ref, y_ref, pbuf0, pbuf1, whi_ref, wlo_ref, acc_ref):
    tt = x_ref.shape[0]
    pbufs = (pbuf0, pbuf1)
    w = w_ref[...]
    w_hi = w.astype(jnp.bfloat16)
    whi_ref[...] = w_hi
    wlo_ref[...] = (w - w_hi.astype(jnp.float32)).astype(jnp.bfloat16)
    sub = lax.broadcasted_iota(jnp.int32, (BF16_ROWS, D_MODEL), 0)
    acc_ref[...] = jnp.zeros_like(acc_ref)

    def process(pb, t):
        b0 = pl.multiple_of(lax.shift_left(lax.shift_right_logical(t, 4), 4), BF16_ROWS)
        lhs = jnp.concatenate([whi_ref[pl.ds(b0, BF16_ROWS), :], wlo_ref[pl.ds(b0, BF16_ROWS), :]], axis=0)
        los, his = [], []
        for s in range(ROW_SUB):
            lo, hi = _unpack_row(_pick_block(pb, s))
            los.append(lo.astype(jnp.bfloat16))
            his.append(hi.astype(jnp.bfloat16))
        rhs = jnp.concatenate(los + his, axis=1)
        res = jnp.dot(lhs, rhs, preferred_element_type=jnp.float32)
        r = res[:BF16_ROWS, :] + res[BF16_ROWS:, :]
        mask = sub == jnp.bitwise_and(t, BF16_ROWS - 1)
        acc_ref[pl.ds(b0, BF16_ROWS), :] = jnp.where(mask, r, acc_ref[pl.ds(b0, BF16_ROWS), :])

    _gather_rows(e_ref, 0, tab_ref, pbuf0)

    def step(i, carry):
        t0 = i * TOK_UNROLL_V
        for u in range(TOK_UNROLL_V):
            process(pbufs[u % 2], t0 + u)
            _gather_rows(e_ref, jnp.minimum(t0 + u + 1, tt - 1), tab_ref, pbufs[(u + 1) % 2])
        return carry

    lax.fori_loop(0, tt // TOK_UNROLL_V, step, 0)
    y_ref[...] = x_ref[...] + gate_ref[...] * acc_ref[...]


def _peer_v(e, w, x1, gate2, tab, seq, n_tc):
    n = x1.shape[0]
    tt = TT_PEER
    return pl.pallas_call(
        _peer_v_kernel,
        grid=(n_tc // tt,),
        in_specs=[pl.BlockSpec((tt * N_PICKS,), lambda i: (i,), memory_space=pltpu.SMEM),
                  pl.BlockSpec((tt, N_PICKS), lambda i: (i, 0)),
                  pl.BlockSpec((tt, D_MODEL), lambda i: (i, 0)),
                  pl.BlockSpec((None, 1, D_MODEL), lambda i: ((i * tt) // seq, 0, 0)),
                  pl.BlockSpec(memory_space=pltpu.VMEM)],
        out_specs=pl.BlockSpec((tt, D_MODEL), lambda i: (i, 0)),
        out_shape=jax.ShapeDtypeStruct((n, D_MODEL), jnp.float32),
        scratch_shapes=[pltpu.VMEM((N_PICKS * ROW_SUB, LANES), jnp.int32),
                        pltpu.VMEM((N_PICKS * ROW_SUB, LANES), jnp.int32),
                        pltpu.VMEM((tt, N_PICKS), jnp.bfloat16),
                        pltpu.VMEM((tt, N_PICKS), jnp.bfloat16),
                        pltpu.VMEM((tt, D_MODEL), jnp.float32)],
        compiler_params=_cparams(("arbitrary",)),
        name="peer_v",
    )(e, w, x1, gate2, tab)


SC_CORES = 2
SC_SUBCORES = 16
SC_LANES = 16
SC_WORKERS = SC_CORES * SC_SUBCORES
SC_HALF = N_PICKS // 2
SC_CHUNKS = ROW_WORDS // SC_LANES
SC_ACC_CHUNKS = 8
SC_SHARE_NUM, SC_SHARE_DEN = 1, 2
SC_ALIGN = 1024


def _sc_params():
    cp = pltpu.CompilerParams()
    if "needs_layout_passes" in pltpu.CompilerParams.__dataclass_fields__:
        cp = dataclasses.replace(cp, needs_layout_passes=False)
    return cp


def _gelu_tanh_via_exp(x):
    c = math.sqrt(2.0 / math.pi)
    z = c * (x + 0.044715 * (x * x * x))
    t = jnp.exp(-2.0 * jnp.abs(z))
    return 0.5 * x * (1.0 + jnp.sign(z) * (1.0 - t) / (1.0 + t))


def _peer_sc(e2, g, h2, tab_u, tab_v, off):
    n_sc = e2.shape[0]
    per_w = n_sc // SC_WORKERS
    mesh = plsc.VectorSubcoreMesh(core_axis_name="c", subcore_axis_name="s")
    ln = SC_LANES

    @functools.partial(
        pl.kernel, mesh=mesh,
        out_type=jax.ShapeDtypeStruct((n_sc, D_MODEL), jnp.float32),
        scratch_types=[pltpu.VMEM((SC_HALF,), jnp.int32), pltpu.VMEM((SC_HALF,), jnp.int32),
                       pltpu.VMEM((SC_HALF, ROW_WORDS), jnp.int32), pltpu.VMEM((SC_HALF, ROW_WORDS), jnp.int32),
                       pltpu.VMEM((D_MODEL,), jnp.float32), pltpu.VMEM((N_PICKS,), jnp.float32),
                       pltpu.VMEM((N_PICKS,), jnp.float32), pltpu.VMEM((D_MODEL,), jnp.float32),
                       pltpu.SemaphoreType.DMA, pltpu.SemaphoreType.DMA],
        compiler_params=_sc_params(),
    )
    def body(e_hbm, g_hbm, x_hbm, tu_hbm, tv_hbm, out_hbm,
             idx_a, idx_b, buf_a, buf_b, x_v, g_v, w_v, o_v, sem_a, sem_b):
        wid = lax.axis_index("s") * SC_CORES + lax.axis_index("c")
        lane = lax.iota(jnp.int32, ln)
        zero = jnp.zeros((ln,), jnp.float32)

        def u_half(buf, k0):
            @pl.loop(0, SC_HALF // ln)
            def _(grp):
                def chunk(j, accs):
                    xl = x_v[pl.ds(j * ln, ln)]
                    xh = x_v[pl.ds(ROW_WORDS + j * ln, ln)]
                    new = []
                    for kk in range(ln):
                        lo, hi = _unpack_row(buf[grp * ln + kk, pl.ds(j * ln, ln)])
                        new.append(accs[kk] + lo * xl + hi * xh)
                    return tuple(new)

                accs = lax.fori_loop(0, SC_CHUNKS, chunk, tuple(zero for _ in range(ln)))
                a = zero
                for kk in range(ln):
                    a = jnp.where(lane == kk, jnp.sum(accs[kk]), a)
                sl = pl.ds(k0 + grp * ln, ln)
                w_v[sl] = g_v[sl] * _gelu_tanh_via_exp(a)

        def v_half(buf, k0):
            @pl.loop(0, SC_CHUNKS // SC_ACC_CHUNKS)
            def _(jb):
                def pick(kq, accs):
                    wk = plsc.load_gather(w_v, [jnp.full((ln,), k0, jnp.int32) + kq])
                    new = []
                    for c in range(SC_ACC_CHUNKS):
                        lo, hi = _unpack_row(buf[kq, pl.ds((jb * SC_ACC_CHUNKS + c) * ln, ln)])
                        new.append(accs[2 * c] + wk * lo)
                        new.append(accs[2 * c + 1] + wk * hi)
                    return tuple(new)

                accs = lax.fori_loop(0, SC_HALF, pick, tuple(zero for _ in range(2 * SC_ACC_CHUNKS)))
                for c in range(SC_ACC_CHUNKS):
                    col = (jb * SC_ACC_CHUNKS + c) * ln
                    o_v[pl.ds(col, ln)] = o_v[pl.ds(col, ln)] + accs[2 * c]
                    o_v[pl.ds(ROW_WORDS + col, ln)] = o_v[pl.ds(ROW_WORDS + col, ln)] + accs[2 * c + 1]

        @pl.loop(0, per_w)
        def _(i):
            tl = wid * per_w + i
            tg = off + tl
            pltpu.sync_copy(e_hbm.at[tl, 0], idx_a)
            pltpu.sync_copy(e_hbm.at[tl, 1], idx_b)
            pltpu.sync_copy(x_hbm.at[tg], x_v)
            pltpu.sync_copy(g_hbm.at[tg], g_v)
            u_a = pltpu.async_copy(tu_hbm.at[idx_a], buf_a, sem_a)
            u_b = pltpu.async_copy(tu_hbm.at[idx_b], buf_b, sem_b)

            @pl.loop(0, D_MODEL // ln)
            def _(c):
                o_v[pl.ds(c * ln, ln)] = zero

            u_a.wait()
            u_half(buf_a, 0)
            v_a = pltpu.async_copy(tv_hbm.at[idx_a], buf_a, sem_a)
            u_b.wait()
            u_half(buf_b, SC_HALF)
            v_b = pltpu.async_copy(tv_hbm.at[idx_b], buf_b, sem_b)
            v_a.wait()
            v_half(buf_a, 0)
            v_b.wait()
            v_half(buf_b, SC_HALF)
            pltpu.sync_copy(o_v, out_hbm.at[tl])

    return body(e2, g, h2, tab_u, tab_v)


def _finish_kernel(y_in_ref, x_ref, gate_ref, p_ref, y_ref):
    del y_in_ref
    y_ref[...] = x_ref[...] + gate_ref[...] * p_ref[...]


def _peer_finish(y, x1, gate2, peer_tail, seq, n_tc):
    n = x1.shape[0]
    tt = TT_PEER
    first = n_tc // tt
    row = lambda i: (first + i, 0)
    return pl.pallas_call(
        _finish_kernel,
        grid=((n - n_tc) // tt,),
        in_specs=[pl.BlockSpec(memory_space=pl.ANY),
                  pl.BlockSpec((tt, D_MODEL), row),
                  pl.BlockSpec((None, 1, D_MODEL), lambda i: (((first + i) * tt) // seq, 0, 0)),
                  pl.BlockSpec((tt, D_MODEL), lambda i: (i, 0))],
        out_specs=pl.BlockSpec((tt, D_MODEL), row),
        out_shape=jax.ShapeDtypeStruct((n, D_MODEL), jnp.float32),
        input_output_aliases={0: 0},
        compiler_params=_cparams(("arbitrary",)),
        name="peer_finish",
    )(y, x1, gate2, peer_tail)


def _pad_heads(w, used, lead=0):
    r = w.shape[0]
    w3 = w.reshape(r, N_HEADS, used)
    out = jnp.zeros((r, N_HEADS, HEAD_SLOT), w.dtype).at[:, :, lead:lead + used].set(w3)
    return out.reshape(r, N_HEADS * HEAD_SLOT)


def _pack_table(t):
    tb = t.astype(jnp.bfloat16)
    lo = lax.bitcast_convert_type(tb[:, :ROW_WORDS], jnp.uint16).astype(jnp.uint32)
    hi = lax.bitcast_convert_type(tb[:, ROW_WORDS:], jnp.uint16).astype(jnp.uint32)
    words = lax.bitcast_convert_type(lo | (hi << 16), jnp.int32)
    return words.reshape(t.shape[0] * ROW_SUB, LANES)


def _prep_weights(g_norm1, w_in, conv_w, w_conv_out, g_q_lora, w_uq, g_kv_lora, w_ukv, g_qnorm,
                  g_knorm, w_attn_out, w_out, g_norm2, peer_wq, peer_k1, peer_k2, peer_u, peer_v):
    bf = jnp.bfloat16
    o1 = 3 * D_CONV
    o2 = o1 + Q_LORA
    o3 = o2 + KV_LORA
    o4 = o3 + QK_ROPE
    o5 = o4 + D_MODEL
    kr_slot = jnp.zeros((D_MODEL, HEAD_SLOT), w_in.dtype).at[:, QK_NOPE:QK_HEAD].set(w_in[:, o3:o4])
    w_in_r = jnp.concatenate([w_in[:, :o3], kr_slot, w_in[:, o4:]], axis=1).astype(bf)
    w_ukv3 = w_ukv.reshape(KV_LORA, N_HEADS, QK_NOPE + V_HEAD)
    w_uk = _pad_heads(w_ukv3[:, :, :QK_NOPE].reshape(KV_LORA, -1), QK_NOPE)
    w_uv = _pad_heads(w_ukv3[:, :, QK_NOPE:].reshape(KV_LORA, -1), V_HEAD)
    pad_gain = lambda g: jnp.zeros((1, HEAD_SLOT), jnp.float32).at[0, :QK_HEAD].set(g)
    w_ao = jnp.zeros((N_HEADS, HEAD_SLOT, D_MODEL), w_attn_out.dtype).at[:, :V_HEAD].set(
        w_attn_out.reshape(N_HEADS, V_HEAD, D_MODEL)).reshape(N_HEADS * HEAD_SLOT, D_MODEL)
    half = PEER_DK // 2
    kcat = jnp.zeros((PEER_HEADS, 2 * N_KEYS, PEER_DK), jnp.float32)
    kcat = kcat.at[:, :N_KEYS, :half].set(peer_k1).at[:, N_KEYS:, half:].set(peer_k2)
    tab_u = _pack_table(peer_u)
    tab_v = _pack_table(peer_v)
    return {
        "g_norm1": g_norm1.reshape(1, -1), "w_in": w_in_r, "conv_w": conv_w,
        "w_conv_out": w_conv_out.astype(bf), "g_q_lora": g_q_lora.reshape(1, -1),
        "w_uq": _pad_heads(w_uq, QK_HEAD).astype(bf), "g_kv_lora": g_kv_lora.reshape(1, -1),
        "w_uk": w_uk.astype(bf), "w_uv": w_uv.astype(bf),
        "g_qnorm": pad_gain(g_qnorm), "g_knorm": pad_gain(g_knorm),
        "w_attn_out": w_ao.astype(bf), "w_out": w_out.astype(bf), "g_norm2": g_norm2.reshape(1, -1),
        "peer_wq": peer_wq.astype(bf), "kcat": kcat.astype(bf),
        "tab_u": tab_u, "tab_v": tab_v,
        "tab_u_sc": tab_u.reshape(-1, ROW_WORDS), "tab_v_sc": tab_v.reshape(-1, ROW_WORDS),
    }


def _rope_tables(seq):
    pos = jnp.arange(seq, dtype=jnp.float32)
    inv = ROPE_THETA ** (-jnp.arange(0, QK_ROPE, 2, dtype=jnp.float32) / QK_ROPE)
    ang = pos[:, None] * inv[None, :]
    cos, sin = jnp.cos(ang), jnp.sin(ang)
    t1 = slice(QK_NOPE, QK_NOPE + HALF_ROPE)
    t2 = slice(QK_NOPE + HALF_ROPE, QK_HEAD)
    cos_t = jnp.ones((seq, HEAD_SLOT), jnp.float32).at[:, t1].set(cos).at[:, t2].set(cos)
    sin_a = jnp.zeros((seq, HEAD_SLOT), jnp.float32).at[:, t1].set(-sin)
    sin_b = jnp.zeros((seq, HEAD_SLOT), jnp.float32).at[:, t2].set(sin)
    return cos_t, sin_a, sin_b


def _encoder_layer(x, c, w_ada, b_ada, wts):
    bsz, seq, d = x.shape
    n = bsz * seq
    ada = _ada(c, w_ada, b_ada)[:, None, :]
    shift1, scale1, gate1, shift2, scale2, gate2 = jnp.split(ada, 6, axis=-1)
    q, k, v, a, sgb = _inproj(x, scale1, shift1, wts, _rope_tables(seq))
    o = _attention(q, k, v)
    x1, h2, e, g = _mix(x, a, sgb, o, gate1, scale2, shift2, wts)
    n_sc = (n * SC_SHARE_NUM // SC_SHARE_DEN) // SC_ALIGN * SC_ALIGN
    n_tc = n - n_sc
    e = e.reshape(n, N_PICKS)
    g = g.reshape(n, N_PICKS)
    h2 = h2.reshape(n, d)
    x1 = x1.reshape(n, d)
    if n_sc:
        ids = lax.shift_right_logical(e[n_tc:], ROW_SUB.bit_length() - 1)
        peer_tail = _peer_sc(ids.reshape(n_sc, 2, SC_HALF), g, h2,
                             wts["tab_u_sc"], wts["tab_v_sc"], n_tc)
    e = e.reshape(n * N_PICKS)
    w = _peer_u(e, h2.reshape(n * SUBLANES, LANES), g, wts["tab_u"], n_tc)
    y = _peer_v(e, w, x1, gate2, wts["tab_v"], seq, n_tc)
    if n_sc:
        y = _peer_finish(y, x1, gate2, peer_tail, seq, n_tc)
    return y.reshape(bsz, seq, d)


def kernel(x_prompt, x_sample, c_prompt, c_sample, w_ada, b_ada, g_norm1, w_in, conv_w, w_conv_out, g_q_lora, w_uq, g_kv_lora, w_ukv, g_qnorm, g_knorm, w_attn_out, w_out, g_norm2, peer_wq, peer_k1, peer_k2, peer_u, peer_v):
    wts = _prep_weights(g_norm1[0], w_in[0], conv_w[0], w_conv_out[0], g_q_lora[0], w_uq[0],
                        g_kv_lora[0], w_ukv[0], g_qnorm[0], g_knorm[0], w_attn_out[0], w_out[0],
                        g_norm2[0], peer_wq[0], peer_k1[0], peer_k2[0], peer_u[0], peer_v[0])
    return (_encoder_layer(x_prompt, c_prompt, w_ada[0], b_ada[0], wts),
            _encoder_layer(x_sample, c_sample, w_ada[0], b_ada[0], wts))
```

```python
import dataclasses
import functools
import math

import jax
import jax.numpy as jnp
import numpy as np
from jax import lax
from jax.experimental import pallas as pl
from jax.experimental.pallas import tpu as pltpu
from jax.experimental.pallas import tpu_sc as plsc

D_MODEL = 1024
D_CONV = 512
N_HEADS = 8
QK_NOPE = 64
QK_ROPE = 32
V_HEAD = 64
Q_LORA = 256
KV_LORA = 128
QK_HEAD = QK_NOPE + QK_ROPE
ROPE_THETA = 10000.0
PEER_HEADS = 8
N_KEYS = 128
PEER_DK = 128
PEER_TOPK = 16
EPS = 1e-6

LANES = 128
SUBLANES = 8
HEAD_SLOT = LANES
HALF_ROPE = QK_ROPE // 2
N_PICKS = PEER_HEADS * PEER_TOPK
ROW_WORDS = D_MODEL // 2
ROW_SUB = ROW_WORDS // LANES
VMEM_LIMIT = 56 * 1024 * 1024

C_BG, C_CG, C_HC = 0, 512, 1024
C_CQ = 1536
C_CKV = C_CQ + Q_LORA
C_KR = C_CKV + KV_LORA
C_GA = C_KR + HEAD_SLOT
C_GB = C_GA + D_MODEL
IN_COLS_R = C_GB + D_MODEL

TS_IN = 512
TQ = 512
TK = 2048
KV_UNROLL = 2
TS_MIX = 256
TT_PEER = 128
TOK_UNROLL_U = 2
TOK_UNROLL_V = 4
RED_TOKENS = 8
BF16_ROWS = 16
IDX_GROUP = 8

_NEG_INF = float("-inf")


def _cparams(sem):
    return pltpu.CompilerParams(dimension_semantics=sem, vmem_limit_bytes=VMEM_LIMIT)


def _ada_kernel(c_ref, w_ref, b_ref, o_ref):
    o_ref[...] = jnp.dot(c_ref[...], w_ref[...], preferred_element_type=jnp.float32,
                         precision=lax.Precision.HIGHEST) + b_ref[...]


def _ada(c, w_ada, b_ada):
    bsz = c.shape[0]
    rows = -(-bsz // SUBLANES) * SUBLANES
    c_p = jnp.zeros((rows, D_MODEL), jnp.float32).at[:bsz].set(c)
    out = pl.pallas_call(
        _ada_kernel,
        grid=(6,),
        in_specs=[pl.BlockSpec((rows, D_MODEL), lambda j: (0, 0)),
                  pl.BlockSpec((D_MODEL, D_MODEL), lambda j: (0, j)),
                  pl.BlockSpec((1, D_MODEL), lambda j: (0, j))],
        out_specs=pl.BlockSpec((rows, D_MODEL), lambda j: (0, j)),
        out_shape=jax.ShapeDtypeStruct((rows, 6 * D_MODEL), jnp.float32),
        compiler_params=_cparams(("arbitrary",)),
        name="ada",
    )(c_p, w_ada, b_ada.reshape(1, -1))
    return out[:bsz]


def _adaln(x, g, scale, shift):
    ms = jnp.mean(x * x, axis=-1, keepdims=True)
    return (x * lax.rsqrt(ms + EPS) * g) * (1.0 + scale) + shift


def _rms_rows(x, g, n):
    ms = jnp.sum(x * x, axis=-1, keepdims=True) * (1.0 / n)
    return x * lax.rsqrt(ms + EPS) * g


def _rope(x, cos, sin_a, sin_b):
    return (x * cos + pltpu.roll(x, LANES - HALF_ROPE, axis=1) * sin_a
            + pltpu.roll(x, HALF_ROPE, axis=1) * sin_b)


def _inproj_kernel(x_ref, xp_ref, xn_ref, scale_ref, shift_ref, g1_ref, w_in_ref, conv_w_ref,
                   w_co_ref, gql_ref, w_uq_ref, gkvl_ref, w_uk_ref, w_uv_ref, gqn_ref, gkn_ref,
                   cos_ref, sa_ref, sb_ref, *rest):
    q_ref, k_ref, v_ref, a_ref, sgb_ref = rest[-5:]
    i = pl.program_id(1)
    n_i = pl.num_programs(1)
    g1 = g1_ref[...]
    scale = scale_ref[...]
    shift = shift_ref[...]
    ts = x_ref.shape[0]

    h = _adaln(x_ref[...], g1, scale, shift).astype(jnp.bfloat16)

    def proj(lo, width):
        return jnp.dot(h, w_in_ref[:, lo:lo + width], preferred_element_type=jnp.float32)

    z = proj(C_CG, D_CONV) * proj(C_HC, D_CONV)

    def halo_z(xh_ref):
        hh = _adaln(xh_ref[...], g1, scale, shift).astype(jnp.bfloat16)
        zc = jnp.dot(hh, w_in_ref[:, C_CG:C_CG + 2 * D_CONV], preferred_element_type=jnp.float32)
        return zc[:, :D_CONV] * zc[:, D_CONV:]

    z_prev = halo_z(xp_ref)[SUBLANES - 1:SUBLANES, :]
    z_next = halo_z(xn_ref)[0:1, :]
    z_prev = jnp.where(i == 0, 0.0, z_prev)
    z_next = jnp.where(i == n_i - 1, 0.0, z_next)
    row = lax.broadcasted_iota(jnp.int32, (ts, D_CONV), 0)
    z_up = jnp.where(row == 0, z_prev, pltpu.roll(z, 1, axis=0))
    z_dn = jnp.where(row == ts - 1, z_next, pltpu.roll(z, ts - 1, axis=0))
    cw = conv_w_ref[...]
    y = z_up * cw[0:1, :] + z * cw[1:2, :] + z_dn * cw[2:3, :]
    out_a = jnp.dot((proj(C_BG, D_CONV) * y).astype(jnp.bfloat16), w_co_ref[...],
                    preferred_element_type=jnp.float32)
    a_ref[...] = jax.nn.sigmoid(proj(C_GA, D_MODEL)) * out_a
    sgb_ref[...] = jax.nn.sigmoid(proj(C_GB, D_MODEL))

    cos = cos_ref[...]
    sin_a = sa_ref[...]
    sin_b = sb_ref[...]
    cq = _rms_rows(proj(C_CQ, Q_LORA), gql_ref[...], Q_LORA).astype(jnp.bfloat16)
    qf = jnp.dot(cq, w_uq_ref[...], preferred_element_type=jnp.float32)
    ckv = _rms_rows(proj(C_CKV, KV_LORA), gkvl_ref[...], KV_LORA).astype(jnp.bfloat16)
    kf = jnp.dot(ckv, w_uk_ref[...], preferred_element_type=jnp.float32)
    lane = lax.broadcasted_iota(jnp.int32, (1, N_HEADS * HEAD_SLOT), 1)
    ones_lane = (jnp.bitwise_and(lane, HEAD_SLOT - 1) == V_HEAD).astype(jnp.float32)
    v_ref[...] = (jnp.dot(ckv, w_uv_ref[...], preferred_element_type=jnp.float32)
                  + ones_lane).astype(jnp.bfloat16)
    kr = proj(C_KR, HEAD_SLOT)
    gqn = gqn_ref[...]
    gkn = gkn_ref[...]
    q_scale = QK_HEAD ** -0.5 * math.log2(math.e)
    for hd in range(N_HEADS):
        sl = slice(hd * HEAD_SLOT, (hd + 1) * HEAD_SLOT)
        qh = _rope(_rms_rows(qf[:, sl], gqn, QK_HEAD), cos, sin_a, sin_b)
        q_ref[:, sl] = (qh * q_scale).astype(jnp.bfloat16)
        kh = _rope(_rms_rows(kf[:, sl] + kr, gkn, QK_HEAD), cos, sin_a, sin_b)
        k_ref[:, sl] = kh.astype(jnp.bfloat16)


def _inproj(x, scale1, shift1, wts, rope, after):
    bsz, seq, _ = x.shape
    ts = min(TS_IN, seq)
    n_i = seq // ts
    nb8 = seq // SUBLANES
    per8 = ts // SUBLANES
    tile = lambda b, i: (b, i, 0)
    per_b = lambda b, i: (b, 0, 0)
    full2 = lambda b, i: (0, 0)
    hw = N_HEADS * HEAD_SLOT
    in_specs = [
        pl.BlockSpec((None, ts, D_MODEL), tile),
        pl.BlockSpec((None, SUBLANES, D_MODEL), lambda b, i: (b, jnp.maximum(i * per8 - 1, 0), 0)),
        pl.BlockSpec((None, SUBLANES, D_MODEL), lambda b, i: (b, jnp.minimum((i + 1) * per8, nb8 - 1), 0)),
        pl.BlockSpec((None, 1, D_MODEL), per_b),
        pl.BlockSpec((None, 1, D_MODEL), per_b),
        pl.BlockSpec((1, D_MODEL), full2),
        pl.BlockSpec((D_MODEL, IN_COLS_R), full2),
        pl.BlockSpec((3, D_CONV), full2),
        pl.BlockSpec((D_CONV, D_MODEL), full2),
        pl.BlockSpec((1, Q_LORA), full2),
        pl.BlockSpec((Q_LORA, hw), full2),
        pl.BlockSpec((1, KV_LORA), full2),
        pl.BlockSpec((KV_LORA, hw), full2),
        pl.BlockSpec((KV_LORA, hw), full2),
        pl.BlockSpec((1, HEAD_SLOT), full2),
        pl.BlockSpec((1, HEAD_SLOT), full2),
        pl.BlockSpec((ts, HEAD_SLOT), lambda b, i: (i, 0)),
        pl.BlockSpec((ts, HEAD_SLOT), lambda b, i: (i, 0)),
        pl.BlockSpec((ts, HEAD_SLOT), lambda b, i: (i, 0)),
    ] + [pl.BlockSpec(memory_space=pl.ANY)] * len(after)
    out_specs = [pl.BlockSpec((None, ts, hw), tile)] * 3 + [pl.BlockSpec((None, ts, D_MODEL), tile)] * 2
    out_shape = ([jax.ShapeDtypeStruct((bsz, seq, hw), jnp.bfloat16)] * 3
                 + [jax.ShapeDtypeStruct((bsz, seq, D_MODEL), jnp.float32)] * 2)
    return pl.pallas_call(
        _inproj_kernel,
        grid=(bsz, n_i),
        in_specs=in_specs,
        out_specs=out_specs,
        out_shape=out_shape,
        compiler_params=_cparams(("parallel", "parallel")),
        name="inproj",
    )(x, x, x, scale1, shift1, wts["g_norm1"], wts["w_in"], wts["conv_w"], wts["w_conv_out"],
      wts["g_q_lora"], wts["w_uq"], wts["g_kv_lora"], wts["w_uk"], wts["w_uv"],
      wts["g_qnorm"], wts["g_knorm"], rope[0], rope[1], rope[2], *after)


def _attn_kernel(q_ref, k_ref, v_ref, o_ref):
    q = q_ref[...]
    tq = q.shape[0]
    seq = k_ref.shape[0]
    tk = min(TK, seq)

    n_chunks = seq // tk
    unroll = KV_UNROLL if n_chunks % KV_UNROLL == 0 else 1

    def chunk(off, carry):
        m, acc = carry
        kc = k_ref[pl.ds(off, tk), :]
        vc = v_ref[pl.ds(off, tk), :]
        s = lax.dot_general(q, kc, (((1,), (1,)), ((), ())), preferred_element_type=jnp.float32)
        m_new = jnp.maximum(m, jnp.max(s, axis=-1, keepdims=True))
        p = jnp.exp2(s - m_new).astype(jnp.bfloat16)
        alpha = jnp.exp2(m - m_new)
        acc = alpha * acc + jnp.dot(p, vc, preferred_element_type=jnp.float32)
        return m_new, acc

    def body(j, carry):
        for u in range(unroll):
            carry = chunk(pl.multiple_of((j * unroll + u) * tk, tk), carry)
        return carry

    m0 = jnp.full((tq, 1), _NEG_INF, jnp.float32)
    acc0 = jnp.zeros((tq, HEAD_SLOT), jnp.float32)
    _, acc = lax.fori_loop(0, n_chunks // unroll, body, (m0, acc0))
    o_ref[...] = (acc / acc[:, V_HEAD:V_HEAD + 1]).astype(jnp.bfloat16)


def _attention(q, k, v):
    bsz, seq, hw = q.shape
    tq = min(TQ, seq)
    return pl.pallas_call(
        _attn_kernel,
        grid=(bsz, N_HEADS, seq // tq),
        in_specs=[pl.BlockSpec((None, tq, HEAD_SLOT), lambda b, h, i: (b, i, h)),
                  pl.BlockSpec((None, seq, HEAD_SLOT), lambda b, h, i: (b, 0, h)),
                  pl.BlockSpec((None, seq, HEAD_SLOT), lambda b, h, i: (b, 0, h))],
        out_specs=pl.BlockSpec((None, tq, HEAD_SLOT), lambda b, h, i: (b, i, h)),
        out_shape=jax.ShapeDtypeStruct((bsz, seq, hw), jnp.bfloat16),
        compiler_params=_cparams(("parallel", "parallel", "arbitrary")),
        name="attn",
    )(q, k, v)


def _topk_rows(s, iota):
    vals, idxs = [], []
    big = jnp.float32(1e9)
    for _ in range(PEER_TOPK):
        m = jnp.max(s, axis=0, keepdims=True)
        am = jnp.min(jnp.where(s == m, iota, big), axis=0, keepdims=True)
        vals.append(m)
        idxs.append(am)
        s = jnp.where(iota == am, _NEG_INF, s)
    return jnp.concatenate(vals, axis=0), jnp.concatenate(idxs, axis=0)


def _take_rows(tab, idx):
    out = jnp.zeros_like(tab)
    for a in range(tab.shape[0]):
        out = out + jnp.where(idx == jnp.float32(a), tab[a:a + 1, :], 0.0)
    return out


_CAND = [(a, b) for a in range(PEER_TOPK) for b in range(PEER_TOPK) if (a + 1) * (b + 1) <= PEER_TOPK]
N_CAND = -(-len(_CAND) // SUBLANES) * SUBLANES
_PAD_LABEL = float(PEER_TOPK * PEER_TOPK)


def _candidate_tables():
    sel = np.zeros((2, N_CAND, PEER_TOPK), np.float32)
    lab = np.full((N_CAND, 1), _PAD_LABEL, np.float32)
    for r, (a, b) in enumerate(_CAND):
        sel[0, r, a] = 1.0
        sel[1, r, b] = 1.0
        lab[r, 0] = a * PEER_TOPK + b
    return sel, lab


def _mix_kernel(x_ref, a_ref, sgb_ref, o_ref, gate1_ref, scale2_ref, shift2_ref, g2_ref,
                w_ao_ref, w_out_ref, w_pq_ref, kcat_ref, sel_ref, lab_ref,
                x1_ref, h2_ref, e_ref, g_ref):
    ts = x_ref.shape[0]
    out_b = jnp.dot(o_ref[...], w_ao_ref[...], preferred_element_type=jnp.float32)
    merged = (a_ref[...] + sgb_ref[...] * out_b).astype(jnp.bfloat16)
    x1 = x_ref[...] + gate1_ref[...] * jnp.dot(merged, w_out_ref[...],
                                              preferred_element_type=jnp.float32)
    x1_ref[...] = x1
    h2 = _adaln(x1, g2_ref[...], scale2_ref[...], shift2_ref[...])
    h2_ref[...] = h2
    qp = jnp.dot(h2.astype(jnp.bfloat16), w_pq_ref[...],
                 preferred_element_type=jnp.float32).astype(jnp.bfloat16)

    iota_k = lax.broadcasted_iota(jnp.int32, (N_KEYS, ts), 0).astype(jnp.float32)
    lab = lab_ref[...]
    valid = lab < _PAD_LABEL
    sel_a = sel_ref[0]
    sel_b = sel_ref[1]
    exact = dict(preferred_element_type=jnp.float32, precision=lax.Precision.HIGHEST)
    e_rows, g_rows = [], []
    for hd in range(PEER_HEADS):
        qh = qp[:, hd * PEER_DK:(hd + 1) * PEER_DK]
        st = lax.dot_general(kcat_ref[hd], qh, (((1,), (1,)), ((), ())),
                             preferred_element_type=jnp.float32)
        v1, i1 = _topk_rows(st[:N_KEYS], iota_k)
        v2, i2 = _topk_rows(st[N_KEYS:], iota_k)
        comb = jnp.dot(sel_a, v1, **exact) + jnp.dot(sel_b, v2, **exact)
        sc, pos = _topk_rows(jnp.where(valid, comb, _NEG_INF), lab)
        ia = jnp.floor(pos * (1.0 / PEER_TOPK))
        ib = pos - ia * PEER_TOPK
        e_rows.append(_take_rows(i1, ia) * N_KEYS + _take_rows(i2, ib))
        p = jnp.exp(sc - sc[0:1, :])
        g_rows.append(p / jnp.sum(p, axis=0, keepdims=True))
    e_all = jnp.concatenate(e_rows, axis=0)
    g_all = jnp.concatenate(g_rows, axis=0)
    for c in range(ts // LANES):
        cs = slice(c * LANES, (c + 1) * LANES)
        e_ref[cs, :] = (e_all[:, cs].T * ROW_SUB).astype(jnp.int32)
        g_ref[cs, :] = g_all[:, cs].T


def _mix(x, a, sgb, o, gate1, scale2, shift2, wts):
    bsz, seq, _ = x.shape
    ts = min(TS_MIX, seq)
    tile = lambda b, i: (b, i, 0)
    per_b = lambda b, i: (b, 0, 0)
    full2 = lambda b, i: (0, 0)
    hw = N_HEADS * HEAD_SLOT
    big = pl.BlockSpec((None, ts, D_MODEL), tile)
    vec = pl.BlockSpec((None, 1, D_MODEL), per_b)
    picks = pl.BlockSpec((None, ts, N_PICKS), tile)
    sel, lab = _candidate_tables()
    return pl.pallas_call(
        _mix_kernel,
        grid=(bsz, seq // ts),
        in_specs=[big, big, big, pl.BlockSpec((None, ts, hw), tile), vec, vec, vec,
                  pl.BlockSpec((1, D_MODEL), full2),
                  pl.BlockSpec((hw, D_MODEL), full2),
                  pl.BlockSpec((D_MODEL, D_MODEL), full2),
                  pl.BlockSpec((D_MODEL, PEER_HEADS * PEER_DK), full2),
                  pl.BlockSpec((PEER_HEADS, 2 * N_KEYS, PEER_DK), lambda b, i: (0, 0, 0)),
                  pl.BlockSpec((2, N_CAND, PEER_TOPK), lambda b, i: (0, 0, 0)),
                  pl.BlockSpec((N_CAND, ts), full2)],
        out_specs=[big, big, picks, picks],
        out_shape=[jax.ShapeDtypeStruct((bsz, seq, D_MODEL), jnp.float32),
                   jax.ShapeDtypeStruct((bsz, seq, D_MODEL), jnp.float32),
                   jax.ShapeDtypeStruct((bsz, seq, N_PICKS), jnp.int32),
                   jax.ShapeDtypeStruct((bsz, seq, N_PICKS), jnp.float32)],
        compiler_params=_cparams(("parallel", "parallel")),
        name="mix",
    )(x, a, sgb, o, gate1, scale2, shift2, wts["g_norm2"], wts["w_attn_out"], wts["w_out"],
      wts["peer_wq"], wts["kcat"], jnp.asarray(sel), jnp.broadcast_to(jnp.asarray(lab), (N_CAND, ts)))


def _unpack_row(w):
    lo = lax.bitcast_convert_type(lax.shift_left(w, 16), jnp.float32)
    hi = lax.bitcast_convert_type(jnp.bitwise_and(w, jnp.int32(-65536)), jnp.float32)
    return lo, hi


def _gelu_tanh(x):
    c = math.sqrt(2.0 / math.pi)
    return 0.5 * x * (1.0 + jnp.tanh(c * (x + 0.044715 * (x * x * x))))


def _gather_rows(e_ref, t, tab_ref, pb):
    for j in range(N_PICKS // IDX_GROUP):
        row = e_ref.at[pl.ds(pl.multiple_of(t * N_PICKS + j * IDX_GROUP, IDX_GROUP), IDX_GROUP)]
        for c in range(IDX_GROUP):
            k = j * IDX_GROUP + c
            r = pl.multiple_of(row[c], ROW_SUB)
            pb[k * ROW_SUB:(k + 1) * ROW_SUB, :] = tab_ref[pl.ds(r, ROW_SUB), :]


def _pick_block(pb, s):
    return pb[pl.ds(s, N_PICKS, stride=ROW_SUB), :]


def _peer_u_kernel(e_ref, x_ref, g_ref, tab_ref, w_ref, pbuf0, pbuf1, rbuf, abuf):
    tt = g_ref.shape[0]
    pbufs = (pbuf0, pbuf1)

    def tok(i, carry):
        for u in range(TOK_UNROLL_U):
            t = i * TOK_UNROLL_U + u
            pb = pbufs[u % 2]
            x_blk = x_ref[pl.ds(pl.multiple_of(t * SUBLANES, SUBLANES), SUBLANES), :]
            _gather_rows(e_ref, t, tab_ref, pb)
            acc = None
            for s in range(ROW_SUB):
                lo, hi = _unpack_row(_pick_block(pb, s))
                term = lo * x_blk[s:s + 1, :] + hi * x_blk[ROW_SUB + s:ROW_SUB + s + 1, :]
                acc = term if acc is None else acc + term
            rbuf[pl.ds(pl.multiple_of(t * N_PICKS, N_PICKS), N_PICKS), :] = acc
        return carry

    lax.fori_loop(0, tt // TOK_UNROLL_U, tok, 0)

    ones2 = jnp.ones((2 * LANES, LANES), jnp.bfloat16)
    lane = lax.broadcasted_iota(jnp.int32, (N_PICKS, tt), 1)
    abuf[...] = jnp.zeros_like(abuf)
    rows = RED_TOKENS * N_PICKS

    def red(i, carry):
        rb = rbuf[pl.ds(pl.multiple_of(i * rows, rows), rows), :]
        r_hi = rb.astype(jnp.bfloat16)
        r_lo = (rb - r_hi.astype(jnp.float32)).astype(jnp.bfloat16)
        res = jnp.dot(jnp.concatenate([r_hi, r_lo], axis=1), ones2, preferred_element_type=jnp.float32)
        at = abuf[...]
        for u in range(RED_TOKENS):
            at = jnp.where(lane == i * RED_TOKENS + u, res[u * N_PICKS:(u + 1) * N_PICKS, :], at)
        abuf[...] = at
        return carry

    lax.fori_loop(0, tt // RED_TOKENS, red, 0)
    w_ref[...] = g_ref[...] * _gelu_tanh(abuf[...].T)


def _peer_u(e, h2_rows, g, tab, n):
    tt = TT_PEER
    return pl.pallas_call(
        _peer_u_kernel,
        grid=(n // tt,),
        in_specs=[pl.BlockSpec((tt * N_PICKS,), lambda i: (i,), memory_space=pltpu.SMEM),
                  pl.BlockSpec((tt * SUBLANES, LANES), lambda i: (i, 0)),
                  pl.BlockSpec((tt, N_PICKS), lambda i: (i, 0)),
                  pl.BlockSpec(memory_space=pltpu.VMEM)],
        out_specs=pl.BlockSpec((tt, N_PICKS), lambda i: (i, 0)),
        out_shape=jax.ShapeDtypeStruct((n, N_PICKS), jnp.float32),
        scratch_shapes=[pltpu.VMEM((N_PICKS * ROW_SUB, LANES), jnp.int32),
                        pltpu.VMEM((N_PICKS * ROW_SUB, LANES), jnp.int32),
                        pltpu.VMEM((tt * N_PICKS, LANES), jnp.float32),
                        pltpu.VMEM((N_PICKS, tt), jnp.float32)],
        compiler_params=_cparams(("arbitrary",)),
        name="peer_u",
    )(e, h2_rows, g, tab)


def _peer_v_kernel(e_ref, w_ref, x_ref, gate_ref, tab_ref, y_ref, done_ref,
                   pbuf0, pbuf1, whi_ref, wlo_ref, acc_ref):
    tt = x_ref.shape[0]
    pbufs = (pbuf0, pbuf1)
    w = w_ref[...]
    w_hi = w.astype(jnp.bfloat16)
    whi_ref[...] = w_hi
    wlo_ref[...] = (w - w_hi.astype(jnp.float32)).astype(jnp.bfloat16)
    sub = lax.broadcasted_iota(jnp.int32, (BF16_ROWS, D_MODEL), 0)
    acc_ref[...] = jnp.zeros_like(acc_ref)

    def process(pb, t):
        b0 = pl.multiple_of(lax.shift_left(lax.shift_right_logical(t, 4), 4), BF16_ROWS)
        lhs = jnp.concatenate([whi_ref[pl.ds(b0, BF16_ROWS), :], wlo_ref[pl.ds(b0, BF16_ROWS), :]], axis=0)
        los, his = [], []
        for s in range(ROW_SUB):
            lo, hi = _unpack_row(_pick_block(pb, s))
            los.append(lo.astype(jnp.bfloat16))
            his.append(hi.astype(jnp.bfloat16))
        rhs = jnp.concatenate(los + his, axis=1)
        res = jnp.dot(lhs, rhs, preferred_element_type=jnp.float32)
        r = res[:BF16_ROWS, :] + res[BF16_ROWS:, :]
        mask = sub == jnp.bitwise_and(t, BF16_ROWS - 1)
        acc_ref[pl.ds(b0, BF16_ROWS), :] = jnp.where(mask, r, acc_ref[pl.ds(b0, BF16_ROWS), :])

    _gather_rows(e_ref, 0, tab_ref, pbuf0)

    def step(i, carry):
        t0 = i * TOK_UNROLL_V
        for u in range(TOK_UNROLL_V):
            process(pbufs[u % 2], t0 + u)
            _gather_rows(e_ref, jnp.minimum(t0 + u + 1, tt - 1), tab_ref, pbufs[(u + 1) % 2])
        return carry

    lax.fori_loop(0, tt // TOK_UNROLL_V, step, 0)
    y_ref[...] = x_ref[...] + gate_ref[...] * acc_ref[...]
    done_ref[...] = jnp.zeros_like(done_ref)


def _peer_v(e, w, x1, gate2, tab, seq, n_tc):
    n = x1.shape[0]
    tt = TT_PEER
    return pl.pallas_call(
        _peer_v_kernel,
        grid=(n_tc // tt,),
        in_specs=[pl.BlockSpec((tt * N_PICKS,), lambda i: (i,), memory_space=pltpu.SMEM),
                  pl.BlockSpec((tt, N_PICKS), lambda i: (i, 0)),
                  pl.BlockSpec((tt, D_MODEL), lambda i: (i, 0)),
                  pl.BlockSpec((None, 1, D_MODEL), lambda i: ((i * tt) // seq, 0, 0)),
                  pl.BlockSpec(memory_space=pltpu.VMEM)],
        out_specs=[pl.BlockSpec((tt, D_MODEL), lambda i: (i, 0)),
                   pl.BlockSpec((SUBLANES, LANES), lambda i: (0, 0))],
        out_shape=[jax.ShapeDtypeStruct((n, D_MODEL), jnp.float32),
                   jax.ShapeDtypeStruct((SUBLANES, LANES), jnp.float32)],
        scratch_shapes=[pltpu.VMEM((N_PICKS * ROW_SUB, LANES), jnp.int32),
                        pltpu.VMEM((N_PICKS * ROW_SUB, LANES), jnp.int32),
                        pltpu.VMEM((tt, N_PICKS), jnp.bfloat16),
                        pltpu.VMEM((tt, N_PICKS), jnp.bfloat16),
                        pltpu.VMEM((tt, D_MODEL), jnp.float32)],
        compiler_params=_cparams(("arbitrary",)),
        name="peer_v",
    )(e, w, x1, gate2, tab)


SC_CORES = 2
SC_SUBCORES = 16
SC_LANES = 16
SC_WORKERS = SC_CORES * SC_SUBCORES
SC_HALF = N_PICKS // 2
SC_CHUNKS = ROW_WORDS // SC_LANES
SC_ACC_CHUNKS = 8
SC_SHARE_FIRST = (1, 1)
SC_SHARE_SECOND = (0, 1)
SC_ALIGN = 1024


def _sc_params():
    cp = pltpu.CompilerParams()
    if "needs_layout_passes" in pltpu.CompilerParams.__dataclass_fields__:
        cp = dataclasses.replace(cp, needs_layout_passes=False)
    return cp


def _gelu_tanh_via_exp(x):
    c = math.sqrt(2.0 / math.pi)
    z = c * (x + 0.044715 * (x * x * x))
    t = jnp.exp(-2.0 * jnp.abs(z))
    return 0.5 * x * (1.0 + jnp.sign(z) * (1.0 - t) / (1.0 + t))


def _peer_sc(e2, g, h2, tab_u, tab_v, off):
    n_sc = e2.shape[0]
    per_w = n_sc // SC_WORKERS
    mesh = plsc.VectorSubcoreMesh(core_axis_name="c", subcore_axis_name="s")
    ln = SC_LANES

    @functools.partial(
        pl.kernel, mesh=mesh,
        out_type=jax.ShapeDtypeStruct((n_sc, D_MODEL), jnp.float32),
        scratch_types=[pltpu.VMEM((SC_HALF,), jnp.int32), pltpu.VMEM((SC_HALF,), jnp.int32),
                       pltpu.VMEM((SC_HALF, ROW_WORDS), jnp.int32), pltpu.VMEM((SC_HALF, ROW_WORDS), jnp.int32),
                       pltpu.VMEM((D_MODEL,), jnp.float32), pltpu.VMEM((N_PICKS,), jnp.float32),
                       pltpu.VMEM((N_PICKS,), jnp.float32), pltpu.VMEM((D_MODEL,), jnp.float32),
                       pltpu.SemaphoreType.DMA, pltpu.SemaphoreType.DMA],
        compiler_params=_sc_params(),
        cost_estimate=pl.CostEstimate(
            flops=4 * n_sc * N_PICKS * D_MODEL,
            transcendentals=n_sc * N_PICKS,
            bytes_accessed=n_sc * (2 * N_PICKS * ROW_WORDS * 4 + 3 * D_MODEL * 4 + 3 * N_PICKS * 4)),
    )
    def body(e_hbm, g_hbm, x_hbm, tu_hbm, tv_hbm, out_hbm,
             idx_a, idx_b, buf_a, buf_b, x_v, g_v, w_v, o_v, sem_a, sem_b):
        wid = lax.axis_index("s") * SC_CORES + lax.axis_index("c")
        lane = lax.iota(jnp.int32, ln)
        zero = jnp.zeros((ln,), jnp.float32)

        def u_half(buf, k0):
            @pl.loop(0, SC_HALF // ln)
            def _(grp):
                def chunk(j, accs):
                    xl = x_v[pl.ds(j * ln, ln)]
                    xh = x_v[pl.ds(ROW_WORDS + j * ln, ln)]
                    new = []
                    for kk in range(ln):
                        lo, hi = _unpack_row(buf[grp * ln + kk, pl.ds(j * ln, ln)])
                        new.append(accs[kk] + lo * xl + hi * xh)
                    return tuple(new)

                accs = lax.fori_loop(0, SC_CHUNKS, chunk, tuple(zero for _ in range(ln)))
                a = zero
                for kk in range(ln):
                    a = jnp.where(lane == kk, jnp.sum(accs[kk]), a)
                sl = pl.ds(k0 + grp * ln, ln)
                w_v[sl] = g_v[sl] * _gelu_tanh_via_exp(a)

        def v_half(buf, k0):
            @pl.loop(0, SC_CHUNKS // SC_ACC_CHUNKS)
            def _(jb):
                def pick(kq, accs):
                    wk = plsc.load_gather(w_v, [jnp.full((ln,), k0, jnp.int32) + kq])
                    new = []
                    for c in range(SC_ACC_CHUNKS):
                        lo, hi = _unpack_row(buf[kq, pl.ds((jb * SC_ACC_CHUNKS + c) * ln, ln)])
                        new.append(accs[2 * c] + wk * lo)
                        new.append(accs[2 * c + 1] + wk * hi)
                    return tuple(new)

                accs = lax.fori_loop(0, SC_HALF, pick, tuple(zero for _ in range(2 * SC_ACC_CHUNKS)))
                for c in range(SC_ACC_CHUNKS):
                    col = (jb * SC_ACC_CHUNKS + c) * ln
                    o_v[pl.ds(col, ln)] = o_v[pl.ds(col, ln)] + accs[2 * c]
                    o_v[pl.ds(ROW_WORDS + col, ln)] = o_v[pl.ds(ROW_WORDS + col, ln)] + accs[2 * c + 1]

        @pl.loop(0, per_w)
        def _(i):
            tl = wid * per_w + i
            tg = off + tl
            pltpu.sync_copy(e_hbm.at[tl, 0], idx_a)
            pltpu.sync_copy(e_hbm.at[tl, 1], idx_b)
            pltpu.sync_copy(x_hbm.at[tg], x_v)
            pltpu.sync_copy(g_hbm.at[tg], g_v)
            u_a = pltpu.async_copy(tu_hbm.at[idx_a], buf_a, sem_a)
            u_b = pltpu.async_copy(tu_hbm.at[idx_b], buf_b, sem_b)

            @pl.loop(0, D_MODEL // ln)
            def _(c):
                o_v[pl.ds(c * ln, ln)] = zero

            u_a.wait()
            u_half(buf_a, 0)
            v_a = pltpu.async_copy(tv_hbm.at[idx_a], buf_a, sem_a)
            u_b.wait()
            u_half(buf_b, SC_HALF)
            v_b = pltpu.async_copy(tv_hbm.at[idx_b], buf_b, sem_b)
            v_a.wait()
            v_half(buf_a, 0)
            v_b.wait()
            v_half(buf_b, SC_HALF)
            pltpu.sync_copy(o_v, out_hbm.at[tl])

    return body(e2, g, h2, tab_u, tab_v)


def _finish_kernel(*refs):
    x_ref, gate_ref, p_ref, y_ref = refs[-4:]
    y_ref[...] = x_ref[...] + gate_ref[...] * p_ref[...]


def _peer_finish(y, x1, gate2, peer_tail, seq, n_tc):
    n = x1.shape[0]
    tt = TT_PEER
    first = n_tc // tt
    row = lambda i: (first + i, 0)
    in_specs = [pl.BlockSpec((tt, D_MODEL), row),
                pl.BlockSpec((None, 1, D_MODEL), lambda i: (((first + i) * tt) // seq, 0, 0)),
                pl.BlockSpec((tt, D_MODEL), lambda i: (i, 0))]
    args = (x1, gate2, peer_tail)
    aliases = {}
    if n_tc:
        in_specs = [pl.BlockSpec(memory_space=pl.ANY)] + in_specs
        args = (y,) + args
        aliases = {0: 0}
    return pl.pallas_call(
        _finish_kernel,
        grid=((n - n_tc) // tt,),
        in_specs=in_specs,
        out_specs=pl.BlockSpec((tt, D_MODEL), row),
        out_shape=jax.ShapeDtypeStruct((n, D_MODEL), jnp.float32),
        input_output_aliases=aliases,
        compiler_params=_cparams(("arbitrary",)),
        name="peer_finish",
    )(*args)


def _pad_heads(w, used, lead=0):
    r = w.shape[0]
    w3 = w.reshape(r, N_HEADS, used)
    out = jnp.zeros((r, N_HEADS, HEAD_SLOT), w.dtype).at[:, :, lead:lead + used].set(w3)
    return out.reshape(r, N_HEADS * HEAD_SLOT)


def _pack_table(t):
    tb = t.astype(jnp.bfloat16)
    lo = lax.bitcast_convert_type(tb[:, :ROW_WORDS], jnp.uint16).astype(jnp.uint32)
    hi = lax.bitcast_convert_type(tb[:, ROW_WORDS:], jnp.uint16).astype(jnp.uint32)
    words = lax.bitcast_convert_type(lo | (hi << 16), jnp.int32)
    return words.reshape(t.shape[0] * ROW_SUB, LANES)


def _prep_weights(g_norm1, w_in, conv_w, w_conv_out, g_q_lora, w_uq, g_kv_lora, w_ukv, g_qnorm,
                  g_knorm, w_attn_out, w_out, g_norm2, peer_wq, peer_k1, peer_k2, peer_u, peer_v):
    bf = jnp.bfloat16
    o1 = 3 * D_CONV
    o2 = o1 + Q_LORA
    o3 = o2 + KV_LORA
    o4 = o3 + QK_ROPE
    o5 = o4 + D_MODEL
    kr_slot = jnp.zeros((D_MODEL, HEAD_SLOT), w_in.dtype).at[:, QK_NOPE:QK_HEAD].set(w_in[:, o3:o4])
    w_in_r = jnp.concatenate([w_in[:, :o3], kr_slot, w_in[:, o4:]], axis=1).astype(bf)
    w_ukv3 = w_ukv.reshape(KV_LORA, N_HEADS, QK_NOPE + V_HEAD)
    w_uk = _pad_heads(w_ukv3[:, :, :QK_NOPE].reshape(KV_LORA, -1), QK_NOPE)
    w_uv = _pad_heads(w_ukv3[:, :, QK_NOPE:].reshape(KV_LORA, -1), V_HEAD)
    pad_gain = lambda g: jnp.zeros((1, HEAD_SLOT), jnp.float32).at[0, :QK_HEAD].set(g)
    w_ao = jnp.zeros((N_HEADS, HEAD_SLOT, D_MODEL), w_attn_out.dtype).at[:, :V_HEAD].set(
        w_attn_out.reshape(N_HEADS, V_HEAD, D_MODEL)).reshape(N_HEADS * HEAD_SLOT, D_MODEL)
    half = PEER_DK // 2
    kcat = jnp.zeros((PEER_HEADS, 2 * N_KEYS, PEER_DK), jnp.float32)
    kcat = kcat.at[:, :N_KEYS, :half].set(peer_k1).at[:, N_KEYS:, half:].set(peer_k2)
    tab_u = _pack_table(peer_u)
    tab_v = _pack_table(peer_v)
    return {
        "g_norm1": g_norm1.reshape(1, -1), "w_in": w_in_r, "conv_w": conv_w,
        "w_conv_out": w_conv_out.astype(bf), "g_q_lora": g_q_lora.reshape(1, -1),
        "w_uq": _pad_heads(w_uq, QK_HEAD).astype(bf), "g_kv_lora": g_kv_lora.reshape(1, -1),
        "w_uk": w_uk.astype(bf), "w_uv": w_uv.astype(bf),
        "g_qnorm": pad_gain(g_qnorm), "g_knorm": pad_gain(g_knorm),
        "w_attn_out": w_ao.astype(bf), "w_out": w_out.astype(bf), "g_norm2": g_norm2.reshape(1, -1),
        "peer_wq": peer_wq.astype(bf), "kcat": kcat.astype(bf),
        "tab_u": tab_u, "tab_v": tab_v,
        "tab_u_sc": tab_u.reshape(-1, ROW_WORDS), "tab_v_sc": tab_v.reshape(-1, ROW_WORDS),
    }


def _rope_tables(seq):
    pos = jnp.arange(seq, dtype=jnp.float32)
    inv = ROPE_THETA ** (-jnp.arange(0, QK_ROPE, 2, dtype=jnp.float32) / QK_ROPE)
    ang = pos[:, None] * inv[None, :]
    cos, sin = jnp.cos(ang), jnp.sin(ang)
    t1 = slice(QK_NOPE, QK_NOPE + HALF_ROPE)
    t2 = slice(QK_NOPE + HALF_ROPE, QK_HEAD)
    cos_t = jnp.ones((seq, HEAD_SLOT), jnp.float32).at[:, t1].set(cos).at[:, t2].set(cos)
    sin_a = jnp.zeros((seq, HEAD_SLOT), jnp.float32).at[:, t1].set(-sin)
    sin_b = jnp.zeros((seq, HEAD_SLOT), jnp.float32).at[:, t2].set(sin)
    return cos_t, sin_a, sin_b


def _encoder_layer(x, c, w_ada, b_ada, wts, sc_share, after):
    bsz, seq, d = x.shape
    n = bsz * seq
    ada = _ada(c, w_ada, b_ada)[:, None, :]
    shift1, scale1, gate1, shift2, scale2, gate2 = jnp.split(ada, 6, axis=-1)
    q, k, v, a, sgb = _inproj(x, scale1, shift1, wts, _rope_tables(seq), after)
    o = _attention(q, k, v)
    x1, h2, e, g = _mix(x, a, sgb, o, gate1, scale2, shift2, wts)
    n_sc = (n * sc_share[0] // sc_share[1]) // SC_ALIGN * SC_ALIGN
    n_tc = n - n_sc
    e = e.reshape(n, N_PICKS)
    g = g.reshape(n, N_PICKS)
    h2 = h2.reshape(n, d)
    x1 = x1.reshape(n, d)
    if n_sc:
        ids = lax.shift_right_logical(e[n_tc:], ROW_SUB.bit_length() - 1)
        peer_tail = _peer_sc(ids.reshape(n_sc, 2, SC_HALF), g, h2,
                             wts["tab_u_sc"], wts["tab_v_sc"], n_tc)
    if n_tc:
        e = e.reshape(n * N_PICKS)
        w = _peer_u(e, h2.reshape(n * SUBLANES, LANES), g, wts["tab_u"], n_tc)
        y, done = _peer_v(e, w, x1, gate2, wts["tab_v"], seq, n_tc)
    else:
        y, done = None, g
    if n_sc:
        y = _peer_finish(y, x1, gate2, peer_tail, seq, n_tc)
    return y.reshape(bsz, seq, d), done


def kernel(x_prompt, x_sample, c_prompt, c_sample, w_ada, b_ada, g_norm1, w_in, conv_w, w_conv_out, g_q_lora, w_uq, g_kv_lora, w_ukv, g_qnorm, g_knorm, w_attn_out, w_out, g_norm2, peer_wq, peer_k1, peer_k2, peer_u, peer_v):
    wts = _prep_weights(g_norm1[0], w_in[0], conv_w[0], w_conv_out[0], g_q_lora[0], w_uq[0],
                        g_kv_lora[0], w_ukv[0], g_qnorm[0], g_knorm[0], w_attn_out[0], w_out[0],
                        g_norm2[0], peer_wq[0], peer_k1[0], peer_k2[0], peer_u[0], peer_v[0])
    sc_tables = (wts["tab_u_sc"], wts["tab_v_sc"])
    y_sample, sample_done = _encoder_layer(x_sample, c_sample, w_ada[0], b_ada[0], wts,
                                           SC_SHARE_FIRST, sc_tables)
    y_prompt, _ = _encoder_layer(x_prompt, c_prompt, w_ada[0], b_ada[0], wts,
                                 SC_SHARE_SECOND, sc_tables + (sample_done,))
    return y_prompt, y_sample
```

```python
import dataclasses
import functools
import math

import jax
import jax.numpy as jnp
import numpy as np
from jax import lax
from jax.experimental import pallas as pl
from jax.experimental.pallas import tpu as pltpu
from jax.experimental.pallas import tpu_sc as plsc

D_MODEL = 1024
D_CONV = 512
N_HEADS = 8
QK_NOPE = 64
QK_ROPE = 32
V_HEAD = 64
Q_LORA = 256
KV_LORA = 128
QK_HEAD = QK_NOPE + QK_ROPE
ROPE_THETA = 10000.0
PEER_HEADS = 8
N_KEYS = 128
PEER_DK = 128
PEER_TOPK = 16
EPS = 1e-6

LANES = 128
SUBLANES = 8
HEAD_SLOT = LANES
HALF_ROPE = QK_ROPE // 2
N_PICKS = PEER_HEADS * PEER_TOPK
ROW_WORDS = D_MODEL // 2
ROW_SUB = ROW_WORDS // LANES
VMEM_LIMIT = 56 * 1024 * 1024

C_BG, C_CG, C_HC = 0, 512, 1024
C_CQ = 1536
C_CKV = C_CQ + Q_LORA
C_KR = C_CKV + KV_LORA
C_GA = C_KR + HEAD_SLOT
C_GB = C_GA + D_MODEL
IN_COLS_R = C_GB + D_MODEL

TS_IN = 512
TQ = 512
TK = 2048
KV_UNROLL = 2
TS_MIX = 256
TT_PEER = 128
TOK_UNROLL_U = 8
TOK_UNROLL_V = 8
BF16_ROWS = 16
IDX_GROUP = 8

_NEG_INF = float("-inf")


def _cparams(sem):
    return pltpu.CompilerParams(dimension_semantics=sem, vmem_limit_bytes=VMEM_LIMIT)


def _ada_kernel(c_ref, w_ref, b_ref, o_ref):
    o_ref[...] = jnp.dot(c_ref[...], w_ref[...], preferred_element_type=jnp.float32,
                         precision=lax.Precision.HIGHEST) + b_ref[...]


def _ada(c, w_ada, b_ada):
    bsz = c.shape[0]
    rows = -(-bsz // SUBLANES) * SUBLANES
    c_p = jnp.zeros((rows, D_MODEL), jnp.float32).at[:bsz].set(c)
    out = pl.pallas_call(
        _ada_kernel,
        grid=(6,),
        in_specs=[pl.BlockSpec((rows, D_MODEL), lambda j: (0, 0)),
                  pl.BlockSpec((D_MODEL, D_MODEL), lambda j: (0, j)),
                  pl.BlockSpec((1, D_MODEL), lambda j: (0, j))],
        out_specs=pl.BlockSpec((rows, D_MODEL), lambda j: (0, j)),
        out_shape=jax.ShapeDtypeStruct((rows, 6 * D_MODEL), jnp.float32),
        compiler_params=_cparams(("arbitrary",)),
        name="ada",
    )(c_p, w_ada, b_ada.reshape(1, -1))
    return out[:bsz]


def _adaln(x, g, scale, shift):
    ms = jnp.mean(x * x, axis=-1, keepdims=True)
    return (x * lax.rsqrt(ms + EPS) * g) * (1.0 + scale) + shift


def _rms_rows(x, g, n):
    ms = jnp.sum(x * x, axis=-1, keepdims=True) * (1.0 / n)
    return x * lax.rsqrt(ms + EPS) * g


def _rope(x, cos, sin_a, sin_b):
    return (x * cos + pltpu.roll(x, LANES - HALF_ROPE, axis=1) * sin_a
            + pltpu.roll(x, HALF_ROPE, axis=1) * sin_b)


def _inproj_kernel(x_ref, xp_ref, xn_ref, scale_ref, shift_ref, g1_ref, w_in_ref, conv_w_ref,
                   w_co_ref, gql_ref, w_uq_ref, gkvl_ref, w_uk_ref, w_uv_ref, gqn_ref, gkn_ref,
                   cos_ref, sa_ref, sb_ref, *rest):
    q_ref, k_ref, v_ref, a_ref, sgb_ref = rest[-5:]
    i = pl.program_id(1)
    n_i = pl.num_programs(1)
    g1 = g1_ref[...]
    scale = scale_ref[...]
    shift = shift_ref[...]
    ts = x_ref.shape[0]

    h = _adaln(x_ref[...], g1, scale, shift).astype(jnp.bfloat16)

    def proj(lo, width):
        return jnp.dot(h, w_in_ref[:, lo:lo + width], preferred_element_type=jnp.float32)

    z = proj(C_CG, D_CONV) * proj(C_HC, D_CONV)

    def halo_z(xh_ref):
        hh = _adaln(xh_ref[...], g1, scale, shift).astype(jnp.bfloat16)
        zc = jnp.dot(hh, w_in_ref[:, C_CG:C_CG + 2 * D_CONV], preferred_element_type=jnp.float32)
        return zc[:, :D_CONV] * zc[:, D_CONV:]

    z_prev = halo_z(xp_ref)[SUBLANES - 1:SUBLANES, :]
    z_next = halo_z(xn_ref)[0:1, :]
    z_prev = jnp.where(i == 0, 0.0, z_prev)
    z_next = jnp.where(i == n_i - 1, 0.0, z_next)
    row = lax.broadcasted_iota(jnp.int32, (ts, D_CONV), 0)
    z_up = jnp.where(row == 0, z_prev, pltpu.roll(z, 1, axis=0))
    z_dn = jnp.where(row == ts - 1, z_next, pltpu.roll(z, ts - 1, axis=0))
    cw = conv_w_ref[...]
    y = z_up * cw[0:1, :] + z * cw[1:2, :] + z_dn * cw[2:3, :]
    out_a = jnp.dot((proj(C_BG, D_CONV) * y).astype(jnp.bfloat16), w_co_ref[...],
                    preferred_element_type=jnp.float32)
    a_ref[...] = jax.nn.sigmoid(proj(C_GA, D_MODEL)) * out_a
    sgb_ref[...] = jax.nn.sigmoid(proj(C_GB, D_MODEL))

    cos = cos_ref[...]
    sin_a = sa_ref[...]
    sin_b = sb_ref[...]
    cq = _rms_rows(proj(C_CQ, Q_LORA), gql_ref[...], Q_LORA).astype(jnp.bfloat16)
    qf = jnp.dot(cq, w_uq_ref[...], preferred_element_type=jnp.float32)
    ckv = _rms_rows(proj(C_CKV, KV_LORA), gkvl_ref[...], KV_LORA).astype(jnp.bfloat16)
    kf = jnp.dot(ckv, w_uk_ref[...], preferred_element_type=jnp.float32)
    lane = lax.broadcasted_iota(jnp.int32, (1, N_HEADS * HEAD_SLOT), 1)
    ones_lane = (jnp.bitwise_and(lane, HEAD_SLOT - 1) == V_HEAD).astype(jnp.float32)
    v_ref[...] = (jnp.dot(ckv, w_uv_ref[...], preferred_element_type=jnp.float32)
                  + ones_lane).astype(jnp.bfloat16)
    kr = proj(C_KR, HEAD_SLOT)
    gqn = gqn_ref[...]
    gkn = gkn_ref[...]
    q_scale = QK_HEAD ** -0.5 * math.log2(math.e)
    for hd in range(N_HEADS):
        sl = slice(hd * HEAD_SLOT, (hd + 1) * HEAD_SLOT)
        qh = _rope(_rms_rows(qf[:, sl], gqn, QK_HEAD), cos, sin_a, sin_b)
        q_ref[:, sl] = (qh * q_scale).astype(jnp.bfloat16)
        kh = _rope(_rms_rows(kf[:, sl] + kr, gkn, QK_HEAD), cos, sin_a, sin_b)
        k_ref[:, sl] = kh.astype(jnp.bfloat16)


def _inproj(x, scale1, shift1, wts, rope, after):
    bsz, seq, _ = x.shape
    ts = min(TS_IN, seq)
    n_i = seq // ts
    nb8 = seq // SUBLANES
    per8 = ts // SUBLANES
    tile = lambda b, i: (b, i, 0)
    per_b = lambda b, i: (b, 0, 0)
    full2 = lambda b, i: (0, 0)
    hw = N_HEADS * HEAD_SLOT
    in_specs = [
        pl.BlockSpec((None, ts, D_MODEL), tile),
        pl.BlockSpec((None, SUBLANES, D_MODEL), lambda b, i: (b, jnp.maximum(i * per8 - 1, 0), 0)),
        pl.BlockSpec((None, SUBLANES, D_MODEL), lambda b, i: (b, jnp.minimum((i + 1) * per8, nb8 - 1), 0)),
        pl.BlockSpec((None, 1, D_MODEL), per_b),
        pl.BlockSpec((None, 1, D_MODEL), per_b),
        pl.BlockSpec((1, D_MODEL), full2),
        pl.BlockSpec((D_MODEL, IN_COLS_R), full2),
        pl.BlockSpec((3, D_CONV), full2),
        pl.BlockSpec((D_CONV, D_MODEL), full2),
        pl.BlockSpec((1, Q_LORA), full2),
        pl.BlockSpec((Q_LORA, hw), full2),
        pl.BlockSpec((1, KV_LORA), full2),
        pl.BlockSpec((KV_LORA, hw), full2),
        pl.BlockSpec((KV_LORA, hw), full2),
        pl.BlockSpec((1, HEAD_SLOT), full2),
        pl.BlockSpec((1, HEAD_SLOT), full2),
        pl.BlockSpec((ts, HEAD_SLOT), lambda b, i: (i, 0)),
        pl.BlockSpec((ts, HEAD_SLOT), lambda b, i: (i, 0)),
        pl.BlockSpec((ts, HEAD_SLOT), lambda b, i: (i, 0)),
    ] + [pl.BlockSpec(memory_space=pl.ANY)] * len(after)
    out_specs = [pl.BlockSpec((None, ts, hw), tile)] * 3 + [pl.BlockSpec((None, ts, D_MODEL), tile)] * 2
    out_shape = ([jax.ShapeDtypeStruct((bsz, seq, hw), jnp.bfloat16)] * 3
                 + [jax.ShapeDtypeStruct((bsz, seq, D_MODEL), jnp.float32)] * 2)
    return pl.pallas_call(
        _inproj_kernel,
        grid=(bsz, n_i),
        in_specs=in_specs,
        out_specs=out_specs,
        out_shape=out_shape,
        compiler_params=_cparams(("parallel", "parallel")),
        name="inproj",
    )(x, x, x, scale1, shift1, wts["g_norm1"], wts["w_in"], wts["conv_w"], wts["w_conv_out"],
      wts["g_q_lora"], wts["w_uq"], wts["g_kv_lora"], wts["w_uk"], wts["w_uv"],
      wts["g_qnorm"], wts["g_knorm"], rope[0], rope[1], rope[2], *after)


def _attn_kernel(q_ref, k_ref, v_ref, o_ref):
    q = q_ref[...]
    tq = q.shape[0]
    seq = k_ref.shape[0]
    tk = min(TK, seq)

    n_chunks = seq // tk
    unroll = KV_UNROLL if n_chunks % KV_UNROLL == 0 else 1

    def chunk(off, carry):
        m, acc = carry
        kc = k_ref[pl.ds(off, tk), :]
        vc = v_ref[pl.ds(off, tk), :]
        s = lax.dot_general(q, kc, (((1,), (1,)), ((), ())), preferred_element_type=jnp.float32)
        m_new = jnp.maximum(m, jnp.max(s, axis=-1, keepdims=True))
        p = jnp.exp2(s - m_new).astype(jnp.bfloat16)
        alpha = jnp.exp2(m - m_new)
        acc = alpha * acc + jnp.dot(p, vc, preferred_element_type=jnp.float32)
        return m_new, acc

    def body(j, carry):
        for u in range(unroll):
            carry = chunk(pl.multiple_of((j * unroll + u) * tk, tk), carry)
        return carry

    m0 = jnp.full((tq, 1), _NEG_INF, jnp.float32)
    acc0 = jnp.zeros((tq, HEAD_SLOT), jnp.float32)
    _, acc = lax.fori_loop(0, n_chunks // unroll, body, (m0, acc0))
    o_ref[...] = (acc / acc[:, V_HEAD:V_HEAD + 1]).astype(jnp.bfloat16)


def _attention(q, k, v):
    bsz, seq, hw = q.shape
    tq = min(TQ, seq)
    return pl.pallas_call(
        _attn_kernel,
        grid=(bsz, N_HEADS, seq // tq),
        in_specs=[pl.BlockSpec((None, tq, HEAD_SLOT), lambda b, h, i: (b, i, h)),
                  pl.BlockSpec((None, seq, HEAD_SLOT), lambda b, h, i: (b, 0, h)),
                  pl.BlockSpec((None, seq, HEAD_SLOT), lambda b, h, i: (b, 0, h))],
        out_specs=pl.BlockSpec((None, tq, HEAD_SLOT), lambda b, h, i: (b, i, h)),
        out_shape=jax.ShapeDtypeStruct((bsz, seq, hw), jnp.bfloat16),
        compiler_params=_cparams(("parallel", "parallel", "arbitrary")),
        name="attn",
    )(q, k, v)


def _topk_rows(s, iota):
    vals, idxs = [], []
    big = jnp.float32(1e9)
    for _ in range(PEER_TOPK):
        m = jnp.max(s, axis=0, keepdims=True)
        am = jnp.min(jnp.where(s == m, iota, big), axis=0, keepdims=True)
        vals.append(m)
        idxs.append(am)
        s = jnp.where(iota == am, _NEG_INF, s)
    return jnp.concatenate(vals, axis=0), jnp.concatenate(idxs, axis=0)


def _take_rows(tab, idx):
    out = jnp.zeros_like(tab)
    for a in range(tab.shape[0]):
        out = out + jnp.where(idx == jnp.float32(a), tab[a:a + 1, :], 0.0)
    return out


_CAND = [(a, b) for a in range(PEER_TOPK) for b in range(PEER_TOPK) if (a + 1) * (b + 1) <= PEER_TOPK]
N_CAND = -(-len(_CAND) // SUBLANES) * SUBLANES
_PAD_LABEL = float(PEER_TOPK * PEER_TOPK)


def _candidate_tables():
    sel = np.zeros((2, N_CAND, PEER_TOPK), np.float32)
    lab = np.full((N_CAND, 1), _PAD_LABEL, np.float32)
    for r, (a, b) in enumerate(_CAND):
        sel[0, r, a] = 1.0
        sel[1, r, b] = 1.0
        lab[r, 0] = a * PEER_TOPK + b
    return sel, lab


def _mix_kernel(x_ref, a_ref, sgb_ref, o_ref, gate1_ref, scale2_ref, shift2_ref, g2_ref,
                w_ao_ref, w_out_ref, w_pq_ref, kcat_ref, sel_ref, lab_ref,
                x1_ref, h2_ref, e_ref, g_ref):
    ts = x_ref.shape[0]
    out_b = jnp.dot(o_ref[...], w_ao_ref[...], preferred_element_type=jnp.float32)
    merged = (a_ref[...] + sgb_ref[...] * out_b).astype(jnp.bfloat16)
    x1 = x_ref[...] + gate1_ref[...] * jnp.dot(merged, w_out_ref[...],
                                              preferred_element_type=jnp.float32)
    x1_ref[...] = x1
    h2 = _adaln(x1, g2_ref[...], scale2_ref[...], shift2_ref[...])
    h2_ref[...] = h2
    qp = jnp.dot(h2.astype(jnp.bfloat16), w_pq_ref[...],
                 preferred_element_type=jnp.float32).astype(jnp.bfloat16)

    iota_k = lax.broadcasted_iota(jnp.int32, (N_KEYS, ts), 0).astype(jnp.float32)
    lab = lab_ref[...]
    valid = lab < _PAD_LABEL
    sel_a = sel_ref[0]
    sel_b = sel_ref[1]
    exact = dict(preferred_element_type=jnp.float32, precision=lax.Precision.HIGHEST)
    e_rows, g_rows = [], []
    for hd in range(PEER_HEADS):
        qh = qp[:, hd * PEER_DK:(hd + 1) * PEER_DK]
        st = lax.dot_general(kcat_ref[hd], qh, (((1,), (1,)), ((), ())),
                             preferred_element_type=jnp.float32)
        v1, i1 = _topk_rows(st[:N_KEYS], iota_k)
        v2, i2 = _topk_rows(st[N_KEYS:], iota_k)
        comb = jnp.dot(sel_a, v1, **exact) + jnp.dot(sel_b, v2, **exact)
        sc, pos = _topk_rows(jnp.where(valid, comb, _NEG_INF), lab)
        ia = jnp.floor(pos * (1.0 / PEER_TOPK))
        ib = pos - ia * PEER_TOPK
        e_rows.append(_take_rows(i1, ia) * N_KEYS + _take_rows(i2, ib))
        p = jnp.exp(sc - sc[0:1, :])
        g_rows.append(p / jnp.sum(p, axis=0, keepdims=True))
    e_all = jnp.concatenate(e_rows, axis=0)
    g_all = jnp.concatenate(g_rows, axis=0)
    for c in range(ts // LANES):
        cs = slice(c * LANES, (c + 1) * LANES)
        e_ref[cs, :] = (e_all[:, cs].T * ROW_SUB).astype(jnp.int32)
        g_ref[cs, :] = g_all[:, cs].T


def _mix(x, a, sgb, o, gate1, scale2, shift2, wts):
    bsz, seq, _ = x.shape
    ts = min(TS_MIX, seq)
    tile = lambda b, i: (b, i, 0)
    per_b = lambda b, i: (b, 0, 0)
    full2 = lambda b, i: (0, 0)
    hw = N_HEADS * HEAD_SLOT
    big = pl.BlockSpec((None, ts, D_MODEL), tile)
    vec = pl.BlockSpec((None, 1, D_MODEL), per_b)
    picks = pl.BlockSpec((None, ts, N_PICKS), tile)
    sel, lab = _candidate_tables()
    return pl.pallas_call(
        _mix_kernel,
        grid=(bsz, seq // ts),
        in_specs=[big, big, big, pl.BlockSpec((None, ts, hw), tile), vec, vec, vec,
                  pl.BlockSpec((1, D_MODEL), full2),
                  pl.BlockSpec((hw, D_MODEL), full2),
                  pl.BlockSpec((D_MODEL, D_MODEL), full2),
                  pl.BlockSpec((D_MODEL, PEER_HEADS * PEER_DK), full2),
                  pl.BlockSpec((PEER_HEADS, 2 * N_KEYS, PEER_DK), lambda b, i: (0, 0, 0)),
                  pl.BlockSpec((2, N_CAND, PEER_TOPK), lambda b, i: (0, 0, 0)),
                  pl.BlockSpec((N_CAND, ts), full2)],
        out_specs=[big, big, picks, picks],
        out_shape=[jax.ShapeDtypeStruct((bsz, seq, D_MODEL), jnp.float32),
                   jax.ShapeDtypeStruct((bsz, seq, D_MODEL), jnp.float32),
                   jax.ShapeDtypeStruct((bsz, seq, N_PICKS), jnp.int32),
                   jax.ShapeDtypeStruct((bsz, seq, N_PICKS), jnp.float32)],
        compiler_params=_cparams(("parallel", "parallel")),
        name="mix",
    )(x, a, sgb, o, gate1, scale2, shift2, wts["g_norm2"], wts["w_attn_out"], wts["w_out"],
      wts["peer_wq"], wts["kcat"], jnp.asarray(sel), jnp.broadcast_to(jnp.asarray(lab), (N_CAND, ts)))


def _unpack_row(w):
    lo = lax.bitcast_convert_type(lax.shift_left(w, 16), jnp.float32)
    hi = lax.bitcast_convert_type(jnp.bitwise_and(w, jnp.int32(-65536)), jnp.float32)
    return lo, hi


def _gelu_tanh(x):
    c = math.sqrt(2.0 / math.pi)
    return 0.5 * x * (1.0 + jnp.tanh(c * (x + 0.044715 * (x * x * x))))


def _gather_rows(e_ref, t, tab_ref, pb):
    for j in range(N_PICKS // IDX_GROUP):
        row = e_ref.at[pl.ds(pl.multiple_of(t * N_PICKS + j * IDX_GROUP, IDX_GROUP), IDX_GROUP)]
        for c in range(IDX_GROUP):
            k = j * IDX_GROUP + c
            r = pl.multiple_of(row[c], ROW_SUB)
            pb[k * ROW_SUB:(k + 1) * ROW_SUB, :] = tab_ref[pl.ds(r, ROW_SUB), :]


def _pick_block(pb, s):
    return pb[pl.ds(s, N_PICKS, stride=ROW_SUB), :]


def _peer_u_kernel(e_ref, x_ref, g_ref, tab_ref, w_ref, pbuf0, pbuf1, abuf):
    tt = g_ref.shape[0]
    pbufs = (pbuf0, pbuf1)
    lane = lax.broadcasted_iota(jnp.int32, (N_PICKS, tt), 1)
    abuf[...] = jnp.zeros_like(abuf)

    def tok(i, carry):
        for u in range(TOK_UNROLL_U):
            t = i * TOK_UNROLL_U + u
            pb = pbufs[u % 2]
            x_blk = x_ref[pl.ds(pl.multiple_of(t * SUBLANES, SUBLANES), SUBLANES), :]
            _gather_rows(e_ref, t, tab_ref, pb)
            acc = None
            for s in range(ROW_SUB):
                lo, hi = _unpack_row(_pick_block(pb, s))
                term = lo * x_blk[s:s + 1, :] + hi * x_blk[ROW_SUB + s:ROW_SUB + s + 1, :]
                acc = term if acc is None else acc + term
            a_col = jnp.sum(acc, axis=1, keepdims=True)
            abuf[...] = jnp.where(lane == t, a_col, abuf[...])
        return carry

    lax.fori_loop(0, tt // TOK_UNROLL_U, tok, 0)
    w_ref[...] = g_ref[...] * _gelu_tanh(abuf[...].T)


def _peer_u(e, h2_rows, g, tab, n):
    tt = TT_PEER
    return pl.pallas_call(
        _peer_u_kernel,
        grid=(n // tt,),
        in_specs=[pl.BlockSpec((tt * N_PICKS,), lambda i: (i,), memory_space=pltpu.SMEM),
                  pl.BlockSpec((tt * SUBLANES, LANES), lambda i: (i, 0)),
                  pl.BlockSpec((tt, N_PICKS), lambda i: (i, 0)),
                  pl.BlockSpec(memory_space=pltpu.VMEM)],
        out_specs=pl.BlockSpec((tt, N_PICKS), lambda i: (i, 0)),
        out_shape=jax.ShapeDtypeStruct((n, N_PICKS), jnp.float32),
        scratch_shapes=[pltpu.VMEM((N_PICKS * ROW_SUB, LANES), jnp.int32),
                        pltpu.VMEM((N_PICKS * ROW_SUB, LANES), jnp.int32),
                        pltpu.VMEM((N_PICKS, tt), jnp.float32)],
        compiler_params=_cparams(("arbitrary",)),
        name="peer_u",
    )(e, h2_rows, g, tab)


def _peer_v_kernel(e_ref, w_ref, x_ref, gate_ref, tab_ref, y_ref, done_ref,
                   pbuf0, pbuf1, whi_ref, wlo_ref, acc_ref):
    tt = x_ref.shape[0]
    pbufs = (pbuf0, pbuf1)
    w = w_ref[...]
    w_hi = w.astype(jnp.bfloat16)
    whi_ref[...] = w_hi
    wlo_ref[...] = (w - w_hi.astype(jnp.float32)).astype(jnp.bfloat16)
    sub = lax.broadcasted_iota(jnp.int32, (BF16_ROWS, D_MODEL), 0)
    acc_ref[...] = jnp.zeros_like(acc_ref)

    def process(pb, t):
        b0 = pl.multiple_of(lax.shift_left(lax.shift_right_logical(t, 4), 4), BF16_ROWS)
        lhs = jnp.concatenate([whi_ref[pl.ds(b0, BF16_ROWS), :], wlo_ref[pl.ds(b0, BF16_ROWS), :]], axis=0)
        los, his = [], []
        for s in range(ROW_SUB):
            lo, hi = _unpack_row(_pick_block(pb, s))
            los.append(lo.astype(jnp.bfloat16))
            his.append(hi.astype(jnp.bfloat16))
        rhs = jnp.concatenate(los + his, axis=1)
        res = jnp.dot(lhs, rhs, preferred_element_type=jnp.float32)
        r = res[:BF16_ROWS, :] + res[BF16_ROWS:, :]
        mask = sub == jnp.bitwise_and(t, BF16_ROWS - 1)
        acc_ref[pl.ds(b0, BF16_ROWS), :] = jnp.where(mask, r, acc_ref[pl.ds(b0, BF16_ROWS), :])

    def step(i, carry):
        t0 = i * TOK_UNROLL_V
        for u in range(TOK_UNROLL_V):
            _gather_rows(e_ref, t0 + u, tab_ref, pbufs[u % 2])
            process(pbufs[u % 2], t0 + u)
        return carry

    lax.fori_loop(0, tt // TOK_UNROLL_V, step, 0)
    y_ref[...] = x_ref[...] + gate_ref[...] * acc_ref[...]
    done_ref[...] = jnp.zeros_like(done_ref)


def _peer_v(e, w, x1, gate2, tab, seq, n_tc):
    n = x1.shape[0]
    tt = TT_PEER
    return pl.pallas_call(
        _peer_v_kernel,
        grid=(n_tc // tt,),
        in_specs=[pl.BlockSpec((tt * N_PICKS,), lambda i: (i,), memory_space=pltpu.SMEM),
                  pl.BlockSpec((tt, N_PICKS), lambda i: (i, 0)),
                  pl.BlockSpec((tt, D_MODEL), lambda i: (i, 0)),
                  pl.BlockSpec((None, 1, D_MODEL), lambda i: ((i * tt) // seq, 0, 0)),
                  pl.BlockSpec(memory_space=pltpu.VMEM)],
        out_specs=[pl.BlockSpec((tt, D_MODEL), lambda i: (i, 0)),
                   pl.BlockSpec((SUBLANES, LANES), lambda i: (0, 0))],
        out_shape=[jax.ShapeDtypeStruct((n, D_MODEL), jnp.float32),
                   jax.ShapeDtypeStruct((SUBLANES, LANES), jnp.float32)],
        scratch_shapes=[pltpu.VMEM((N_PICKS * ROW_SUB, LANES), jnp.int32),
                        pltpu.VMEM((N_PICKS * ROW_SUB, LANES), jnp.int32),
                        pltpu.VMEM((tt, N_PICKS), jnp.bfloat16),
                        pltpu.VMEM((tt, N_PICKS), jnp.bfloat16),
                        pltpu.VMEM((tt, D_MODEL), jnp.float32)],
        compiler_params=_cparams(("arbitrary",)),
        name="peer_v",
    )(e, w, x1, gate2, tab)


SC_CORES = 2
SC_SUBCORES = 16
SC_LANES = 16
SC_WORKERS = SC_CORES * SC_SUBCORES
SC_HALF = N_PICKS // 2
SC_CHUNKS = ROW_WORDS // SC_LANES
SC_ACC_CHUNKS = 8
SC_SHARE_FIRST = (1, 1)
SC_SHARE_SECOND = (0, 1)
SC_ALIGN = 1024


def _sc_params():
    cp = pltpu.CompilerParams()
    if "needs_layout_passes" in pltpu.CompilerParams.__dataclass_fields__:
        cp = dataclasses.replace(cp, needs_layout_passes=False)
    return cp


def _gelu_tanh_via_exp(x):
    c = math.sqrt(2.0 / math.pi)
    z = c * (x + 0.044715 * (x * x * x))
    t = jnp.exp(-2.0 * jnp.abs(z))
    return 0.5 * x * (1.0 + jnp.sign(z) * (1.0 - t) / (1.0 + t))


def _peer_sc(e2, g, h2, tab_u, tab_v, off):
    n_sc = e2.shape[0]
    per_w = n_sc // SC_WORKERS
    mesh = plsc.VectorSubcoreMesh(core_axis_name="c", subcore_axis_name="s")
    ln = SC_LANES

    @functools.partial(
        pl.kernel, mesh=mesh,
        out_type=jax.ShapeDtypeStruct((n_sc, D_MODEL), jnp.float32),
        scratch_types=[pltpu.VMEM((SC_HALF,), jnp.int32), pltpu.VMEM((SC_HALF,), jnp.int32),
                       pltpu.VMEM((SC_HALF, ROW_WORDS), jnp.int32), pltpu.VMEM((SC_HALF, ROW_WORDS), jnp.int32),
                       pltpu.VMEM((D_MODEL,), jnp.float32), pltpu.VMEM((N_PICKS,), jnp.float32),
                       pltpu.VMEM((N_PICKS,), jnp.float32), pltpu.VMEM((D_MODEL,), jnp.float32),
                       pltpu.SemaphoreType.DMA, pltpu.SemaphoreType.DMA],
        compiler_params=_sc_params(),
        cost_estimate=pl.CostEstimate(
            flops=4 * n_sc * N_PICKS * D_MODEL,
            transcendentals=n_sc * N_PICKS,
            bytes_accessed=n_sc * (2 * N_PICKS * ROW_WORDS * 4 + 3 * D_MODEL * 4 + 3 * N_PICKS * 4)),
    )
    def body(e_hbm, g_hbm, x_hbm, tu_hbm, tv_hbm, out_hbm,
             idx_a, idx_b, buf_a, buf_b, x_v, g_v, w_v, o_v, sem_a, sem_b):
        wid = lax.axis_index("s") * SC_CORES + lax.axis_index("c")
        lane = lax.iota(jnp.int32, ln)
        zero = jnp.zeros((ln,), jnp.float32)

        def u_half(buf, k0):
            @pl.loop(0, SC_HALF // ln)
            def _(grp):
                def chunk(j, accs):
                    xl = x_v[pl.ds(j * ln, ln)]
                    xh = x_v[pl.ds(ROW_WORDS + j * ln, ln)]
                    new = []
                    for kk in range(ln):
                        lo, hi = _unpack_row(buf[grp * ln + kk, pl.ds(j * ln, ln)])
                        new.append(accs[kk] + lo * xl + hi * xh)
                    return tuple(new)

                accs = lax.fori_loop(0, SC_CHUNKS, chunk, tuple(zero for _ in range(ln)))
                a = zero
                for kk in range(ln):
                    a = jnp.where(lane == kk, jnp.sum(accs[kk]), a)
                sl = pl.ds(k0 + grp * ln, ln)
                w_v[sl] = g_v[sl] * _gelu_tanh_via_exp(a)

        def v_half(buf, k0):
            @pl.loop(0, SC_CHUNKS // SC_ACC_CHUNKS)
            def _(jb):
                def pick(kq, accs):
                    wk = plsc.load_gather(w_v, [jnp.full((ln,), k0, jnp.int32) + kq])
                    new = []
                    for c in range(SC_ACC_CHUNKS):
                        lo, hi = _unpack_row(buf[kq, pl.ds((jb * SC_ACC_CHUNKS + c) * ln, ln)])
                        new.append(accs[2 * c] + wk * lo)
                        new.append(accs[2 * c + 1] + wk * hi)
                    return tuple(new)

                accs = lax.fori_loop(0, SC_HALF, pick, tuple(zero for _ in range(2 * SC_ACC_CHUNKS)))
                for c in range(SC_ACC_CHUNKS):
                    col = (jb * SC_ACC_CHUNKS + c) * ln
                    o_v[pl.ds(col, ln)] = o_v[pl.ds(col, ln)] + accs[2 * c]
                    o_v[pl.ds(ROW_WORDS + col, ln)] = o_v[pl.ds(ROW_WORDS + col, ln)] + accs[2 * c + 1]

        @pl.loop(0, per_w)
        def _(i):
            tl = wid * per_w + i
            tg = off + tl
            pltpu.sync_copy(e_hbm.at[tl, 0], idx_a)
            pltpu.sync_copy(e_hbm.at[tl, 1], idx_b)
            pltpu.sync_copy(x_hbm.at[tg], x_v)
            pltpu.sync_copy(g_hbm.at[tg], g_v)
            u_a = pltpu.async_copy(tu_hbm.at[idx_a], buf_a, sem_a)
            u_b = pltpu.async_copy(tu_hbm.at[idx_b], buf_b, sem_b)

            @pl.loop(0, D_MODEL // ln)
            def _(c):
                o_v[pl.ds(c * ln, ln)] = zero

            u_a.wait()
            u_half(buf_a, 0)
            v_a = pltpu.async_copy(tv_hbm.at[idx_a], buf_a, sem_a)
            u_b.wait()
            u_half(buf_b, SC_HALF)
            v_b = pltpu.async_copy(tv_hbm.at[idx_b], buf_b, sem_b)
            v_a.wait()
            v_half(buf_a, 0)
            v_b.wait()
            v_half(buf_b, SC_HALF)
            pltpu.sync_copy(o_v, out_hbm.at[tl])

    return body(e2, g, h2, tab_u, tab_v)


def _finish_kernel(*refs):
    x_ref, gate_ref, p_ref, y_ref = refs[-4:]
    y_ref[...] = x_ref[...] + gate_ref[...] * p_ref[...]


def _peer_finish(y, x1, gate2, peer_tail, seq, n_tc):
    n = x1.shape[0]
    tt = TT_PEER
    first = n_tc // tt
    row = lambda i: (first + i, 0)
    in_specs = [pl.BlockSpec((tt, D_MODEL), row),
                pl.BlockSpec((None, 1, D_MODEL), lambda i: (((first + i) * tt) // seq, 0, 0)),
                pl.BlockSpec((tt, D_MODEL), lambda i: (i, 0))]
    args = (x1, gate2, peer_tail)
    aliases = {}
    if n_tc:
        in_specs = [pl.BlockSpec(memory_space=pl.ANY)] + in_specs
        args = (y,) + args
        aliases = {0: 0}
    return pl.pallas_call(
        _finish_kernel,
        grid=((n - n_tc) // tt,),
        in_specs=in_specs,
        out_specs=pl.BlockSpec((tt, D_MODEL), row),
        out_shape=jax.ShapeDtypeStruct((n, D_MODEL), jnp.float32),
        input_output_aliases=aliases,
        compiler_params=_cparams(("arbitrary",)),
        name="peer_finish",
    )(*args)


def _pad_heads(w, used, lead=0):
    r = w.shape[0]
    w3 = w.reshape(r, N_HEADS, used)
    out = jnp.zeros((r, N_HEADS, HEAD_SLOT), w.dtype).at[:, :, lead:lead + used].set(w3)
    return out.reshape(r, N_HEADS * HEAD_SLOT)


def _pack_table(t):
    tb = t.astype(jnp.bfloat16)
    lo = lax.bitcast_convert_type(tb[:, :ROW_WORDS], jnp.uint16).astype(jnp.uint32)
    hi = lax.bitcast_convert_type(tb[:, ROW_WORDS:], jnp.uint16).astype(jnp.uint32)
    words = lax.bitcast_convert_type(lo | (hi << 16), jnp.int32)
    return words.reshape(t.shape[0] * ROW_SUB, LANES)


def _prep_weights(g_norm1, w_in, conv_w, w_conv_out, g_q_lora, w_uq, g_kv_lora, w_ukv, g_qnorm,
                  g_knorm, w_attn_out, w_out, g_norm2, peer_wq, peer_k1, peer_k2, peer_u, peer_v):
    bf = jnp.bfloat16
    o1 = 3 * D_CONV
    o2 = o1 + Q_LORA
    o3 = o2 + KV_LORA
    o4 = o3 + QK_ROPE
    o5 = o4 + D_MODEL
    kr_slot = jnp.zeros((D_MODEL, HEAD_SLOT), w_in.dtype).at[:, QK_NOPE:QK_HEAD].set(w_in[:, o3:o4])
    w_in_r = jnp.concatenate([w_in[:, :o3], kr_slot, w_in[:, o4:]], axis=1).astype(bf)
    w_ukv3 = w_ukv.reshape(KV_LORA, N_HEADS, QK_NOPE + V_HEAD)
    w_uk = _pad_heads(w_ukv3[:, :, :QK_NOPE].reshape(KV_LORA, -1), QK_NOPE)
    w_uv = _pad_heads(w_ukv3[:, :, QK_NOPE:].reshape(KV_LORA, -1), V_HEAD)
    pad_gain = lambda g: jnp.zeros((1, HEAD_SLOT), jnp.float32).at[0, :QK_HEAD].set(g)
    w_ao = jnp.zeros((N_HEADS, HEAD_SLOT, D_MODEL), w_attn_out.dtype).at[:, :V_HEAD].set(
        w_attn_out.reshape(N_HEADS, V_HEAD, D_MODEL)).reshape(N_HEADS * HEAD_SLOT, D_MODEL)
    half = PEER_DK // 2
    kcat = jnp.zeros((PEER_HEADS, 2 * N_KEYS, PEER_DK), jnp.float32)
    kcat = kcat.at[:, :N_KEYS, :half].set(peer_k1).at[:, N_KEYS:, half:].set(peer_k2)
    tab_u = _pack_table(peer_u)
    tab_v = _pack_table(peer_v)
    return {
        "g_norm1": g_norm1.reshape(1, -1), "w_in": w_in_r, "conv_w": conv_w,
        "w_conv_out": w_conv_out.astype(bf), "g_q_lora": g_q_lora.reshape(1, -1),
        "w_uq": _pad_heads(w_uq, QK_HEAD).astype(bf), "g_kv_lora": g_kv_lora.reshape(1, -1),
        "w_uk": w_uk.astype(bf), "w_uv": w_uv.astype(bf),
        "g_qnorm": pad_gain(g_qnorm), "g_knorm": pad_gain(g_knorm),
        "w_attn_out": w_ao.astype(bf), "w_out": w_out.astype(bf), "g_norm2": g_norm2.reshape(1, -1),
        "peer_wq": peer_wq.astype(bf), "kcat": kcat.astype(bf),
        "tab_u": tab_u, "tab_v": tab_v,
        "tab_u_sc": tab_u.reshape(-1, ROW_WORDS), "tab_v_sc": tab_v.reshape(-1, ROW_WORDS),
    }


def _rope_tables(seq):
    pos = jnp.arange(seq, dtype=jnp.float32)
    inv = ROPE_THETA ** (-jnp.arange(0, QK_ROPE, 2, dtype=jnp.float32) / QK_ROPE)
    ang = pos[:, None] * inv[None, :]
    cos, sin = jnp.cos(ang), jnp.sin(ang)
    t1 = slice(QK_NOPE, QK_NOPE + HALF_ROPE)
    t2 = slice(QK_NOPE + HALF_ROPE, QK_HEAD)
    cos_t = jnp.ones((seq, HEAD_SLOT), jnp.float32).at[:, t1].set(cos).at[:, t2].set(cos)
    sin_a = jnp.zeros((seq, HEAD_SLOT), jnp.float32).at[:, t1].set(-sin)
    sin_b = jnp.zeros((seq, HEAD_SLOT), jnp.float32).at[:, t2].set(sin)
    return cos_t, sin_a, sin_b


def _encoder_layer(x, c, w_ada, b_ada, wts, sc_share, after):
    bsz, seq, d = x.shape
    n = bsz * seq
    ada = _ada(c, w_ada, b_ada)[:, None, :]
    shift1, scale1, gate1, shift2, scale2, gate2 = jnp.split(ada, 6, axis=-1)
    q, k, v, a, sgb = _inproj(x, scale1, shift1, wts, _rope_tables(seq), after)
    o = _attention(q, k, v)
    x1, h2, e, g = _mix(x, a, sgb, o, gate1, scale2, shift2, wts)
    n_sc = (n * sc_share[0] // sc_share[1]) // SC_ALIGN * SC_ALIGN
    n_tc = n - n_sc
    e = e.reshape(n, N_PICKS)
    g = g.reshape(n, N_PICKS)
    h2 = h2.reshape(n, d)
    x1 = x1.reshape(n, d)
    if n_sc:
        ids = lax.shift_right_logical(e[n_tc:], ROW_SUB.bit_length() - 1)
        peer_tail = _peer_sc(ids.reshape(n_sc, 2, SC_HALF), g, h2,
                             wts["tab_u_sc"], wts["tab_v_sc"], n_tc)
    if n_tc:
        e = e.reshape(n * N_PICKS)
        w = _peer_u(e, h2.reshape(n * SUBLANES, LANES), g, wts["tab_u"], n_tc)
        y, done = _peer_v(e, w, x1, gate2, wts["tab_v"], seq, n_tc)
    else:
        y, done = None, g
    if n_sc:
        y = _peer_finish(y, x1, gate2, peer_tail, seq, n_tc)
    return y.reshape(bsz, seq, d), done


def kernel(x_prompt, x_sample, c_prompt, c_sample, w_ada, b_ada, g_norm1, w_in, conv_w, w_conv_out, g_q_lora, w_uq, g_kv_lora, w_ukv, g_qnorm, g_knorm, w_attn_out, w_out, g_norm2, peer_wq, peer_k1, peer_k2, peer_u, peer_v):
    wts = _prep_weights(g_norm1[0], w_in[0], conv_w[0], w_conv_out[0], g_q_lora[0], w_uq[0],
                        g_kv_lora[0], w_ukv[0], g_qnorm[0], g_knorm[0], w_attn_out[0], w_out[0],
                        g_norm2[0], peer_wq[0], peer_k1[0], peer_k2[0], peer_u[0], peer_v[0])
    sc_tables = (wts["tab_u_sc"], wts["tab_v_sc"])
    y_sample, sample_done = _encoder_layer(x_sample, c_sample, w_ada[0], b_ada[0], wts,
                                           SC_SHARE_FIRST, sc_tables)
    y_prompt, _ = _encoder_layer(x_prompt, c_prompt, w_ada[0], b_ada[0], wts,
                                 SC_SHARE_SECOND, sc_tables + (sample_done,))
    return y_prompt, y_sample
```

```python
import dataclasses
import functools
import math

import jax
import jax.numpy as jnp
import numpy as np
from jax import lax
from jax.experimental import pallas as pl
from jax.experimental.pallas import tpu as pltpu
from jax.experimental.pallas import tpu_sc as plsc

D_MODEL = 1024
D_CONV = 512
N_HEADS = 8
QK_NOPE = 64
QK_ROPE = 32
V_HEAD = 64
Q_LORA = 256
KV_LORA = 128
QK_HEAD = QK_NOPE + QK_ROPE
ROPE_THETA = 10000.0
PEER_HEADS = 8
N_KEYS = 128
PEER_DK = 128
PEER_TOPK = 16
EPS = 1e-6

LANES = 128
SUBLANES = 8
HEAD_SLOT = LANES
HALF_ROPE = QK_ROPE // 2
N_PICKS = PEER_HEADS * PEER_TOPK
ROW_WORDS = D_MODEL // 2
ROW_SUB = ROW_WORDS // LANES
VMEM_LIMIT = 56 * 1024 * 1024

C_BG, C_CG, C_HC = 0, 512, 1024
C_CQ = 1536
C_CKV = C_CQ + Q_LORA
C_KR = C_CKV + KV_LORA
C_GA = C_KR + HEAD_SLOT
C_GB = C_GA + D_MODEL
IN_COLS_R = C_GB + D_MODEL

TS_IN = 512
TQ = 512
TK = 2048
KV_UNROLL = 2
TS_MIX = 256
TT_PEER = 128
TOK_UNROLL_U = 16
TOK_UNROLL_V = 16
BF16_ROWS = 16
IDX_GROUP = 8

_NEG_INF = float("-inf")


def _cparams(sem):
    return pltpu.CompilerParams(dimension_semantics=sem, vmem_limit_bytes=VMEM_LIMIT)


def _ada_kernel(c_ref, w_ref, b_ref, o_ref):
    o_ref[...] = jnp.dot(c_ref[...], w_ref[...], preferred_element_type=jnp.float32,
                         precision=lax.Precision.HIGHEST) + b_ref[...]


def _ada(c, w_ada, b_ada):
    bsz = c.shape[0]
    rows = -(-bsz // SUBLANES) * SUBLANES
    c_p = jnp.zeros((rows, D_MODEL), jnp.float32).at[:bsz].set(c)
    out = pl.pallas_call(
        _ada_kernel,
        grid=(6,),
        in_specs=[pl.BlockSpec((rows, D_MODEL), lambda j: (0, 0)),
                  pl.BlockSpec((D_MODEL, D_MODEL), lambda j: (0, j)),
                  pl.BlockSpec((1, D_MODEL), lambda j: (0, j))],
        out_specs=pl.BlockSpec((rows, D_MODEL), lambda j: (0, j)),
        out_shape=jax.ShapeDtypeStruct((rows, 6 * D_MODEL), jnp.float32),
        compiler_params=_cparams(("arbitrary",)),
        name="ada",
    )(c_p, w_ada, b_ada.reshape(1, -1))
    return out[:bsz]


def _adaln(x, g, scale, shift):
    ms = jnp.mean(x * x, axis=-1, keepdims=True)
    return (x * lax.rsqrt(ms + EPS) * g) * (1.0 + scale) + shift


def _rms_rows(x, g, n):
    ms = jnp.sum(x * x, axis=-1, keepdims=True) * (1.0 / n)
    return x * lax.rsqrt(ms + EPS) * g


def _rope(x, cos, sin_a, sin_b):
    return (x * cos + pltpu.roll(x, LANES - HALF_ROPE, axis=1) * sin_a
            + pltpu.roll(x, HALF_ROPE, axis=1) * sin_b)


def _inproj_kernel(x_ref, xp_ref, xn_ref, scale_ref, shift_ref, g1_ref, w_in_ref, conv_w_ref,
                   w_co_ref, gql_ref, w_uq_ref, gkvl_ref, w_uk_ref, w_uv_ref, gqn_ref, gkn_ref,
                   cos_ref, sa_ref, sb_ref, *rest):
    q_ref, k_ref, v_ref, a_ref, sgb_ref = rest[-5:]
    i = pl.program_id(1)
    n_i = pl.num_programs(1)
    g1 = g1_ref[...]
    scale = scale_ref[...]
    shift = shift_ref[...]
    ts = x_ref.shape[0]

    h = _adaln(x_ref[...], g1, scale, shift).astype(jnp.bfloat16)

    def proj(lo, width):
        return jnp.dot(h, w_in_ref[:, lo:lo + width], preferred_element_type=jnp.float32)

    z = proj(C_CG, D_CONV) * proj(C_HC, D_CONV)

    def halo_z(xh_ref):
        hh = _adaln(xh_ref[...], g1, scale, shift).astype(jnp.bfloat16)
        zc = jnp.dot(hh, w_in_ref[:, C_CG:C_CG + 2 * D_CONV], preferred_element_type=jnp.float32)
        return zc[:, :D_CONV] * zc[:, D_CONV:]

    z_prev = halo_z(xp_ref)[SUBLANES - 1:SUBLANES, :]
    z_next = halo_z(xn_ref)[0:1, :]
    z_prev = jnp.where(i == 0, 0.0, z_prev)
    z_next = jnp.where(i == n_i - 1, 0.0, z_next)
    row = lax.broadcasted_iota(jnp.int32, (ts, D_CONV), 0)
    z_up = jnp.where(row == 0, z_prev, pltpu.roll(z, 1, axis=0))
    z_dn = jnp.where(row == ts - 1, z_next, pltpu.roll(z, ts - 1, axis=0))
    cw = conv_w_ref[...]
    y = z_up * cw[0:1, :] + z * cw[1:2, :] + z_dn * cw[2:3, :]
    out_a = jnp.dot((proj(C_BG, D_CONV) * y).astype(jnp.bfloat16), w_co_ref[...],
                    preferred_element_type=jnp.float32)
    a_ref[...] = jax.nn.sigmoid(proj(C_GA, D_MODEL)) * out_a
    sgb_ref[...] = jax.nn.sigmoid(proj(C_GB, D_MODEL))

    cos = cos_ref[...]
    sin_a = sa_ref[...]
    sin_b = sb_ref[...]
    cq = _rms_rows(proj(C_CQ, Q_LORA), gql_ref[...], Q_LORA).astype(jnp.bfloat16)
    qf = jnp.dot(cq, w_uq_ref[...], preferred_element_type=jnp.float32)
    ckv = _rms_rows(proj(C_CKV, KV_LORA), gkvl_ref[...], KV_LORA).astype(jnp.bfloat16)
    kf = jnp.dot(ckv, w_uk_ref[...], preferred_element_type=jnp.float32)
    lane = lax.broadcasted_iota(jnp.int32, (1, N_HEADS * HEAD_SLOT), 1)
    ones_lane = (jnp.bitwise_and(lane, HEAD_SLOT - 1) == V_HEAD).astype(jnp.float32)
    v_ref[...] = (jnp.dot(ckv, w_uv_ref[...], preferred_element_type=jnp.float32)
                  + ones_lane).astype(jnp.bfloat16)
    kr = proj(C_KR, HEAD_SLOT)
    gqn = gqn_ref[...]
    gkn = gkn_ref[...]
    q_scale = QK_HEAD ** -0.5 * math.log2(math.e)
    for hd in range(N_HEADS):
        sl = slice(hd * HEAD_SLOT, (hd + 1) * HEAD_SLOT)
        qh = _rope(_rms_rows(qf[:, sl], gqn, QK_HEAD), cos, sin_a, sin_b)
        q_ref[:, sl] = (qh * q_scale).astype(jnp.bfloat16)
        kh = _rope(_rms_rows(kf[:, sl] + kr, gkn, QK_HEAD), cos, sin_a, sin_b)
        k_ref[:, sl] = kh.astype(jnp.bfloat16)


def _inproj(x, scale1, shift1, wts, rope, after):
    bsz, seq, _ = x.shape
    ts = min(TS_IN, seq)
    n_i = seq // ts
    nb8 = seq // SUBLANES
    per8 = ts // SUBLANES
    tile = lambda b, i: (b, i, 0)
    per_b = lambda b, i: (b, 0, 0)
    full2 = lambda b, i: (0, 0)
    hw = N_HEADS * HEAD_SLOT
    in_specs = [
        pl.BlockSpec((None, ts, D_MODEL), tile),
        pl.BlockSpec((None, SUBLANES, D_MODEL), lambda b, i: (b, jnp.maximum(i * per8 - 1, 0), 0)),
        pl.BlockSpec((None, SUBLANES, D_MODEL), lambda b, i: (b, jnp.minimum((i + 1) * per8, nb8 - 1), 0)),
        pl.BlockSpec((None, 1, D_MODEL), per_b),
        pl.BlockSpec((None, 1, D_MODEL), per_b),
        pl.BlockSpec((1, D_MODEL), full2),
        pl.BlockSpec((D_MODEL, IN_COLS_R), full2),
        pl.BlockSpec((3, D_CONV), full2),
        pl.BlockSpec((D_CONV, D_MODEL), full2),
        pl.BlockSpec((1, Q_LORA), full2),
        pl.BlockSpec((Q_LORA, hw), full2),
        pl.BlockSpec((1, KV_LORA), full2),
        pl.BlockSpec((KV_LORA, hw), full2),
        pl.BlockSpec((KV_LORA, hw), full2),
        pl.BlockSpec((1, HEAD_SLOT), full2),
        pl.BlockSpec((1, HEAD_SLOT), full2),
        pl.BlockSpec((ts, HEAD_SLOT), lambda b, i: (i, 0)),
        pl.BlockSpec((ts, HEAD_SLOT), lambda b, i: (i, 0)),
        pl.BlockSpec((ts, HEAD_SLOT), lambda b, i: (i, 0)),
    ] + [pl.BlockSpec(memory_space=pl.ANY)] * len(after)
    out_specs = [pl.BlockSpec((None, ts, hw), tile)] * 3 + [pl.BlockSpec((None, ts, D_MODEL), tile)] * 2
    out_shape = ([jax.ShapeDtypeStruct((bsz, seq, hw), jnp.bfloat16)] * 3
                 + [jax.ShapeDtypeStruct((bsz, seq, D_MODEL), jnp.float32)] * 2)
    return pl.pallas_call(
        _inproj_kernel,
        grid=(bsz, n_i),
        in_specs=in_specs,
        out_specs=out_specs,
        out_shape=out_shape,
        compiler_params=_cparams(("parallel", "parallel")),
        name="inproj",
    )(x, x, x, scale1, shift1, wts["g_norm1"], wts["w_in"], wts["conv_w"], wts["w_conv_out"],
      wts["g_q_lora"], wts["w_uq"], wts["g_kv_lora"], wts["w_uk"], wts["w_uv"],
      wts["g_qnorm"], wts["g_knorm"], rope[0], rope[1], rope[2], *after)


def _attn_kernel(q_ref, k_ref, v_ref, o_ref):
    q = q_ref[...]
    tq = q.shape[0]
    seq = k_ref.shape[0]
    tk = min(TK, seq)

    n_chunks = seq // tk
    unroll = KV_UNROLL if n_chunks % KV_UNROLL == 0 else 1

    def chunk(off, carry):
        m, acc = carry
        kc = k_ref[pl.ds(off, tk), :]
        vc = v_ref[pl.ds(off, tk), :]
        s = lax.dot_general(q, kc, (((1,), (1,)), ((), ())), preferred_element_type=jnp.float32)
        m_new = jnp.maximum(m, jnp.max(s, axis=-1, keepdims=True))
        p = jnp.exp2(s - m_new).astype(jnp.bfloat16)
        alpha = jnp.exp2(m - m_new)
        acc = alpha * acc + jnp.dot(p, vc, preferred_element_type=jnp.float32)
        return m_new, acc

    def body(j, carry):
        for u in range(unroll):
            carry = chunk(pl.multiple_of((j * unroll + u) * tk, tk), carry)
        return carry

    m0 = jnp.full((tq, 1), _NEG_INF, jnp.float32)
    acc0 = jnp.zeros((tq, HEAD_SLOT), jnp.float32)
    _, acc = lax.fori_loop(0, n_chunks // unroll, body, (m0, acc0))
    o_ref[...] = (acc / acc[:, V_HEAD:V_HEAD + 1]).astype(jnp.bfloat16)


def _attention(q, k, v):
    bsz, seq, hw = q.shape
    tq = min(TQ, seq)
    return pl.pallas_call(
        _attn_kernel,
        grid=(bsz, N_HEADS, seq // tq),
        in_specs=[pl.BlockSpec((None, tq, HEAD_SLOT), lambda b, h, i: (b, i, h)),
                  pl.BlockSpec((None, seq, HEAD_SLOT), lambda b, h, i: (b, 0, h)),
                  pl.BlockSpec((None, seq, HEAD_SLOT), lambda b, h, i: (b, 0, h))],
        out_specs=pl.BlockSpec((None, tq, HEAD_SLOT), lambda b, h, i: (b, i, h)),
        out_shape=jax.ShapeDtypeStruct((bsz, seq, hw), jnp.bfloat16),
        compiler_params=_cparams(("parallel", "parallel", "arbitrary")),
        name="attn",
    )(q, k, v)


def _topk_rows(s, iota):
    vals, idxs = [], []
    big = jnp.float32(1e9)
    for _ in range(PEER_TOPK):
        m = jnp.max(s, axis=0, keepdims=True)
        am = jnp.min(jnp.where(s == m, iota, big), axis=0, keepdims=True)
        vals.append(m)
        idxs.append(am)
        s = jnp.where(iota == am, _NEG_INF, s)
    return jnp.concatenate(vals, axis=0), jnp.concatenate(idxs, axis=0)


def _take_rows(tab, idx):
    out = jnp.zeros_like(tab)
    for a in range(tab.shape[0]):
        out = out + jnp.where(idx == jnp.float32(a), tab[a:a + 1, :], 0.0)
    return out


_CAND = [(a, b) for a in range(PEER_TOPK) for b in range(PEER_TOPK) if (a + 1) * (b + 1) <= PEER_TOPK]
N_CAND = -(-len(_CAND) // SUBLANES) * SUBLANES
_PAD_LABEL = float(PEER_TOPK * PEER_TOPK)


def _candidate_tables():
    sel = np.zeros((2, N_CAND, PEER_TOPK), np.float32)
    lab = np.full((N_CAND, 1), _PAD_LABEL, np.float32)
    for r, (a, b) in enumerate(_CAND):
        sel[0, r, a] = 1.0
        sel[1, r, b] = 1.0
        lab[r, 0] = a * PEER_TOPK + b
    return sel, lab


def _mix_kernel(x_ref, a_ref, sgb_ref, o_ref, gate1_ref, scale2_ref, shift2_ref, g2_ref,
                w_ao_ref, w_out_ref, w_pq_ref, kcat_ref, sel_ref, lab_ref,
                x1_ref, h2_ref, e_ref, g_ref):
    ts = x_ref.shape[0]
    out_b = jnp.dot(o_ref[...], w_ao_ref[...], preferred_element_type=jnp.float32)
    merged = (a_ref[...] + sgb_ref[...] * out_b).astype(jnp.bfloat16)
    x1 = x_ref[...] + gate1_ref[...] * jnp.dot(merged, w_out_ref[...],
                                              preferred_element_type=jnp.float32)
    x1_ref[...] = x1
    h2 = _adaln(x1, g2_ref[...], scale2_ref[...], shift2_ref[...])
    h2_ref[...] = h2
    qp = jnp.dot(h2.astype(jnp.bfloat16), w_pq_ref[...],
                 preferred_element_type=jnp.float32).astype(jnp.bfloat16)

    iota_k = lax.broadcasted_iota(jnp.int32, (N_KEYS, ts), 0).astype(jnp.float32)
    lab = lab_ref[...]
    valid = lab < _PAD_LABEL
    sel_a = sel_ref[0]
    sel_b = sel_ref[1]
    exact = dict(preferred_element_type=jnp.float32, precision=lax.Precision.HIGHEST)
    e_rows, g_rows = [], []
    for hd in range(PEER_HEADS):
        qh = qp[:, hd * PEER_DK:(hd + 1) * PEER_DK]
        st = lax.dot_general(kcat_ref[hd], qh, (((1,), (1,)), ((), ())),
                             preferred_element_type=jnp.float32)
        v1, i1 = _topk_rows(st[:N_KEYS], iota_k)
        v2, i2 = _topk_rows(st[N_KEYS:], iota_k)
        comb = jnp.dot(sel_a, v1, **exact) + jnp.dot(sel_b, v2, **exact)
        sc, pos = _topk_rows(jnp.where(valid, comb, _NEG_INF), lab)
        ia = jnp.floor(pos * (1.0 / PEER_TOPK))
        ib = pos - ia * PEER_TOPK
        e_rows.append(_take_rows(i1, ia) * N_KEYS + _take_rows(i2, ib))
        p = jnp.exp(sc - sc[0:1, :])
        g_rows.append(p / jnp.sum(p, axis=0, keepdims=True))
    e_all = jnp.concatenate(e_rows, axis=0)
    g_all = jnp.concatenate(g_rows, axis=0)
    for c in range(ts // LANES):
        cs = slice(c * LANES, (c + 1) * LANES)
        e_ref[cs, :] = (e_all[:, cs].T * ROW_SUB).astype(jnp.int32)
        g_ref[cs, :] = g_all[:, cs].T


def _mix(x, a, sgb, o, gate1, scale2, shift2, wts):
    bsz, seq, _ = x.shape
    ts = min(TS_MIX, seq)
    tile = lambda b, i: (b, i, 0)
    per_b = lambda b, i: (b, 0, 0)
    full2 = lambda b, i: (0, 0)
    hw = N_HEADS * HEAD_SLOT
    big = pl.BlockSpec((None, ts, D_MODEL), tile)
    vec = pl.BlockSpec((None, 1, D_MODEL), per_b)
    picks = pl.BlockSpec((None, ts, N_PICKS), tile)
    sel, lab = _candidate_tables()
    return pl.pallas_call(
        _mix_kernel,
        grid=(bsz, seq // ts),
        in_specs=[big, big, big, pl.BlockSpec((None, ts, hw), tile), vec, vec, vec,
                  pl.BlockSpec((1, D_MODEL), full2),
                  pl.BlockSpec((hw, D_MODEL), full2),
                  pl.BlockSpec((D_MODEL, D_MODEL), full2),
                  pl.BlockSpec((D_MODEL, PEER_HEADS * PEER_DK), full2),
                  pl.BlockSpec((PEER_HEADS, 2 * N_KEYS, PEER_DK), lambda b, i: (0, 0, 0)),
                  pl.BlockSpec((2, N_CAND, PEER_TOPK), lambda b, i: (0, 0, 0)),
                  pl.BlockSpec((N_CAND, ts), full2)],
        out_specs=[big, big, picks, picks],
        out_shape=[jax.ShapeDtypeStruct((bsz, seq, D_MODEL), jnp.float32),
                   jax.ShapeDtypeStruct((bsz, seq, D_MODEL), jnp.float32),
                   jax.ShapeDtypeStruct((bsz, seq, N_PICKS), jnp.int32),
                   jax.ShapeDtypeStruct((bsz, seq, N_PICKS), jnp.float32)],
        compiler_params=_cparams(("parallel", "parallel")),
        name="mix",
    )(x, a, sgb, o, gate1, scale2, shift2, wts["g_norm2"], wts["w_attn_out"], wts["w_out"],
      wts["peer_wq"], wts["kcat"], jnp.asarray(sel), jnp.broadcast_to(jnp.asarray(lab), (N_CAND, ts)))


def _unpack_row(w):
    lo = lax.bitcast_convert_type(lax.shift_left(w, 16), jnp.float32)
    hi = lax.bitcast_convert_type(jnp.bitwise_and(w, jnp.int32(-65536)), jnp.float32)
    return lo, hi


def _gelu_tanh(x):
    c = math.sqrt(2.0 / math.pi)
    return 0.5 * x * (1.0 + jnp.tanh(c * (x + 0.044715 * (x * x * x))))


def _gather_rows(e_ref, t, tab_ref, pb):
    for j in range(N_PICKS // IDX_GROUP):
        row = e_ref.at[pl.ds(pl.multiple_of(t * N_PICKS + j * IDX_GROUP, IDX_GROUP), IDX_GROUP)]
        for c in range(IDX_GROUP):
            k = j * IDX_GROUP + c
            r = pl.multiple_of(row[c], ROW_SUB)
            pb[k * ROW_SUB:(k + 1) * ROW_SUB, :] = tab_ref[pl.ds(r, ROW_SUB), :]


def _pick_block(pb, s):
    return pb[pl.ds(s, N_PICKS, stride=ROW_SUB), :]


def _peer_u_kernel(e_ref, x_ref, g_ref, tab_ref, w_ref, pbuf0, pbuf1, abuf):
    tt = g_ref.shape[0]
    pbufs = (pbuf0, pbuf1)
    lane = lax.broadcasted_iota(jnp.int32, (N_PICKS, tt), 1)
    abuf[...] = jnp.zeros_like(abuf)

    def tok(i, carry):
        for u in range(TOK_UNROLL_U):
            t = i * TOK_UNROLL_U + u
            pb = pbufs[u % 2]
            x_blk = x_ref[pl.ds(pl.multiple_of(t * SUBLANES, SUBLANES), SUBLANES), :]
            _gather_rows(e_ref, t, tab_ref, pb)
            acc = None
            for s in range(ROW_SUB):
                lo, hi = _unpack_row(_pick_block(pb, s))
                term = lo * x_blk[s:s + 1, :] + hi * x_blk[ROW_SUB + s:ROW_SUB + s + 1, :]
                acc = term if acc is None else acc + term
            a_col = jnp.sum(acc, axis=1, keepdims=True)
            abuf[...] = jnp.where(lane == t, a_col, abuf[...])
        return carry

    lax.fori_loop(0, tt // TOK_UNROLL_U, tok, 0)
    w_ref[...] = g_ref[...] * _gelu_tanh(abuf[...].T)


def _peer_u(e, h2_rows, g, tab, n):
    tt = TT_PEER
    return pl.pallas_call(
        _peer_u_kernel,
        grid=(n // tt,),
        in_specs=[pl.BlockSpec((tt * N_PICKS,), lambda i: (i,), memory_space=pltpu.SMEM),
                  pl.BlockSpec((tt * SUBLANES, LANES), lambda i: (i, 0)),
                  pl.BlockSpec((tt, N_PICKS), lambda i: (i, 0)),
                  pl.BlockSpec(memory_space=pltpu.VMEM)],
        out_specs=pl.BlockSpec((tt, N_PICKS), lambda i: (i, 0)),
        out_shape=jax.ShapeDtypeStruct((n, N_PICKS), jnp.float32),
        scratch_shapes=[pltpu.VMEM((N_PICKS * ROW_SUB, LANES), jnp.int32),
                        pltpu.VMEM((N_PICKS * ROW_SUB, LANES), jnp.int32),
                        pltpu.VMEM((N_PICKS, tt), jnp.float32)],
        compiler_params=_cparams(("arbitrary",)),
        name="peer_u",
    )(e, h2_rows, g, tab)


def _peer_v_kernel(e_ref, w_ref, x_ref, gate_ref, tab_ref, y_ref, done_ref,
                   pbuf0, pbuf1, whi_ref, wlo_ref, acc_ref):
    tt = x_ref.shape[0]
    pbufs = (pbuf0, pbuf1)
    w = w_ref[...]
    w_hi = w.astype(jnp.bfloat16)
    whi_ref[...] = w_hi
    wlo_ref[...] = (w - w_hi.astype(jnp.float32)).astype(jnp.bfloat16)
    sub = lax.broadcasted_iota(jnp.int32, (BF16_ROWS, D_MODEL), 0)
    acc_ref[...] = jnp.zeros_like(acc_ref)

    def process(pb, t):
        b0 = pl.multiple_of(lax.shift_left(lax.shift_right_logical(t, 4), 4), BF16_ROWS)
        lhs = jnp.concatenate([whi_ref[pl.ds(b0, BF16_ROWS), :], wlo_ref[pl.ds(b0, BF16_ROWS), :]], axis=0)
        los, his = [], []
        for s in range(ROW_SUB):
            lo, hi = _unpack_row(_pick_block(pb, s))
            los.append(lo.astype(jnp.bfloat16))
            his.append(hi.astype(jnp.bfloat16))
        rhs = jnp.concatenate(los + his, axis=1)
        res = jnp.dot(lhs, rhs, preferred_element_type=jnp.float32)
        r = res[:BF16_ROWS, :] + res[BF16_ROWS:, :]
        mask = sub == jnp.bitwise_and(t, BF16_ROWS - 1)
        acc_ref[pl.ds(b0, BF16_ROWS), :] = jnp.where(mask, r, acc_ref[pl.ds(b0, BF16_ROWS), :])

    def step(i, carry):
        t0 = i * TOK_UNROLL_V
        for u in range(TOK_UNROLL_V):
            _gather_rows(e_ref, t0 + u, tab_ref, pbufs[u % 2])
            process(pbufs[u % 2], t0 + u)
        return carry

    lax.fori_loop(0, tt // TOK_UNROLL_V, step, 0)
    y_ref[...] = x_ref[...] + gate_ref[...] * acc_ref[...]
    done_ref[...] = jnp.zeros_like(done_ref)


def _peer_v(e, w, x1, gate2, tab, seq, n_tc):
    n = x1.shape[0]
    tt = TT_PEER
    return pl.pallas_call(
        _peer_v_kernel,
        grid=(n_tc // tt,),
        in_specs=[pl.BlockSpec((tt * N_PICKS,), lambda i: (i,), memory_space=pltpu.SMEM),
                  pl.BlockSpec((tt, N_PICKS), lambda i: (i, 0)),
                  pl.BlockSpec((tt, D_MODEL), lambda i: (i, 0)),
                  pl.BlockSpec((None, 1, D_MODEL), lambda i: ((i * tt) // seq, 0, 0)),
                  pl.BlockSpec(memory_space=pltpu.VMEM)],
        out_specs=[pl.BlockSpec((tt, D_MODEL), lambda i: (i, 0)),
                   pl.BlockSpec((SUBLANES, LANES), lambda i: (0, 0))],
        out_shape=[jax.ShapeDtypeStruct((n, D_MODEL), jnp.float32),
                   jax.ShapeDtypeStruct((SUBLANES, LANES), jnp.float32)],
        scratch_shapes=[pltpu.VMEM((N_PICKS * ROW_SUB, LANES), jnp.int32),
                        pltpu.VMEM((N_PICKS * ROW_SUB, LANES), jnp.int32),
                        pltpu.VMEM((tt, N_PICKS), jnp.bfloat16),
                        pltpu.VMEM((tt, N_PICKS), jnp.bfloat16),
                        pltpu.VMEM((tt, D_MODEL), jnp.float32)],
        compiler_params=_cparams(("arbitrary",)),
        name="peer_v",
    )(e, w, x1, gate2, tab)


SC_CORES = 2
SC_SUBCORES = 16
SC_LANES = 16
SC_WORKERS = SC_CORES * SC_SUBCORES
SC_HALF = N_PICKS // 2
SC_CHUNKS = ROW_WORDS // SC_LANES
SC_ACC_CHUNKS = 8
SC_SHARE_FIRST = (1, 1)
SC_SHARE_SECOND = (0, 1)
SC_ALIGN = 1024


def _sc_params():
    cp = pltpu.CompilerParams()
    if "needs_layout_passes" in pltpu.CompilerParams.__dataclass_fields__:
        cp = dataclasses.replace(cp, needs_layout_passes=False)
    return cp


def _gelu_tanh_via_exp(x):
    c = math.sqrt(2.0 / math.pi)
    z = c * (x + 0.044715 * (x * x * x))
    t = jnp.exp(-2.0 * jnp.abs(z))
    return 0.5 * x * (1.0 + jnp.sign(z) * (1.0 - t) / (1.0 + t))


def _peer_sc(e2, g, h2, tab_u, tab_v, off):
    n_sc = e2.shape[0]
    per_w = n_sc // SC_WORKERS
    mesh = plsc.VectorSubcoreMesh(core_axis_name="c", subcore_axis_name="s")
    ln = SC_LANES
    per_token_inputs = lambda: [pltpu.VMEM((SC_HALF,), jnp.int32), pltpu.VMEM((SC_HALF,), jnp.int32),
                                pltpu.VMEM((D_MODEL,), jnp.float32), pltpu.VMEM((N_PICKS,), jnp.float32)]

    @functools.partial(
        pl.kernel, mesh=mesh,
        out_type=jax.ShapeDtypeStruct((n_sc, D_MODEL), jnp.float32),
        scratch_types=per_token_inputs() + per_token_inputs() + [
            pltpu.VMEM((SC_HALF, ROW_WORDS), jnp.int32), pltpu.VMEM((SC_HALF, ROW_WORDS), jnp.int32),
            pltpu.VMEM((N_PICKS,), jnp.float32), pltpu.VMEM((D_MODEL,), jnp.float32),
            pltpu.SemaphoreType.DMA, pltpu.SemaphoreType.DMA,
            pltpu.SemaphoreType.DMA, pltpu.SemaphoreType.DMA],
        compiler_params=_sc_params(),
        cost_estimate=pl.CostEstimate(
            flops=4 * n_sc * N_PICKS * D_MODEL,
            transcendentals=n_sc * N_PICKS,
            bytes_accessed=n_sc * (2 * N_PICKS * ROW_WORDS * 4 + 3 * D_MODEL * 4 + 3 * N_PICKS * 4)),
    )
    def body(e_hbm, g_hbm, x_hbm, tu_hbm, tv_hbm, out_hbm,
             ia0, ib0, x0, g0, ia1, ib1, x1, g1, buf_a, buf_b, w_v, o_v, sem_a, sem_b, sem_in0, sem_in1):
        wid = lax.axis_index("s") * SC_CORES + lax.axis_index("c")
        base = wid * per_w
        lane = lax.iota(jnp.int32, ln)
        zero = jnp.zeros((ln,), jnp.float32)
        in_sets = ((ia0, ib0, x0, g0, sem_in0), (ia1, ib1, x1, g1, sem_in1))

        def input_copies(st, i):
            ia, ib, xv, gv, sem = st
            tl = base + jnp.minimum(i, per_w - 1)
            return [pltpu.make_async_copy(e_hbm.at[tl, 0], ia, sem),
                    pltpu.make_async_copy(e_hbm.at[tl, 1], ib, sem),
                    pltpu.make_async_copy(x_hbm.at[off + tl], xv, sem),
                    pltpu.make_async_copy(g_hbm.at[off + tl], gv, sem)]

        def u_half(buf, k0, x_v, g_v):
            @pl.loop(0, SC_HALF // ln)
            def _(grp):
                def chunk(j, accs):
                    xl = x_v[pl.ds(j * ln, ln)]
                    xh = x_v[pl.ds(ROW_WORDS + j * ln, ln)]
                    new = []
                    for kk in range(ln):
                        lo, hi = _unpack_row(buf[grp * ln + kk, pl.ds(j * ln, ln)])
                        new.append(accs[kk] + lo * xl + hi * xh)
                    return tuple(new)

                accs = lax.fori_loop(0, SC_CHUNKS, chunk, tuple(zero for _ in range(ln)))
                a = zero
                for kk in range(ln):
                    a = jnp.where(lane == kk, jnp.sum(accs[kk]), a)
                sl = pl.ds(k0 + grp * ln, ln)
                w_v[sl] = g_v[sl] * _gelu_tanh_via_exp(a)

        def v_half(buf, k0):
            @pl.loop(0, SC_CHUNKS // SC_ACC_CHUNKS)
            def _(jb):
                def pick(kq, accs):
                    wk = plsc.load_gather(w_v, [jnp.full((ln,), k0, jnp.int32) + kq])
                    new = []
                    for c in range(SC_ACC_CHUNKS):
                        lo, hi = _unpack_row(buf[kq, pl.ds((jb * SC_ACC_CHUNKS + c) * ln, ln)])
                        new.append(accs[2 * c] + wk * lo)
                        new.append(accs[2 * c + 1] + wk * hi)
                    return tuple(new)

                accs = lax.fori_loop(0, SC_HALF, pick, tuple(zero for _ in range(2 * SC_ACC_CHUNKS)))
                for c in range(SC_ACC_CHUNKS):
                    col = (jb * SC_ACC_CHUNKS + c) * ln
                    o_v[pl.ds(col, ln)] = o_v[pl.ds(col, ln)] + accs[2 * c]
                    o_v[pl.ds(ROW_WORDS + col, ln)] = o_v[pl.ds(ROW_WORDS + col, ln)] + accs[2 * c + 1]

        def token(i, cur, nxt):
            ia, ib, x_v, g_v, _ = cur
            for c in input_copies(nxt, i + 1):
                c.start()
            u_b = pltpu.async_copy(tu_hbm.at[ib], buf_b, sem_b)

            @pl.loop(0, D_MODEL // ln)
            def _(c):
                o_v[pl.ds(c * ln, ln)] = zero

            pltpu.make_async_copy(tu_hbm.at[ia], buf_a, sem_a).wait()
            u_half(buf_a, 0, x_v, g_v)
            v_a = pltpu.async_copy(tv_hbm.at[ia], buf_a, sem_a)
            u_b.wait()
            u_half(buf_b, SC_HALF, x_v, g_v)
            v_b = pltpu.async_copy(tv_hbm.at[ib], buf_b, sem_b)
            v_a.wait()
            v_half(buf_a, 0)
            for c in input_copies(nxt, i + 1):
                c.wait()
            pltpu.async_copy(tu_hbm.at[nxt[0]], buf_a, sem_a)
            v_b.wait()
            v_half(buf_b, SC_HALF)
            pltpu.sync_copy(o_v, out_hbm.at[base + i])

        for c in input_copies(in_sets[0], 0):
            c.start()
        for c in input_copies(in_sets[0], 0):
            c.wait()
        pltpu.async_copy(tu_hbm.at[ia0], buf_a, sem_a)

        @pl.loop(0, per_w // 2)
        def _(pair):
            token(2 * pair, in_sets[0], in_sets[1])
            token(2 * pair + 1, in_sets[1], in_sets[0])

        pltpu.make_async_copy(tu_hbm.at[ia0], buf_a, sem_a).wait()

    return body(e2, g, h2, tab_u, tab_v)


def _finish_kernel(*refs):
    x_ref, gate_ref, p_ref, y_ref = refs[-4:]
    y_ref[...] = x_ref[...] + gate_ref[...] * p_ref[...]


def _peer_finish(y, x1, gate2, peer_tail, seq, n_tc):
    n = x1.shape[0]
    tt = TT_PEER
    first = n_tc // tt
    row = lambda i: (first + i, 0)
    in_specs = [pl.BlockSpec((tt, D_MODEL), row),
                pl.BlockSpec((None, 1, D_MODEL), lambda i: (((first + i) * tt) // seq, 0, 0)),
                pl.BlockSpec((tt, D_MODEL), lambda i: (i, 0))]
    args = (x1, gate2, peer_tail)
    aliases = {}
    if n_tc:
        in_specs = [pl.BlockSpec(memory_space=pl.ANY)] + in_specs
        args = (y,) + args
        aliases = {0: 0}
    return pl.pallas_call(
        _finish_kernel,
        grid=((n - n_tc) // tt,),
        in_specs=in_specs,
        out_specs=pl.BlockSpec((tt, D_MODEL), row),
        out_shape=jax.ShapeDtypeStruct((n, D_MODEL), jnp.float32),
        input_output_aliases=aliases,
        compiler_params=_cparams(("arbitrary",)),
        name="peer_finish",
    )(*args)


def _pad_heads(w, used, lead=0):
    r = w.shape[0]
    w3 = w.reshape(r, N_HEADS, used)
    out = jnp.zeros((r, N_HEADS, HEAD_SLOT), w.dtype).at[:, :, lead:lead + used].set(w3)
    return out.reshape(r, N_HEADS * HEAD_SLOT)


def _pack_table(t):
    tb = t.astype(jnp.bfloat16)
    lo = lax.bitcast_convert_type(tb[:, :ROW_WORDS], jnp.uint16).astype(jnp.uint32)
    hi = lax.bitcast_convert_type(tb[:, ROW_WORDS:], jnp.uint16).astype(jnp.uint32)
    words = lax.bitcast_convert_type(lo | (hi << 16), jnp.int32)
    return words.reshape(t.shape[0] * ROW_SUB, LANES)


def _prep_weights(g_norm1, w_in, conv_w, w_conv_out, g_q_lora, w_uq, g_kv_lora, w_ukv, g_qnorm,
                  g_knorm, w_attn_out, w_out, g_norm2, peer_wq, peer_k1, peer_k2, peer_u, peer_v):
    bf = jnp.bfloat16
    o1 = 3 * D_CONV
    o2 = o1 + Q_LORA
    o3 = o2 + KV_LORA
    o4 = o3 + QK_ROPE
    o5 = o4 + D_MODEL
    kr_slot = jnp.zeros((D_MODEL, HEAD_SLOT), w_in.dtype).at[:, QK_NOPE:QK_HEAD].set(w_in[:, o3:o4])
    w_in_r = jnp.concatenate([w_in[:, :o3], kr_slot, w_in[:, o4:]], axis=1).astype(bf)
    w_ukv3 = w_ukv.reshape(KV_LORA, N_HEADS, QK_NOPE + V_HEAD)
    w_uk = _pad_heads(w_ukv3[:, :, :QK_NOPE].reshape(KV_LORA, -1), QK_NOPE)
    w_uv = _pad_heads(w_ukv3[:, :, QK_NOPE:].reshape(KV_LORA, -1), V_HEAD)
    pad_gain = lambda g: jnp.zeros((1, HEAD_SLOT), jnp.float32).at[0, :QK_HEAD].set(g)
    w_ao = jnp.zeros((N_HEADS, HEAD_SLOT, D_MODEL), w_attn_out.dtype).at[:, :V_HEAD].set(
        w_attn_out.reshape(N_HEADS, V_HEAD, D_MODEL)).reshape(N_HEADS * HEAD_SLOT, D_MODEL)
    half = PEER_DK // 2
    kcat = jnp.zeros((PEER_HEADS, 2 * N_KEYS, PEER_DK), jnp.float32)
    kcat = kcat.at[:, :N_KEYS, :half].set(peer_k1).at[:, N_KEYS:, half:].set(peer_k2)
    tab_u = _pack_table(peer_u)
    tab_v = _pack_table(peer_v)
    return {
        "g_norm1": g_norm1.reshape(1, -1), "w_in": w_in_r, "conv_w": conv_w,
        "w_conv_out": w_conv_out.astype(bf), "g_q_lora": g_q_lora.reshape(1, -1),
        "w_uq": _pad_heads(w_uq, QK_HEAD).astype(bf), "g_kv_lora": g_kv_lora.reshape(1, -1),
        "w_uk": w_uk.astype(bf), "w_uv": w_uv.astype(bf),
        "g_qnorm": pad_gain(g_qnorm), "g_knorm": pad_gain(g_knorm),
        "w_attn_out": w_ao.astype(bf), "w_out": w_out.astype(bf), "g_norm2": g_norm2.reshape(1, -1),
        "peer_wq": peer_wq.astype(bf), "kcat": kcat.astype(bf),
        "tab_u": tab_u, "tab_v": tab_v,
        "tab_u_sc": tab_u.reshape(-1, ROW_WORDS), "tab_v_sc": tab_v.reshape(-1, ROW_WORDS),
    }


def _rope_tables(seq):
    pos = jnp.arange(seq, dtype=jnp.float32)
    inv = ROPE_THETA ** (-jnp.arange(0, QK_ROPE, 2, dtype=jnp.float32) / QK_ROPE)
    ang = pos[:, None] * inv[None, :]
    cos, sin = jnp.cos(ang), jnp.sin(ang)
    t1 = slice(QK_NOPE, QK_NOPE + HALF_ROPE)
    t2 = slice(QK_NOPE + HALF_ROPE, QK_HEAD)
    cos_t = jnp.ones((seq, HEAD_SLOT), jnp.float32).at[:, t1].set(cos).at[:, t2].set(cos)
    sin_a = jnp.zeros((seq, HEAD_SLOT), jnp.float32).at[:, t1].set(-sin)
    sin_b = jnp.zeros((seq, HEAD_SLOT), jnp.float32).at[:, t2].set(sin)
    return cos_t, sin_a, sin_b


def _encoder_layer(x, c, w_ada, b_ada, wts, sc_share, after):
    bsz, seq, d = x.shape
    n = bsz * seq
    ada = _ada(c, w_ada, b_ada)[:, None, :]
    shift1, scale1, gate1, shift2, scale2, gate2 = jnp.split(ada, 6, axis=-1)
    q, k, v, a, sgb = _inproj(x, scale1, shift1, wts, _rope_tables(seq), after)
    o = _attention(q, k, v)
    x1, h2, e, g = _mix(x, a, sgb, o, gate1, scale2, shift2, wts)
    n_sc = (n * sc_share[0] // sc_share[1]) // SC_ALIGN * SC_ALIGN
    n_tc = n - n_sc
    e = e.reshape(n, N_PICKS)
    g = g.reshape(n, N_PICKS)
    h2 = h2.reshape(n, d)
    x1 = x1.reshape(n, d)
    if n_sc:
        ids = lax.shift_right_logical(e[n_tc:], ROW_SUB.bit_length() - 1)
        peer_tail = _peer_sc(ids.reshape(n_sc, 2, SC_HALF), g, h2,
                             wts["tab_u_sc"], wts["tab_v_sc"], n_tc)
    if n_tc:
        e = e.reshape(n * N_PICKS)
        w = _peer_u(e, h2.reshape(n * SUBLANES, LANES), g, wts["tab_u"], n_tc)
        y, done = _peer_v(e, w, x1, gate2, wts["tab_v"], seq, n_tc)
    else:
        y, done = None, g
    if n_sc:
        y = _peer_finish(y, x1, gate2, peer_tail, seq, n_tc)
    return y.reshape(bsz, seq, d), done


def kernel(x_prompt, x_sample, c_prompt, c_sample, w_ada, b_ada, g_norm1, w_in, conv_w, w_conv_out, g_q_lora, w_uq, g_kv_lora, w_ukv, g_qnorm, g_knorm, w_attn_out, w_out, g_norm2, peer_wq, peer_k1, peer_k2, peer_u, peer_v):
    wts = _prep_weights(g_norm1[0], w_in[0], conv_w[0], w_conv_out[0], g_q_lora[0], w_uq[0],
                        g_kv_lora[0], w_ukv[0], g_qnorm[0], g_knorm[0], w_attn_out[0], w_out[0],
                        g_norm2[0], peer_wq[0], peer_k1[0], peer_k2[0], peer_u[0], peer_v[0])
    sc_tables = (wts["tab_u_sc"], wts["tab_v_sc"])
    y_sample, sample_done = _encoder_layer(x_sample, c_sample, w_ada[0], b_ada[0], wts,
                                           SC_SHARE_FIRST, sc_tables)
    y_prompt, _ = _encoder_layer(x_prompt, c_prompt, w_ada[0], b_ada[0], wts,
                                 SC_SHARE_SECOND, sc_tables + (sample_done,))
    return y_prompt, y_sample
```

```python
import dataclasses
import functools
import math

import jax
import jax.numpy as jnp
import numpy as np
from jax import lax
from jax.experimental import pallas as pl
from jax.experimental.pallas import tpu as pltpu
from jax.experimental.pallas import tpu_sc as plsc

D_MODEL = 1024
D_CONV = 512
N_HEADS = 8
QK_NOPE = 64
QK_ROPE = 32
V_HEAD = 64
Q_LORA = 256
KV_LORA = 128
QK_HEAD = QK_NOPE + QK_ROPE
ROPE_THETA = 10000.0
PEER_HEADS = 8
N_KEYS = 128
PEER_DK = 128
PEER_TOPK = 16
EPS = 1e-6

LANES = 128
SUBLANES = 8
HEAD_SLOT = LANES
HALF_ROPE = QK_ROPE // 2
N_PICKS = PEER_HEADS * PEER_TOPK
ROW_WORDS = D_MODEL // 2
ROW_SUB = ROW_WORDS // LANES
VMEM_LIMIT = 56 * 1024 * 1024

C_BG, C_CG, C_HC = 0, 512, 1024
C_CQ = 1536
C_CKV = C_CQ + Q_LORA
C_KR = C_CKV + KV_LORA
C_GA = C_KR + HEAD_SLOT
C_GB = C_GA + D_MODEL
IN_COLS_R = C_GB + D_MODEL

TS_IN = 512
TQ = 512
TK = 2048
KV_UNROLL = 2
TS_MIX = 256
TT_PEER = 128
TOK_UNROLL_U = 16
TOK_UNROLL_V = 16
BF16_ROWS = 16
IDX_GROUP = 8

_NEG_INF = float("-inf")


def _cparams(sem):
    return pltpu.CompilerParams(dimension_semantics=sem, vmem_limit_bytes=VMEM_LIMIT)


def _ada_kernel(c_ref, w_ref, b_ref, o_ref):
    o_ref[...] = jnp.dot(c_ref[...], w_ref[...], preferred_element_type=jnp.float32,
                         precision=lax.Precision.HIGHEST) + b_ref[...]


def _ada(c, w_ada, b_ada):
    bsz = c.shape[0]
    rows = -(-bsz // SUBLANES) * SUBLANES
    c_p = jnp.zeros((rows, D_MODEL), jnp.float32).at[:bsz].set(c)
    out = pl.pallas_call(
        _ada_kernel,
        grid=(6,),
        in_specs=[pl.BlockSpec((rows, D_MODEL), lambda j: (0, 0)),
                  pl.BlockSpec((D_MODEL, D_MODEL), lambda j: (0, j)),
                  pl.BlockSpec((1, D_MODEL), lambda j: (0, j))],
        out_specs=pl.BlockSpec((rows, D_MODEL), lambda j: (0, j)),
        out_shape=jax.ShapeDtypeStruct((rows, 6 * D_MODEL), jnp.float32),
        compiler_params=_cparams(("arbitrary",)),
        name="ada",
    )(c_p, w_ada, b_ada.reshape(1, -1))
    return out[:bsz]


def _adaln(x, g, scale, shift):
    ms = jnp.mean(x * x, axis=-1, keepdims=True)
    return (x * lax.rsqrt(ms + EPS) * g) * (1.0 + scale) + shift


def _rms_rows(x, g, n):
    ms = jnp.sum(x * x, axis=-1, keepdims=True) * (1.0 / n)
    return x * lax.rsqrt(ms + EPS) * g


def _rope(x, cos, sin_a, sin_b):
    return (x * cos + pltpu.roll(x, LANES - HALF_ROPE, axis=1) * sin_a
            + pltpu.roll(x, HALF_ROPE, axis=1) * sin_b)


def _inproj_kernel(x_ref, xp_ref, xn_ref, scale_ref, shift_ref, g1_ref, w_in_ref, conv_w_ref,
                   w_co_ref, gql_ref, w_uq_ref, gkvl_ref, w_uk_ref, w_uv_ref, gqn_ref, gkn_ref,
                   cos_ref, sa_ref, sb_ref, *rest):
    q_ref, k_ref, v_ref, a_ref, sgb_ref = rest[-5:]
    i = pl.program_id(1)
    n_i = pl.num_programs(1)
    g1 = g1_ref[...]
    scale = scale_ref[...]
    shift = shift_ref[...]
    ts = x_ref.shape[0]

    h = _adaln(x_ref[...], g1, scale, shift).astype(jnp.bfloat16)

    def proj(lo, width):
        return jnp.dot(h, w_in_ref[:, lo:lo + width], preferred_element_type=jnp.float32)

    z = proj(C_CG, D_CONV) * proj(C_HC, D_CONV)

    def halo_z(xh_ref):
        hh = _adaln(xh_ref[...], g1, scale, shift).astype(jnp.bfloat16)
        zc = jnp.dot(hh, w_in_ref[:, C_CG:C_CG + 2 * D_CONV], preferred_element_type=jnp.float32)
        return zc[:, :D_CONV] * zc[:, D_CONV:]

    z_prev = halo_z(xp_ref)[SUBLANES - 1:SUBLANES, :]
    z_next = halo_z(xn_ref)[0:1, :]
    z_prev = jnp.where(i == 0, 0.0, z_prev)
    z_next = jnp.where(i == n_i - 1, 0.0, z_next)
    row = lax.broadcasted_iota(jnp.int32, (ts, D_CONV), 0)
    z_up = jnp.where(row == 0, z_prev, pltpu.roll(z, 1, axis=0))
    z_dn = jnp.where(row == ts - 1, z_next, pltpu.roll(z, ts - 1, axis=0))
    cw = conv_w_ref[...]
    y = z_up * cw[0:1, :] + z * cw[1:2, :] + z_dn * cw[2:3, :]
    out_a = jnp.dot((proj(C_BG, D_CONV) * y).astype(jnp.bfloat16), w_co_ref[...],
                    preferred_element_type=jnp.float32)
    a_ref[...] = jax.nn.sigmoid(proj(C_GA, D_MODEL)) * out_a
    sgb_ref[...] = jax.nn.sigmoid(proj(C_GB, D_MODEL))

    cos = cos_ref[...]
    sin_a = sa_ref[...]
    sin_b = sb_ref[...]
    cq = _rms_rows(proj(C_CQ, Q_LORA), gql_ref[...], Q_LORA).astype(jnp.bfloat16)
    qf = jnp.dot(cq, w_uq_ref[...], preferred_element_type=jnp.float32)
    ckv = _rms_rows(proj(C_CKV, KV_LORA), gkvl_ref[...], KV_LORA).astype(jnp.bfloat16)
    kf = jnp.dot(ckv, w_uk_ref[...], preferred_element_type=jnp.float32)
    lane = lax.broadcasted_iota(jnp.int32, (1, N_HEADS * HEAD_SLOT), 1)
    ones_lane = (jnp.bitwise_and(lane, HEAD_SLOT - 1) == V_HEAD).astype(jnp.float32)
    v_ref[...] = (jnp.dot(ckv, w_uv_ref[...], preferred_element_type=jnp.float32)
                  + ones_lane).astype(jnp.bfloat16)
    kr = proj(C_KR, HEAD_SLOT)
    gqn = gqn_ref[...]
    gkn = gkn_ref[...]
    q_scale = QK_HEAD ** -0.5 * math.log2(math.e)
    for hd in range(N_HEADS):
        sl = slice(hd * HEAD_SLOT, (hd + 1) * HEAD_SLOT)
        qh = _rope(_rms_rows(qf[:, sl], gqn, QK_HEAD), cos, sin_a, sin_b)
        q_ref[:, sl] = (qh * q_scale).astype(jnp.bfloat16)
        kh = _rope(_rms_rows(kf[:, sl] + kr, gkn, QK_HEAD), cos, sin_a, sin_b)
        k_ref[:, sl] = kh.astype(jnp.bfloat16)


def _inproj(x, scale1, shift1, wts, rope, after):
    bsz, seq, _ = x.shape
    ts = min(TS_IN, seq)
    n_i = seq // ts
    nb8 = seq // SUBLANES
    per8 = ts // SUBLANES
    tile = lambda b, i: (b, i, 0)
    per_b = lambda b, i: (b, 0, 0)
    full2 = lambda b, i: (0, 0)
    hw = N_HEADS * HEAD_SLOT
    in_specs = [
        pl.BlockSpec((None, ts, D_MODEL), tile),
        pl.BlockSpec((None, SUBLANES, D_MODEL), lambda b, i: (b, jnp.maximum(i * per8 - 1, 0), 0)),
        pl.BlockSpec((None, SUBLANES, D_MODEL), lambda b, i: (b, jnp.minimum((i + 1) * per8, nb8 - 1), 0)),
        pl.BlockSpec((None, 1, D_MODEL), per_b),
        pl.BlockSpec((None, 1, D_MODEL), per_b),
        pl.BlockSpec((1, D_MODEL), full2),
        pl.BlockSpec((D_MODEL, IN_COLS_R), full2),
        pl.BlockSpec((3, D_CONV), full2),
        pl.BlockSpec((D_CONV, D_MODEL), full2),
        pl.BlockSpec((1, Q_LORA), full2),
        pl.BlockSpec((Q_LORA, hw), full2),
        pl.BlockSpec((1, KV_LORA), full2),
        pl.BlockSpec((KV_LORA, hw), full2),
        pl.BlockSpec((KV_LORA, hw), full2),
        pl.BlockSpec((1, HEAD_SLOT), full2),
        pl.BlockSpec((1, HEAD_SLOT), full2),
        pl.BlockSpec((ts, HEAD_SLOT), lambda b, i: (i, 0)),
        pl.BlockSpec((ts, HEAD_SLOT), lambda b, i: (i, 0)),
        pl.BlockSpec((ts, HEAD_SLOT), lambda b, i: (i, 0)),
    ] + [pl.BlockSpec(memory_space=pl.ANY)] * len(after)
    out_specs = [pl.BlockSpec((None, ts, hw), tile)] * 3 + [pl.BlockSpec((None, ts, D_MODEL), tile)] * 2
    out_shape = ([jax.ShapeDtypeStruct((bsz, seq, hw), jnp.bfloat16)] * 3
                 + [jax.ShapeDtypeStruct((bsz, seq, D_MODEL), jnp.float32)] * 2)
    return pl.pallas_call(
        _inproj_kernel,
        grid=(bsz, n_i),
        in_specs=in_specs,
        out_specs=out_specs,
        out_shape=out_shape,
        compiler_params=_cparams(("parallel", "parallel")),
        name="inproj",
    )(x, x, x, scale1, shift1, wts["g_norm1"], wts["w_in"], wts["conv_w"], wts["w_conv_out"],
      wts["g_q_lora"], wts["w_uq"], wts["g_kv_lora"], wts["w_uk"], wts["w_uv"],
      wts["g_qnorm"], wts["g_knorm"], rope[0], rope[1], rope[2], *after)


def _attn_kernel(q_ref, k_ref, v_ref, o_ref):
    q = q_ref[...]
    tq = q.shape[0]
    seq = k_ref.shape[0]
    tk = min(TK, seq)

    n_chunks = seq // tk
    unroll = KV_UNROLL if n_chunks % KV_UNROLL == 0 else 1

    def chunk(off, carry):
        m, acc = carry
        kc = k_ref[pl.ds(off, tk), :]
        vc = v_ref[pl.ds(off, tk), :]
        s = lax.dot_general(q, kc, (((1,), (1,)), ((), ())), preferred_element_type=jnp.float32)
        m_new = jnp.maximum(m, jnp.max(s, axis=-1, keepdims=True))
        p = jnp.exp2(s - m_new).astype(jnp.bfloat16)
        alpha = jnp.exp2(m - m_new)
        acc = alpha * acc + jnp.dot(p, vc, preferred_element_type=jnp.float32)
        return m_new, acc

    def body(j, carry):
        for u in range(unroll):
            carry = chunk(pl.multiple_of((j * unroll + u) * tk, tk), carry)
        return carry

    m0 = jnp.full((tq, 1), _NEG_INF, jnp.float32)
    acc0 = jnp.zeros((tq, HEAD_SLOT), jnp.float32)
    _, acc = lax.fori_loop(0, n_chunks // unroll, body, (m0, acc0))
    o_ref[...] = (acc / acc[:, V_HEAD:V_HEAD + 1]).astype(jnp.bfloat16)


def _attention(q, k, v):
    bsz, seq, hw = q.shape
    tq = min(TQ, seq)
    return pl.pallas_call(
        _attn_kernel,
        grid=(bsz, N_HEADS, seq // tq),
        in_specs=[pl.BlockSpec((None, tq, HEAD_SLOT), lambda b, h, i: (b, i, h)),
                  pl.BlockSpec((None, seq, HEAD_SLOT), lambda b, h, i: (b, 0, h)),
                  pl.BlockSpec((None, seq, HEAD_SLOT), lambda b, h, i: (b, 0, h))],
        out_specs=pl.BlockSpec((None, tq, HEAD_SLOT), lambda b, h, i: (b, i, h)),
        out_shape=jax.ShapeDtypeStruct((bsz, seq, hw), jnp.bfloat16),
        compiler_params=_cparams(("parallel", "parallel", "arbitrary")),
        name="attn",
    )(q, k, v)


def _topk_rows(s, iota, iota_blocks):
    vals, idxs = [], []
    big = jnp.float32(1e9)
    n_blk = s.shape[0] // SUBLANES
    for _ in range(PEER_TOPK):
        vb = [s[j * SUBLANES:(j + 1) * SUBLANES, :] for j in range(n_blk)]
        ib = list(iota_blocks)
        while len(vb) > 1:
            nv, ni = [], []
            for a in range(0, len(vb) - 1, 2):
                keep = vb[a] >= vb[a + 1]
                nv.append(jnp.where(keep, vb[a], vb[a + 1]))
                ni.append(jnp.where(keep, ib[a], ib[a + 1]))
            if len(vb) % 2:
                nv.append(vb[-1])
                ni.append(ib[-1])
            vb, ib = nv, ni
        m = jnp.max(vb[0], axis=0, keepdims=True)
        am = jnp.min(jnp.where(vb[0] == m, ib[0], big), axis=0, keepdims=True)
        vals.append(m)
        idxs.append(am)
        s = jnp.where(iota == am, _NEG_INF, s)
    return jnp.concatenate(vals, axis=0), jnp.concatenate(idxs, axis=0)


def _take_rows(tab, idx):
    out = jnp.zeros_like(tab)
    for a in range(tab.shape[0]):
        out = out + jnp.where(idx == jnp.float32(a), tab[a:a + 1, :], 0.0)
    return out


_CAND = [(a, b) for a in range(PEER_TOPK) for b in range(PEER_TOPK) if (a + 1) * (b + 1) <= PEER_TOPK]
N_CAND = -(-len(_CAND) // SUBLANES) * SUBLANES
_PAD_LABEL = float(PEER_TOPK * PEER_TOPK)


def _candidate_tables():
    sel = np.zeros((2, N_CAND, PEER_TOPK), np.float32)
    lab = np.full((N_CAND, 1), _PAD_LABEL, np.float32)
    for r, (a, b) in enumerate(_CAND):
        sel[0, r, a] = 1.0
        sel[1, r, b] = 1.0
        lab[r, 0] = a * PEER_TOPK + b
    return sel, lab


def _mix_kernel(x_ref, a_ref, sgb_ref, o_ref, gate1_ref, scale2_ref, shift2_ref, g2_ref,
                w_ao_ref, w_out_ref, w_pq_ref, kcat_ref, sel_ref, lab_ref,
                x1_ref, h2_ref, e_ref, g_ref):
    ts = x_ref.shape[0]
    out_b = jnp.dot(o_ref[...], w_ao_ref[...], preferred_element_type=jnp.float32)
    merged = (a_ref[...] + sgb_ref[...] * out_b).astype(jnp.bfloat16)
    x1 = x_ref[...] + gate1_ref[...] * jnp.dot(merged, w_out_ref[...],
                                              preferred_element_type=jnp.float32)
    x1_ref[...] = x1
    h2 = _adaln(x1, g2_ref[...], scale2_ref[...], shift2_ref[...])
    h2_ref[...] = h2
    qp = jnp.dot(h2.astype(jnp.bfloat16), w_pq_ref[...],
                 preferred_element_type=jnp.float32).astype(jnp.bfloat16)

    iota_k = lax.broadcasted_iota(jnp.int32, (N_KEYS, ts), 0).astype(jnp.float32)
    iota8 = lax.broadcasted_iota(jnp.int32, (SUBLANES, ts), 0).astype(jnp.float32)
    iota_k_blocks = [iota8 + float(SUBLANES * j) for j in range(N_KEYS // SUBLANES)]
    lab = lab_ref[...]
    lab_blocks = [lab_ref[j * SUBLANES:(j + 1) * SUBLANES, :] for j in range(N_CAND // SUBLANES)]
    valid = lab < _PAD_LABEL
    sel_a = sel_ref[0]
    sel_b = sel_ref[1]
    exact = dict(preferred_element_type=jnp.float32, precision=lax.Precision.HIGHEST)
    e_rows, g_rows = [], []
    for hd in range(PEER_HEADS):
        qh = qp[:, hd * PEER_DK:(hd + 1) * PEER_DK]
        st = lax.dot_general(kcat_ref[hd], qh, (((1,), (1,)), ((), ())),
                             preferred_element_type=jnp.float32)
        v1, i1 = _topk_rows(st[:N_KEYS], iota_k, iota_k_blocks)
        v2, i2 = _topk_rows(st[N_KEYS:], iota_k, iota_k_blocks)
        comb = jnp.dot(sel_a, v1, **exact) + jnp.dot(sel_b, v2, **exact)
        sc, pos = _topk_rows(jnp.where(valid, comb, _NEG_INF), lab, lab_blocks)
        ia = jnp.floor(pos * (1.0 / PEER_TOPK))
        ib = pos - ia * PEER_TOPK
        e_rows.append(_take_rows(i1, ia) * N_KEYS + _take_rows(i2, ib))
        p = jnp.exp(sc - sc[0:1, :])
        g_rows.append(p / jnp.sum(p, axis=0, keepdims=True))
    e_all = jnp.concatenate(e_rows, axis=0)
    g_all = jnp.concatenate(g_rows, axis=0)
    for c in range(ts // LANES):
        cs = slice(c * LANES, (c + 1) * LANES)
        e_ref[cs, :] = (e_all[:, cs].T * ROW_SUB).astype(jnp.int32)
        g_ref[cs, :] = g_all[:, cs].T


def _mix(x, a, sgb, o, gate1, scale2, shift2, wts):
    bsz, seq, _ = x.shape
    ts = min(TS_MIX, seq)
    tile = lambda b, i: (b, i, 0)
    per_b = lambda b, i: (b, 0, 0)
    full2 = lambda b, i: (0, 0)
    hw = N_HEADS * HEAD_SLOT
    big = pl.BlockSpec((None, ts, D_MODEL), tile)
    vec = pl.BlockSpec((None, 1, D_MODEL), per_b)
    picks = pl.BlockSpec((None, ts, N_PICKS), tile)
    sel, lab = _candidate_tables()
    return pl.pallas_call(
        _mix_kernel,
        grid=(bsz, seq // ts),
        in_specs=[big, big, big, pl.BlockSpec((None, ts, hw), tile), vec, vec, vec,
                  pl.BlockSpec((1, D_MODEL), full2),
                  pl.BlockSpec((hw, D_MODEL), full2),
                  pl.BlockSpec((D_MODEL, D_MODEL), full2),
                  pl.BlockSpec((D_MODEL, PEER_HEADS * PEER_DK), full2),
                  pl.BlockSpec((PEER_HEADS, 2 * N_KEYS, PEER_DK), lambda b, i: (0, 0, 0)),
                  pl.BlockSpec((2, N_CAND, PEER_TOPK), lambda b, i: (0, 0, 0)),
                  pl.BlockSpec((N_CAND, ts), full2)],
        out_specs=[big, big, picks, picks],
        out_shape=[jax.ShapeDtypeStruct((bsz, seq, D_MODEL), jnp.float32),
                   jax.ShapeDtypeStruct((bsz, seq, D_MODEL), jnp.float32),
                   jax.ShapeDtypeStruct((bsz, seq, N_PICKS), jnp.int32),
                   jax.ShapeDtypeStruct((bsz, seq, N_PICKS), jnp.float32)],
        compiler_params=_cparams(("parallel", "parallel")),
        name="mix",
    )(x, a, sgb, o, gate1, scale2, shift2, wts["g_norm2"], wts["w_attn_out"], wts["w_out"],
      wts["peer_wq"], wts["kcat"], jnp.asarray(sel), jnp.broadcast_to(jnp.asarray(lab), (N_CAND, ts)))


def _unpack_row(w):
    lo = lax.bitcast_convert_type(lax.shift_left(w, 16), jnp.float32)
    hi = lax.bitcast_convert_type(jnp.bitwise_and(w, jnp.int32(-65536)), jnp.float32)
    return lo, hi


def _gelu_tanh(x):
    c = math.sqrt(2.0 / math.pi)
    return 0.5 * x * (1.0 + jnp.tanh(c * (x + 0.044715 * (x * x * x))))


def _gather_rows(e_ref, t, tab_ref, pb):
    for j in range(N_PICKS // IDX_GROUP):
        row = e_ref.at[pl.ds(pl.multiple_of(t * N_PICKS + j * IDX_GROUP, IDX_GROUP), IDX_GROUP)]
        for c in range(IDX_GROUP):
            k = j * IDX_GROUP + c
            r = pl.multiple_of(row[c], ROW_SUB)
            pb[k * ROW_SUB:(k + 1) * ROW_SUB, :] = tab_ref[pl.ds(r, ROW_SUB), :]


def _pick_block(pb, s):
    return pb[pl.ds(s, N_PICKS, stride=ROW_SUB), :]


def _peer_u_kernel(e_ref, x_ref, g_ref, tab_ref, w_ref, pbuf0, pbuf1, abuf):
    tt = g_ref.shape[0]
    pbufs = (pbuf0, pbuf1)
    lane = lax.broadcasted_iota(jnp.int32, (N_PICKS, tt), 1)
    abuf[...] = jnp.zeros_like(abuf)

    def tok(i, carry):
        x_all = x_ref[pl.ds(pl.multiple_of(i * TOK_UNROLL_U, TOK_UNROLL_U), TOK_UNROLL_U), :]
        for u in range(TOK_UNROLL_U):
            t = i * TOK_UNROLL_U + u
            pb = pbufs[u % 2]
            _gather_rows(e_ref, t, tab_ref, pb)
            acc = None
            for s in range(ROW_SUB):
                lo, hi = _unpack_row(_pick_block(pb, s))
                x_lo = x_all[u:u + 1, s * LANES:(s + 1) * LANES]
                x_hi = x_all[u:u + 1, ROW_WORDS + s * LANES:ROW_WORDS + (s + 1) * LANES]
                term = lo * x_lo + hi * x_hi
                acc = term if acc is None else acc + term
            a_col = jnp.sum(acc, axis=1, keepdims=True)
            abuf[...] = jnp.where(lane == t, a_col, abuf[...])
        return carry

    lax.fori_loop(0, tt // TOK_UNROLL_U, tok, 0)
    w_ref[...] = g_ref[...] * _gelu_tanh(abuf[...].T)


def _peer_u(e, h2, g, tab, n):
    tt = TT_PEER
    return pl.pallas_call(
        _peer_u_kernel,
        grid=(n // tt,),
        in_specs=[pl.BlockSpec((tt * N_PICKS,), lambda i: (i,), memory_space=pltpu.SMEM),
                  pl.BlockSpec((tt, D_MODEL), lambda i: (i, 0)),
                  pl.BlockSpec((tt, N_PICKS), lambda i: (i, 0)),
                  pl.BlockSpec(memory_space=pltpu.VMEM)],
        out_specs=pl.BlockSpec((tt, N_PICKS), lambda i: (i, 0)),
        out_shape=jax.ShapeDtypeStruct((n, N_PICKS), jnp.float32),
        scratch_shapes=[pltpu.VMEM((N_PICKS * ROW_SUB, LANES), jnp.int32),
                        pltpu.VMEM((N_PICKS * ROW_SUB, LANES), jnp.int32),
                        pltpu.VMEM((N_PICKS, tt), jnp.float32)],
        compiler_params=_cparams(("arbitrary",)),
        name="peer_u",
    )(e, h2, g, tab)


def _peer_v_kernel(e_ref, w_ref, x_ref, gate_ref, tab_ref, y_ref, done_ref,
                   pbuf0, pbuf1, whi_ref, wlo_ref, acc_ref):
    tt = x_ref.shape[0]
    pbufs = (pbuf0, pbuf1)
    w = w_ref[...]
    w_hi = w.astype(jnp.bfloat16)
    whi_ref[...] = w_hi
    wlo_ref[...] = (w - w_hi.astype(jnp.float32)).astype(jnp.bfloat16)
    sub = lax.broadcasted_iota(jnp.int32, (BF16_ROWS, D_MODEL), 0)
    acc_ref[...] = jnp.zeros_like(acc_ref)

    def process(pb, t):
        b0 = pl.multiple_of(jnp.bitwise_and(t, -BF16_ROWS), BF16_ROWS)
        lhs = jnp.concatenate([whi_ref[pl.ds(b0, BF16_ROWS), :], wlo_ref[pl.ds(b0, BF16_ROWS), :]], axis=0)
        los, his = [], []
        for s in range(ROW_SUB):
            lo, hi = _unpack_row(_pick_block(pb, s))
            los.append(lo.astype(jnp.bfloat16))
            his.append(hi.astype(jnp.bfloat16))
        rhs = jnp.concatenate(los + his, axis=1)
        res = jnp.dot(lhs, rhs, preferred_element_type=jnp.float32)
        r = res[:BF16_ROWS, :] + res[BF16_ROWS:, :]
        mask = sub == jnp.bitwise_and(t, BF16_ROWS - 1)
        acc_ref[pl.ds(b0, BF16_ROWS), :] = jnp.where(mask, r, acc_ref[pl.ds(b0, BF16_ROWS), :])

    def step(i, carry):
        t0 = i * TOK_UNROLL_V
        for u in range(TOK_UNROLL_V):
            _gather_rows(e_ref, t0 + u, tab_ref, pbufs[u % 2])
            process(pbufs[u % 2], t0 + u)
        return carry

    lax.fori_loop(0, tt // TOK_UNROLL_V, step, 0)
    y_ref[...] = x_ref[...] + gate_ref[...] * acc_ref[...]
    done_ref[...] = jnp.zeros_like(done_ref)


def _peer_v(e, w, x1, gate2, tab, seq, n_tc):
    n = x1.shape[0]
    tt = TT_PEER
    return pl.pallas_call(
        _peer_v_kernel,
        grid=(n_tc // tt,),
        in_specs=[pl.BlockSpec((tt * N_PICKS,), lambda i: (i,), memory_space=pltpu.SMEM),
                  pl.BlockSpec((tt, N_PICKS), lambda i: (i, 0)),
                  pl.BlockSpec((tt, D_MODEL), lambda i: (i, 0)),
                  pl.BlockSpec((None, 1, D_MODEL), lambda i: ((i * tt) // seq, 0, 0)),
                  pl.BlockSpec(memory_space=pltpu.VMEM)],
        out_specs=[pl.BlockSpec((tt, D_MODEL), lambda i: (i, 0)),
                   pl.BlockSpec((SUBLANES, LANES), lambda i: (0, 0))],
        out_shape=[jax.ShapeDtypeStruct((n, D_MODEL), jnp.float32),
                   jax.ShapeDtypeStruct((SUBLANES, LANES), jnp.float32)],
        scratch_shapes=[pltpu.VMEM((N_PICKS * ROW_SUB, LANES), jnp.int32),
                        pltpu.VMEM((N_PICKS * ROW_SUB, LANES), jnp.int32),
                        pltpu.VMEM((tt, N_PICKS), jnp.bfloat16),
                        pltpu.VMEM((tt, N_PICKS), jnp.bfloat16),
                        pltpu.VMEM((tt, D_MODEL), jnp.float32)],
        compiler_params=_cparams(("arbitrary",)),
        name="peer_v",
    )(e, w, x1, gate2, tab)


SC_CORES = 2
SC_SUBCORES = 16
SC_LANES = 16
SC_WORKERS = SC_CORES * SC_SUBCORES
SC_HALF = N_PICKS // 2
SC_CHUNKS = ROW_WORDS // SC_LANES
SC_ACC_CHUNKS = 8
SC_SHARE_FIRST = (1, 1)
SC_SHARE_SECOND = (0, 1)
SC_ALIGN = 1024


def _sc_params():
    cp = pltpu.CompilerParams()
    if "needs_layout_passes" in pltpu.CompilerParams.__dataclass_fields__:
        cp = dataclasses.replace(cp, needs_layout_passes=False)
    return cp


def _gelu_tanh_via_exp(x):
    c = math.sqrt(2.0 / math.pi)
    z = c * (x + 0.044715 * (x * x * x))
    t = jnp.exp(-2.0 * jnp.abs(z))
    return 0.5 * x * (1.0 + jnp.sign(z) * (1.0 - t) / (1.0 + t))


def _peer_sc(e2, g, h2, tab_u, tab_v, off):
    n_sc = e2.shape[0]
    per_w = n_sc // SC_WORKERS
    mesh = plsc.VectorSubcoreMesh(core_axis_name="c", subcore_axis_name="s")
    ln = SC_LANES
    per_token_inputs = lambda: [pltpu.VMEM((SC_HALF,), jnp.int32), pltpu.VMEM((SC_HALF,), jnp.int32),
                                pltpu.VMEM((D_MODEL,), jnp.float32), pltpu.VMEM((N_PICKS,), jnp.float32)]

    @functools.partial(
        pl.kernel, mesh=mesh,
        out_type=jax.ShapeDtypeStruct((n_sc, D_MODEL), jnp.float32),
        scratch_types=per_token_inputs() + per_token_inputs() + [
            pltpu.VMEM((SC_HALF, ROW_WORDS), jnp.int32), pltpu.VMEM((SC_HALF, ROW_WORDS), jnp.int32),
            pltpu.VMEM((N_PICKS,), jnp.float32), pltpu.VMEM((D_MODEL,), jnp.float32),
            pltpu.SemaphoreType.DMA, pltpu.SemaphoreType.DMA,
            pltpu.SemaphoreType.DMA, pltpu.SemaphoreType.DMA],
        compiler_params=_sc_params(),
        cost_estimate=pl.CostEstimate(
            flops=4 * n_sc * N_PICKS * D_MODEL,
            transcendentals=n_sc * N_PICKS,
            bytes_accessed=n_sc * (2 * N_PICKS * ROW_WORDS * 4 + 3 * D_MODEL * 4 + 3 * N_PICKS * 4)),
    )
    def body(e_hbm, g_hbm, x_hbm, tu_hbm, tv_hbm, out_hbm,
             ia0, ib0, x0, g0, ia1, ib1, x1, g1, buf_a, buf_b, w_v, o_v, sem_a, sem_b, sem_in0, sem_in1):
        wid = lax.axis_index("s") * SC_CORES + lax.axis_index("c")
        base = wid * per_w
        lane = lax.iota(jnp.int32, ln)
        zero = jnp.zeros((ln,), jnp.float32)
        in_sets = ((ia0, ib0, x0, g0, sem_in0), (ia1, ib1, x1, g1, sem_in1))

        def input_copies(st, i):
            ia, ib, xv, gv, sem = st
            tl = base + jnp.minimum(i, per_w - 1)
            return [pltpu.make_async_copy(e_hbm.at[tl, 0], ia, sem),
                    pltpu.make_async_copy(e_hbm.at[tl, 1], ib, sem),
                    pltpu.make_async_copy(x_hbm.at[off + tl], xv, sem),
                    pltpu.make_async_copy(g_hbm.at[off + tl], gv, sem)]

        def u_half(buf, k0, x_v, g_v):
            @pl.loop(0, SC_HALF // ln)
            def _(grp):
                def chunk(j, accs):
                    xl = x_v[pl.ds(j * ln, ln)]
                    xh = x_v[pl.ds(ROW_WORDS + j * ln, ln)]
                    new = []
                    for kk in range(ln):
                        lo, hi = _unpack_row(buf[grp * ln + kk, pl.ds(j * ln, ln)])
                        new.append(accs[kk] + lo * xl + hi * xh)
                    return tuple(new)

                accs = lax.fori_loop(0, SC_CHUNKS, chunk, tuple(zero for _ in range(ln)))
                a = zero
                for kk in range(ln):
                    a = jnp.where(lane == kk, jnp.sum(accs[kk]), a)
                sl = pl.ds(k0 + grp * ln, ln)
                w_v[sl] = g_v[sl] * _gelu_tanh_via_exp(a)

        def v_half(buf, k0):
            @pl.loop(0, SC_CHUNKS // SC_ACC_CHUNKS)
            def _(jb):
                def pick(kq, accs):
                    wk = plsc.load_gather(w_v, [jnp.full((ln,), k0, jnp.int32) + kq])
                    new = []
                    for c in range(SC_ACC_CHUNKS):
                        lo, hi = _unpack_row(buf[kq, pl.ds((jb * SC_ACC_CHUNKS + c) * ln, ln)])
                        new.append(accs[2 * c] + wk * lo)
                        new.append(accs[2 * c + 1] + wk * hi)
                    return tuple(new)

                accs = lax.fori_loop(0, SC_HALF, pick, tuple(zero for _ in range(2 * SC_ACC_CHUNKS)))
                for c in range(SC_ACC_CHUNKS):
                    col = (jb * SC_ACC_CHUNKS + c) * ln
                    o_v[pl.ds(col, ln)] = o_v[pl.ds(col, ln)] + accs[2 * c]
                    o_v[pl.ds(ROW_WORDS + col, ln)] = o_v[pl.ds(ROW_WORDS + col, ln)] + accs[2 * c + 1]

        def token(i, cur, nxt):
            ia, ib, x_v, g_v, _ = cur
            for c in input_copies(nxt, i + 1):
                c.start()
            u_b = pltpu.async_copy(tu_hbm.at[ib], buf_b, sem_b)

            @pl.loop(0, D_MODEL // ln)
            def _(c):
                o_v[pl.ds(c * ln, ln)] = zero

            pltpu.make_async_copy(tu_hbm.at[ia], buf_a, sem_a).wait()
            u_half(buf_a, 0, x_v, g_v)
            v_a = pltpu.async_copy(tv_hbm.at[ia], buf_a, sem_a)
            u_b.wait()
            u_half(buf_b, SC_HALF, x_v, g_v)
            v_b = pltpu.async_copy(tv_hbm.at[ib], buf_b, sem_b)
            v_a.wait()
            v_half(buf_a, 0)
            for c in input_copies(nxt, i + 1):
                c.wait()
            pltpu.async_copy(tu_hbm.at[nxt[0]], buf_a, sem_a)
            v_b.wait()
            v_half(buf_b, SC_HALF)
            pltpu.sync_copy(o_v, out_hbm.at[base + i])

        for c in input_copies(in_sets[0], 0):
            c.start()
        for c in input_copies(in_sets[0], 0):
            c.wait()
        pltpu.async_copy(tu_hbm.at[ia0], buf_a, sem_a)

        @pl.loop(0, per_w // 2)
        def _(pair):
            token(2 * pair, in_sets[0], in_sets[1])
            token(2 * pair + 1, in_sets[1], in_sets[0])

        pltpu.make_async_copy(tu_hbm.at[ia0], buf_a, sem_a).wait()

    return body(e2, g, h2, tab_u, tab_v)


def _finish_kernel(*refs):
    x_ref, gate_ref, p_ref, y_ref = refs[-4:]
    y_ref[...] = x_ref[...] + gate_ref[...] * p_ref[...]


def _peer_finish(y, x1, gate2, peer_tail, seq, n_tc):
    n = x1.shape[0]
    tt = TT_PEER
    first = n_tc // tt
    row = lambda i: (first + i, 0)
    in_specs = [pl.BlockSpec((tt, D_MODEL), row),
                pl.BlockSpec((None, 1, D_MODEL), lambda i: (((first + i) * tt) // seq, 0, 0)),
                pl.BlockSpec((tt, D_MODEL), lambda i: (i, 0))]
    args = (x1, gate2, peer_tail)
    aliases = {}
    if n_tc:
        in_specs = [pl.BlockSpec(memory_space=pl.ANY)] + in_specs
        args = (y,) + args
        aliases = {0: 0}
    return pl.pallas_call(
        _finish_kernel,
        grid=((n - n_tc) // tt,),
        in_specs=in_specs,
        out_specs=pl.BlockSpec((tt, D_MODEL), row),
        out_shape=jax.ShapeDtypeStruct((n, D_MODEL), jnp.float32),
        input_output_aliases=aliases,
        compiler_params=_cparams(("arbitrary",)),
        name="peer_finish",
    )(*args)


def _pad_heads(w, used, lead=0):
    r = w.shape[0]
    w3 = w.reshape(r, N_HEADS, used)
    out = jnp.zeros((r, N_HEADS, HEAD_SLOT), w.dtype).at[:, :, lead:lead + used].set(w3)
    return out.reshape(r, N_HEADS * HEAD_SLOT)


def _pack_table(t):
    tb = t.astype(jnp.bfloat16)
    lo = lax.bitcast_convert_type(tb[:, :ROW_WORDS], jnp.uint16).astype(jnp.uint32)
    hi = lax.bitcast_convert_type(tb[:, ROW_WORDS:], jnp.uint16).astype(jnp.uint32)
    words = lax.bitcast_convert_type(lo | (hi << 16), jnp.int32)
    return words.reshape(t.shape[0] * ROW_SUB, LANES)


def _prep_weights(g_norm1, w_in, conv_w, w_conv_out, g_q_lora, w_uq, g_kv_lora, w_ukv, g_qnorm,
                  g_knorm, w_attn_out, w_out, g_norm2, peer_wq, peer_k1, peer_k2, peer_u, peer_v):
    bf = jnp.bfloat16
    o1 = 3 * D_CONV
    o2 = o1 + Q_LORA
    o3 = o2 + KV_LORA
    o4 = o3 + QK_ROPE
    o5 = o4 + D_MODEL
    kr_slot = jnp.zeros((D_MODEL, HEAD_SLOT), w_in.dtype).at[:, QK_NOPE:QK_HEAD].set(w_in[:, o3:o4])
    w_in_r = jnp.concatenate([w_in[:, :o3], kr_slot, w_in[:, o4:]], axis=1).astype(bf)
    w_ukv3 = w_ukv.reshape(KV_LORA, N_HEADS, QK_NOPE + V_HEAD)
    w_uk = _pad_heads(w_ukv3[:, :, :QK_NOPE].reshape(KV_LORA, -1), QK_NOPE)
    w_uv = _pad_heads(w_ukv3[:, :, QK_NOPE:].reshape(KV_LORA, -1), V_HEAD)
    pad_gain = lambda g: jnp.zeros((1, HEAD_SLOT), jnp.float32).at[0, :QK_HEAD].set(g)
    w_ao = jnp.zeros((N_HEADS, HEAD_SLOT, D_MODEL), w_attn_out.dtype).at[:, :V_HEAD].set(
        w_attn_out.reshape(N_HEADS, V_HEAD, D_MODEL)).reshape(N_HEADS * HEAD_SLOT, D_MODEL)
    half = PEER_DK // 2
    kcat = jnp.zeros((PEER_HEADS, 2 * N_KEYS, PEER_DK), jnp.float32)
    kcat = kcat.at[:, :N_KEYS, :half].set(peer_k1).at[:, N_KEYS:, half:].set(peer_k2)
    tab_u = _pack_table(peer_u)
    tab_v = _pack_table(peer_v)
    return {
        "g_norm1": g_norm1.reshape(1, -1), "w_in": w_in_r, "conv_w": conv_w,
        "w_conv_out": w_conv_out.astype(bf), "g_q_lora": g_q_lora.reshape(1, -1),
        "w_uq": _pad_heads(w_uq, QK_HEAD).astype(bf), "g_kv_lora": g_kv_lora.reshape(1, -1),
        "w_uk": w_uk.astype(bf), "w_uv": w_uv.astype(bf),
        "g_qnorm": pad_gain(g_qnorm), "g_knorm": pad_gain(g_knorm),
        "w_attn_out": w_ao.astype(bf), "w_out": w_out.astype(bf), "g_norm2": g_norm2.reshape(1, -1),
        "peer_wq": peer_wq.astype(bf), "kcat": kcat.astype(bf),
        "tab_u": tab_u, "tab_v": tab_v,
        "tab_u_sc": tab_u.reshape(-1, ROW_WORDS), "tab_v_sc": tab_v.reshape(-1, ROW_WORDS),
    }


def _rope_tables(seq):
    pos = jnp.arange(seq, dtype=jnp.float32)
    inv = ROPE_THETA ** (-jnp.arange(0, QK_ROPE, 2, dtype=jnp.float32) / QK_ROPE)
    ang = pos[:, None] * inv[None, :]
    cos, sin = jnp.cos(ang), jnp.sin(ang)
    t1 = slice(QK_NOPE, QK_NOPE + HALF_ROPE)
    t2 = slice(QK_NOPE + HALF_ROPE, QK_HEAD)
    cos_t = jnp.ones((seq, HEAD_SLOT), jnp.float32).at[:, t1].set(cos).at[:, t2].set(cos)
    sin_a = jnp.zeros((seq, HEAD_SLOT), jnp.float32).at[:, t1].set(-sin)
    sin_b = jnp.zeros((seq, HEAD_SLOT), jnp.float32).at[:, t2].set(sin)
    return cos_t, sin_a, sin_b


def _encoder_layer(x, c, w_ada, b_ada, wts, sc_share, after):
    bsz, seq, d = x.shape
    n = bsz * seq
    ada = _ada(c, w_ada, b_ada)[:, None, :]
    shift1, scale1, gate1, shift2, scale2, gate2 = jnp.split(ada, 6, axis=-1)
    q, k, v, a, sgb = _inproj(x, scale1, shift1, wts, _rope_tables(seq), after)
    o = _attention(q, k, v)
    x1, h2, e, g = _mix(x, a, sgb, o, gate1, scale2, shift2, wts)
    n_sc = (n * sc_share[0] // sc_share[1]) // SC_ALIGN * SC_ALIGN
    n_tc = n - n_sc
    e = e.reshape(n, N_PICKS)
    g = g.reshape(n, N_PICKS)
    h2 = h2.reshape(n, d)
    x1 = x1.reshape(n, d)
    if n_sc:
        ids = lax.shift_right_logical(e[n_tc:], ROW_SUB.bit_length() - 1)
        peer_tail = _peer_sc(ids.reshape(n_sc, 2, SC_HALF), g, h2,
                             wts["tab_u_sc"], wts["tab_v_sc"], n_tc)
    if n_tc:
        e = e.reshape(n * N_PICKS)
        w = _peer_u(e, h2, g, wts["tab_u"], n_tc)
        y, done = _peer_v(e, w, x1, gate2, wts["tab_v"], seq, n_tc)
    else:
        y, done = None, g
    if n_sc:
        y = _peer_finish(y, x1, gate2, peer_tail, seq, n_tc)
    return y.reshape(bsz, seq, d), done


def kernel(x_prompt, x_sample, c_prompt, c_sample, w_ada, b_ada, g_norm1, w_in, conv_w, w_conv_out, g_q_lora, w_uq, g_kv_lora, w_ukv, g_qnorm, g_knorm, w_attn_out, w_out, g_norm2, peer_wq, peer_k1, peer_k2, peer_u, peer_v):
    wts = _prep_weights(g_norm1[0], w_in[0], conv_w[0], w_conv_out[0], g_q_lora[0], w_uq[0],
                        g_kv_lora[0], w_ukv[0], g_qnorm[0], g_knorm[0], w_attn_out[0], w_out[0],
                        g_norm2[0], peer_wq[0], peer_k1[0], peer_k2[0], peer_u[0], peer_v[0])
    sc_tables = (wts["tab_u_sc"], wts["tab_v_sc"])
    y_sample, sample_done = _encoder_layer(x_sample, c_sample, w_ada[0], b_ada[0], wts,
                                           SC_SHARE_FIRST, sc_tables)
    y_prompt, _ = _encoder_layer(x_prompt, c_prompt, w_ada[0], b_ada[0], wts,
                                 SC_SHARE_SECOND, sc_tables + (sample_done,))
    return y_prompt, y_sample
```

```python
import dataclasses
import functools
import math

import jax
import jax.numpy as jnp
import numpy as np
from jax import lax
from jax.experimental import pallas as pl
from jax.experimental.pallas import tpu as pltpu
from jax.experimental.pallas import tpu_sc as plsc

D_MODEL = 1024
D_CONV = 512
N_HEADS = 8
QK_NOPE = 64
QK_ROPE = 32
V_HEAD = 64
Q_LORA = 256
KV_LORA = 128
QK_HEAD = QK_NOPE + QK_ROPE
ROPE_THETA = 10000.0
PEER_HEADS = 8
N_KEYS = 128
PEER_DK = 128
PEER_TOPK = 16
EPS = 1e-6

LANES = 128
SUBLANES = 8
HEAD_SLOT = LANES
HALF_ROPE = QK_ROPE // 2
N_PICKS = PEER_HEADS * PEER_TOPK
ROW_WORDS = D_MODEL // 2
ROW_SUB = ROW_WORDS // LANES
VMEM_LIMIT = 56 * 1024 * 1024

C_BG, C_CG, C_HC = 0, 512, 1024
C_CQ = 1536
C_CKV = C_CQ + Q_LORA
C_KR = C_CKV + KV_LORA
C_GA = C_KR + HEAD_SLOT
C_GB = C_GA + D_MODEL
IN_COLS_R = C_GB + D_MODEL

TS_IN = 512
TQ = 512
TK = 2048
KV_UNROLL = 2
TS_MIX = 256
TT_PEER = 128
TOK_UNROLL_U = 16
TOK_UNROLL_V = 16
IDX_GROUP = 8

_NEG_INF = float("-inf")


def _cparams(sem):
    return pltpu.CompilerParams(dimension_semantics=sem, vmem_limit_bytes=VMEM_LIMIT)


def _ada_kernel(c_ref, w_ref, b_ref, o_ref):
    o_ref[...] = jnp.dot(c_ref[...], w_ref[...], preferred_element_type=jnp.float32,
                         precision=lax.Precision.HIGHEST) + b_ref[...]


def _ada(c, w_ada, b_ada):
    bsz = c.shape[0]
    rows = -(-bsz // SUBLANES) * SUBLANES
    c_p = jnp.zeros((rows, D_MODEL), jnp.float32).at[:bsz].set(c)
    out = pl.pallas_call(
        _ada_kernel,
        grid=(6,),
        in_specs=[pl.BlockSpec((rows, D_MODEL), lambda j: (0, 0)),
                  pl.BlockSpec((D_MODEL, D_MODEL), lambda j: (0, j)),
                  pl.BlockSpec((1, D_MODEL), lambda j: (0, j))],
        out_specs=pl.BlockSpec((rows, D_MODEL), lambda j: (0, j)),
        out_shape=jax.ShapeDtypeStruct((rows, 6 * D_MODEL), jnp.float32),
        compiler_params=_cparams(("arbitrary",)),
        name="ada",
    )(c_p, w_ada, b_ada.reshape(1, -1))
    return out[:bsz]


def _adaln(x, g, scale, shift):
    ms = jnp.mean(x * x, axis=-1, keepdims=True)
    return (x * lax.rsqrt(ms + EPS) * g) * (1.0 + scale) + shift


def _rms_rows(x, g, n):
    ms = jnp.sum(x * x, axis=-1, keepdims=True) * (1.0 / n)
    return x * lax.rsqrt(ms + EPS) * g


def _rope(x, cos, sin_a, sin_b):
    return (x * cos + pltpu.roll(x, LANES - HALF_ROPE, axis=1) * sin_a
            + pltpu.roll(x, HALF_ROPE, axis=1) * sin_b)


def _inproj_kernel(x_ref, xp_ref, xn_ref, scale_ref, shift_ref, g1_ref, w_in_ref, conv_w_ref,
                   w_co_ref, gql_ref, w_uq_ref, gkvl_ref, w_uk_ref, w_uv_ref, gqn_ref, gkn_ref,
                   cos_ref, sa_ref, sb_ref, *rest):
    q_ref, k_ref, v_ref, a_ref, sgb_ref = rest[-5:]
    i = pl.program_id(1)
    n_i = pl.num_programs(1)
    g1 = g1_ref[...]
    scale = scale_ref[...]
    shift = shift_ref[...]
    ts = x_ref.shape[0]

    h = _adaln(x_ref[...], g1, scale, shift).astype(jnp.bfloat16)

    def proj(lo, width):
        return jnp.dot(h, w_in_ref[:, lo:lo + width], preferred_element_type=jnp.float32)

    z = proj(C_CG, D_CONV) * proj(C_HC, D_CONV)

    def halo_z(xh_ref):
        hh = _adaln(xh_ref[...], g1, scale, shift).astype(jnp.bfloat16)
        zc = jnp.dot(hh, w_in_ref[:, C_CG:C_CG + 2 * D_CONV], preferred_element_type=jnp.float32)
        return zc[:, :D_CONV] * zc[:, D_CONV:]

    z_prev = halo_z(xp_ref)[SUBLANES - 1:SUBLANES, :]
    z_next = halo_z(xn_ref)[0:1, :]
    z_prev = jnp.where(i == 0, 0.0, z_prev)
    z_next = jnp.where(i == n_i - 1, 0.0, z_next)
    row = lax.broadcasted_iota(jnp.int32, (ts, D_CONV), 0)
    z_up = jnp.where(row == 0, z_prev, pltpu.roll(z, 1, axis=0))
    z_dn = jnp.where(row == ts - 1, z_next, pltpu.roll(z, ts - 1, axis=0))
    cw = conv_w_ref[...]
    y = z_up * cw[0:1, :] + z * cw[1:2, :] + z_dn * cw[2:3, :]
    out_a = jnp.dot((proj(C_BG, D_CONV) * y).astype(jnp.bfloat16), w_co_ref[...],
                    preferred_element_type=jnp.float32)
    a_ref[...] = jax.nn.sigmoid(proj(C_GA, D_MODEL)) * out_a
    sgb_ref[...] = jax.nn.sigmoid(proj(C_GB, D_MODEL))

    cos = cos_ref[...]
    sin_a = sa_ref[...]
    sin_b = sb_ref[...]
    cq = _rms_rows(proj(C_CQ, Q_LORA), gql_ref[...], Q_LORA).astype(jnp.bfloat16)
    qf = jnp.dot(cq, w_uq_ref[...], preferred_element_type=jnp.float32)
    ckv = _rms_rows(proj(C_CKV, KV_LORA), gkvl_ref[...], KV_LORA).astype(jnp.bfloat16)
    kf = jnp.dot(ckv, w_uk_ref[...], preferred_element_type=jnp.float32)
    lane = lax.broadcasted_iota(jnp.int32, (1, N_HEADS * HEAD_SLOT), 1)
    ones_lane = (jnp.bitwise_and(lane, HEAD_SLOT - 1) == V_HEAD).astype(jnp.float32)
    v_ref[...] = (jnp.dot(ckv, w_uv_ref[...], preferred_element_type=jnp.float32)
                  + ones_lane).astype(jnp.bfloat16)
    kr = proj(C_KR, HEAD_SLOT)
    gqn = gqn_ref[...]
    gkn = gkn_ref[...]
    q_scale = QK_HEAD ** -0.5 * math.log2(math.e)
    for hd in range(N_HEADS):
        sl = slice(hd * HEAD_SLOT, (hd + 1) * HEAD_SLOT)
        qh = _rope(_rms_rows(qf[:, sl], gqn, QK_HEAD), cos, sin_a, sin_b)
        q_ref[:, sl] = (qh * q_scale).astype(jnp.bfloat16)
        kh = _rope(_rms_rows(kf[:, sl] + kr, gkn, QK_HEAD), cos, sin_a, sin_b)
        k_ref[:, sl] = kh.astype(jnp.bfloat16)


def _inproj(x, scale1, shift1, wts, rope, after):
    bsz, seq, _ = x.shape
    ts = min(TS_IN, seq)
    n_i = seq // ts
    nb8 = seq // SUBLANES
    per8 = ts // SUBLANES
    tile = lambda b, i: (b, i, 0)
    per_b = lambda b, i: (b, 0, 0)
    full2 = lambda b, i: (0, 0)
    hw = N_HEADS * HEAD_SLOT
    in_specs = [
        pl.BlockSpec((None, ts, D_MODEL), tile),
        pl.BlockSpec((None, SUBLANES, D_MODEL), lambda b, i: (b, jnp.maximum(i * per8 - 1, 0), 0)),
        pl.BlockSpec((None, SUBLANES, D_MODEL), lambda b, i: (b, jnp.minimum((i + 1) * per8, nb8 - 1), 0)),
        pl.BlockSpec((None, 1, D_MODEL), per_b),
        pl.BlockSpec((None, 1, D_MODEL), per_b),
        pl.BlockSpec((1, D_MODEL), full2),
        pl.BlockSpec((D_MODEL, IN_COLS_R), full2),
        pl.BlockSpec((3, D_CONV), full2),
        pl.BlockSpec((D_CONV, D_MODEL), full2),
        pl.BlockSpec((1, Q_LORA), full2),
        pl.BlockSpec((Q_LORA, hw), full2),
        pl.BlockSpec((1, KV_LORA), full2),
        pl.BlockSpec((KV_LORA, hw), full2),
        pl.BlockSpec((KV_LORA, hw), full2),
        pl.BlockSpec((1, HEAD_SLOT), full2),
        pl.BlockSpec((1, HEAD_SLOT), full2),
        pl.BlockSpec((ts, HEAD_SLOT), lambda b, i: (i, 0)),
        pl.BlockSpec((ts, HEAD_SLOT), lambda b, i: (i, 0)),
        pl.BlockSpec((ts, HEAD_SLOT), lambda b, i: (i, 0)),
    ] + [pl.BlockSpec(memory_space=pl.ANY)] * len(after)
    out_specs = [pl.BlockSpec((None, ts, hw), tile)] * 3 + [pl.BlockSpec((None, ts, D_MODEL), tile)] * 2
    out_shape = ([jax.ShapeDtypeStruct((bsz, seq, hw), jnp.bfloat16)] * 3
                 + [jax.ShapeDtypeStruct((bsz, seq, D_MODEL), jnp.float32)] * 2)
    return pl.pallas_call(
        _inproj_kernel,
        grid=(bsz, n_i),
        in_specs=in_specs,
        out_specs=out_specs,
        out_shape=out_shape,
        compiler_params=_cparams(("parallel", "parallel")),
        name="inproj",
    )(x, x, x, scale1, shift1, wts["g_norm1"], wts["w_in"], wts["conv_w"], wts["w_conv_out"],
      wts["g_q_lora"], wts["w_uq"], wts["g_kv_lora"], wts["w_uk"], wts["w_uv"],
      wts["g_qnorm"], wts["g_knorm"], rope[0], rope[1], rope[2], *after)


def _attn_kernel(q_ref, k_ref, v_ref, o_ref):
    q = q_ref[...]
    tq = q.shape[0]
    seq = k_ref.shape[0]
    tk = min(TK, seq)

    n_chunks = seq // tk
    unroll = KV_UNROLL if n_chunks % KV_UNROLL == 0 else 1

    def chunk(off, carry):
        m, acc = carry
        kc = k_ref[pl.ds(off, tk), :]
        vc = v_ref[pl.ds(off, tk), :]
        s = lax.dot_general(q, kc, (((1,), (1,)), ((), ())), preferred_element_type=jnp.float32)
        m_new = jnp.maximum(m, jnp.max(s, axis=-1, keepdims=True))
        p = jnp.exp2(s - m_new).astype(jnp.bfloat16)
        alpha = jnp.exp2(m - m_new)
        acc = alpha * acc + jnp.dot(p, vc, preferred_element_type=jnp.float32)
        return m_new, acc

    def body(j, carry):
        for u in range(unroll):
            carry = chunk(pl.multiple_of((j * unroll + u) * tk, tk), carry)
        return carry

    m0 = jnp.full((tq, 1), _NEG_INF, jnp.float32)
    acc0 = jnp.zeros((tq, HEAD_SLOT), jnp.float32)
    _, acc = lax.fori_loop(0, n_chunks // unroll, body, (m0, acc0))
    o_ref[...] = (acc / acc[:, V_HEAD:V_HEAD + 1]).astype(jnp.bfloat16)


def _attention(q, k, v):
    bsz, seq, hw = q.shape
    tq = min(TQ, seq)
    return pl.pallas_call(
        _attn_kernel,
        grid=(bsz, N_HEADS, seq // tq),
        in_specs=[pl.BlockSpec((None, tq, HEAD_SLOT), lambda b, h, i: (b, i, h)),
                  pl.BlockSpec((None, seq, HEAD_SLOT), lambda b, h, i: (b, 0, h)),
                  pl.BlockSpec((None, seq, HEAD_SLOT), lambda b, h, i: (b, 0, h))],
        out_specs=pl.BlockSpec((None, tq, HEAD_SLOT), lambda b, h, i: (b, i, h)),
        out_shape=jax.ShapeDtypeStruct((bsz, seq, hw), jnp.bfloat16),
        compiler_params=_cparams(("parallel", "parallel", "arbitrary")),
        name="attn",
    )(q, k, v)


def _topk_rows(s, iota, iota_blocks):
    vals, idxs = [], []
    big = jnp.float32(1e9)
    n_blk = s.shape[0] // SUBLANES
    for _ in range(PEER_TOPK):
        vb = [s[j * SUBLANES:(j + 1) * SUBLANES, :] for j in range(n_blk)]
        ib = list(iota_blocks)
        while len(vb) > 1:
            nv, ni = [], []
            for a in range(0, len(vb) - 1, 2):
                keep = vb[a] >= vb[a + 1]
                nv.append(jnp.where(keep, vb[a], vb[a + 1]))
                ni.append(jnp.where(keep, ib[a], ib[a + 1]))
            if len(vb) % 2:
                nv.append(vb[-1])
                ni.append(ib[-1])
            vb, ib = nv, ni
        m = jnp.max(vb[0], axis=0, keepdims=True)
        am = jnp.min(jnp.where(vb[0] == m, ib[0], big), axis=0, keepdims=True)
        vals.append(m)
        idxs.append(am)
        s = jnp.where(iota == am, _NEG_INF, s)
    return jnp.concatenate(vals, axis=0), jnp.concatenate(idxs, axis=0)


def _take_rows(tab, idx):
    out = jnp.zeros_like(tab)
    for a in range(tab.shape[0]):
        out = out + jnp.where(idx == jnp.float32(a), tab[a:a + 1, :], 0.0)
    return out


_CAND = [(a, b) for a in range(PEER_TOPK) for b in range(PEER_TOPK) if (a + 1) * (b + 1) <= PEER_TOPK]
N_CAND = -(-len(_CAND) // SUBLANES) * SUBLANES
_PAD_LABEL = float(PEER_TOPK * PEER_TOPK)


def _candidate_tables():
    sel = np.zeros((2, N_CAND, PEER_TOPK), np.float32)
    lab = np.full((N_CAND, 1), _PAD_LABEL, np.float32)
    for r, (a, b) in enumerate(_CAND):
        sel[0, r, a] = 1.0
        sel[1, r, b] = 1.0
        lab[r, 0] = a * PEER_TOPK + b
    return sel, lab


def _mix_kernel(x_ref, a_ref, sgb_ref, o_ref, gate1_ref, scale2_ref, shift2_ref, g2_ref,
                w_ao_ref, w_out_ref, w_pq_ref, kcat_ref, sel_ref, lab_ref,
                x1_ref, h2_ref, e_ref, g_ref):
    ts = x_ref.shape[0]
    out_b = jnp.dot(o_ref[...], w_ao_ref[...], preferred_element_type=jnp.float32)
    merged = (a_ref[...] + sgb_ref[...] * out_b).astype(jnp.bfloat16)
    x1 = x_ref[...] + gate1_ref[...] * jnp.dot(merged, w_out_ref[...],
                                              preferred_element_type=jnp.float32)
    x1_ref[...] = x1
    h2 = _adaln(x1, g2_ref[...], scale2_ref[...], shift2_ref[...])
    h2_ref[...] = h2
    qp = jnp.dot(h2.astype(jnp.bfloat16), w_pq_ref[...],
                 preferred_element_type=jnp.float32).astype(jnp.bfloat16)

    iota_k = lax.broadcasted_iota(jnp.int32, (N_KEYS, ts), 0).astype(jnp.float32)
    iota8 = lax.broadcasted_iota(jnp.int32, (SUBLANES, ts), 0).astype(jnp.float32)
    iota_k_blocks = [iota8 + float(SUBLANES * j) for j in range(N_KEYS // SUBLANES)]
    lab = lab_ref[...]
    lab_blocks = [lab_ref[j * SUBLANES:(j + 1) * SUBLANES, :] for j in range(N_CAND // SUBLANES)]
    valid = lab < _PAD_LABEL
    sel_a = sel_ref[0]
    sel_b = sel_ref[1]
    exact = dict(preferred_element_type=jnp.float32, precision=lax.Precision.HIGHEST)
    e_rows, g_rows = [], []
    for hd in range(PEER_HEADS):
        qh = qp[:, hd * PEER_DK:(hd + 1) * PEER_DK]
        st = lax.dot_general(kcat_ref[hd], qh, (((1,), (1,)), ((), ())),
                             preferred_element_type=jnp.float32)
        v1, i1 = _topk_rows(st[:N_KEYS], iota_k, iota_k_blocks)
        v2, i2 = _topk_rows(st[N_KEYS:], iota_k, iota_k_blocks)
        comb = jnp.dot(sel_a, v1, **exact) + jnp.dot(sel_b, v2, **exact)
        sc, pos = _topk_rows(jnp.where(valid, comb, _NEG_INF), lab, lab_blocks)
        ia = jnp.floor(pos * (1.0 / PEER_TOPK))
        ib = pos - ia * PEER_TOPK
        e_rows.append(_take_rows(i1, ia) * N_KEYS + _take_rows(i2, ib))
        p = jnp.exp(sc - sc[0:1, :])
        g_rows.append(p / jnp.sum(p, axis=0, keepdims=True))
    e_all = jnp.concatenate(e_rows, axis=0)
    g_all = jnp.concatenate(g_rows, axis=0)
    for c in range(ts // LANES):
        cs = slice(c * LANES, (c + 1) * LANES)
        e_ref[cs, :] = (e_all[:, cs].T * ROW_SUB).astype(jnp.int32)
        g_ref[cs, :] = g_all[:, cs].T


def _mix(x, a, sgb, o, gate1, scale2, shift2, wts):
    bsz, seq, _ = x.shape
    ts = min(TS_MIX, seq)
    tile = lambda b, i: (b, i, 0)
    per_b = lambda b, i: (b, 0, 0)
    full2 = lambda b, i: (0, 0)
    hw = N_HEADS * HEAD_SLOT
    big = pl.BlockSpec((None, ts, D_MODEL), tile)
    vec = pl.BlockSpec((None, 1, D_MODEL), per_b)
    picks = pl.BlockSpec((None, ts, N_PICKS), tile)
    sel, lab = _candidate_tables()
    return pl.pallas_call(
        _mix_kernel,
        grid=(bsz, seq // ts),
        in_specs=[big, big, big, pl.BlockSpec((None, ts, hw), tile), vec, vec, vec,
                  pl.BlockSpec((1, D_MODEL), full2),
                  pl.BlockSpec((hw, D_MODEL), full2),
                  pl.BlockSpec((D_MODEL, D_MODEL), full2),
                  pl.BlockSpec((D_MODEL, PEER_HEADS * PEER_DK), full2),
                  pl.BlockSpec((PEER_HEADS, 2 * N_KEYS, PEER_DK), lambda b, i: (0, 0, 0)),
                  pl.BlockSpec((2, N_CAND, PEER_TOPK), lambda b, i: (0, 0, 0)),
                  pl.BlockSpec((N_CAND, ts), full2)],
        out_specs=[big, big, picks, picks],
        out_shape=[jax.ShapeDtypeStruct((bsz, seq, D_MODEL), jnp.float32),
                   jax.ShapeDtypeStruct((bsz, seq, D_MODEL), jnp.float32),
                   jax.ShapeDtypeStruct((bsz, seq, N_PICKS), jnp.int32),
                   jax.ShapeDtypeStruct((bsz, seq, N_PICKS), jnp.float32)],
        compiler_params=_cparams(("parallel", "parallel")),
        name="mix",
    )(x, a, sgb, o, gate1, scale2, shift2, wts["g_norm2"], wts["w_attn_out"], wts["w_out"],
      wts["peer_wq"], wts["kcat"], jnp.asarray(sel), jnp.broadcast_to(jnp.asarray(lab), (N_CAND, ts)))


def _unpack_row(w):
    lo = lax.bitcast_convert_type(lax.shift_left(w, 16), jnp.float32)
    hi = lax.bitcast_convert_type(jnp.bitwise_and(w, jnp.int32(-65536)), jnp.float32)
    return lo, hi


def _gelu_tanh(x):
    c = math.sqrt(2.0 / math.pi)
    return 0.5 * x * (1.0 + jnp.tanh(c * (x + 0.044715 * (x * x * x))))


def _gather_rows(e_ref, t, tab_ref, pb):
    for j in range(N_PICKS // IDX_GROUP):
        row = e_ref.at[pl.ds(pl.multiple_of(t * N_PICKS + j * IDX_GROUP, IDX_GROUP), IDX_GROUP)]
        for c in range(IDX_GROUP):
            k = j * IDX_GROUP + c
            r = pl.multiple_of(row[c], ROW_SUB)
            pb[k * ROW_SUB:(k + 1) * ROW_SUB, :] = tab_ref[pl.ds(r, ROW_SUB), :]


def _pick_block(pb, s):
    return pb[pl.ds(s, N_PICKS, stride=ROW_SUB), :]


def _peer_u_kernel(e_ref, x_ref, g_ref, tab_ref, w_ref, pbuf0, pbuf1, abuf):
    tt = g_ref.shape[0]
    pbufs = (pbuf0, pbuf1)
    lane = lax.broadcasted_iota(jnp.int32, (N_PICKS, tt), 1)
    abuf[...] = jnp.zeros_like(abuf)

    def tok(i, carry):
        x_all = x_ref[pl.ds(pl.multiple_of(i * TOK_UNROLL_U, TOK_UNROLL_U), TOK_UNROLL_U), :]
        for u in range(TOK_UNROLL_U):
            t = i * TOK_UNROLL_U + u
            pb = pbufs[u % 2]
            _gather_rows(e_ref, t, tab_ref, pb)
            acc = None
            for s in range(ROW_SUB):
                lo, hi = _unpack_row(_pick_block(pb, s))
                x_lo = x_all[u:u + 1, s * LANES:(s + 1) * LANES]
                x_hi = x_all[u:u + 1, ROW_WORDS + s * LANES:ROW_WORDS + (s + 1) * LANES]
                term = lo * x_lo + hi * x_hi
                acc = term if acc is None else acc + term
            a_col = jnp.sum(acc, axis=1, keepdims=True)
            abuf[...] = jnp.where(lane == t, a_col, abuf[...])
        return carry

    lax.fori_loop(0, tt // TOK_UNROLL_U, tok, 0)
    w_ref[...] = g_ref[...] * _gelu_tanh(abuf[...].T)


def _peer_u(e, h2, g, tab):
    n = h2.shape[0]
    tt = TT_PEER
    return pl.pallas_call(
        _peer_u_kernel,
        grid=(n // tt,),
        in_specs=[pl.BlockSpec((tt * N_PICKS,), lambda i: (i,), memory_space=pltpu.SMEM),
                  pl.BlockSpec((tt, D_MODEL), lambda i: (i, 0)),
                  pl.BlockSpec((tt, N_PICKS), lambda i: (i, 0)),
                  pl.BlockSpec(memory_space=pltpu.VMEM)],
        out_specs=pl.BlockSpec((tt, N_PICKS), lambda i: (i, 0)),
        out_shape=jax.ShapeDtypeStruct((n, N_PICKS), jnp.float32),
        scratch_shapes=[pltpu.VMEM((N_PICKS * ROW_SUB, LANES), jnp.int32),
                        pltpu.VMEM((N_PICKS * ROW_SUB, LANES), jnp.int32),
                        pltpu.VMEM((N_PICKS, tt), jnp.float32)],
        compiler_params=_cparams(("arbitrary",)),
        name="peer_u",
    )(e, h2, g, tab)


def _peer_v_kernel(e_ref, w_ref, x_ref, gate_ref, tab_ref, y_ref, pbuf0, pbuf1, wt_ref, acc_ref):
    tt = x_ref.shape[0]
    pbufs = (pbuf0, pbuf1)
    wt_ref[...] = w_ref[...].T
    lane = lax.broadcasted_iota(jnp.int32, (N_PICKS, tt), 1)
    sub = lax.broadcasted_iota(jnp.int32, (SUBLANES, D_MODEL), 0)
    acc_ref[...] = jnp.zeros_like(acc_ref)

    def process(pb, t):
        w_col = jnp.sum(jnp.where(lane == t, wt_ref[...], 0.0), axis=1, keepdims=True)
        los, his = [], []
        for s in range(ROW_SUB):
            lo, hi = _unpack_row(_pick_block(pb, s))
            los.append(jnp.sum(lo * w_col, axis=0, keepdims=True))
            his.append(jnp.sum(hi * w_col, axis=0, keepdims=True))
        row = jnp.concatenate(los + his, axis=1)
        b0 = pl.multiple_of(jnp.bitwise_and(t, -SUBLANES), SUBLANES)
        mask = sub == jnp.bitwise_and(t, SUBLANES - 1)
        acc_ref[pl.ds(b0, SUBLANES), :] = jnp.where(mask, row, acc_ref[pl.ds(b0, SUBLANES), :])

    def step(i, carry):
        t0 = i * TOK_UNROLL_V
        for u in range(TOK_UNROLL_V):
            _gather_rows(e_ref, t0 + u, tab_ref, pbufs[u % 2])
            process(pbufs[u % 2], t0 + u)
        return carry

    lax.fori_loop(0, tt // TOK_UNROLL_V, step, 0)
    y_ref[...] = x_ref[...] + gate_ref[...] * acc_ref[...]


def _peer_v(e, w, x1, gate2, tab, seq):
    n = x1.shape[0]
    tt = TT_PEER
    return pl.pallas_call(
        _peer_v_kernel,
        grid=(n // tt,),
        in_specs=[pl.BlockSpec((tt * N_PICKS,), lambda i: (i,), memory_space=pltpu.SMEM),
                  pl.BlockSpec((tt, N_PICKS), lambda i: (i, 0)),
                  pl.BlockSpec((tt, D_MODEL), lambda i: (i, 0)),
                  pl.BlockSpec((None, 1, D_MODEL), lambda i: ((i * tt) // seq, 0, 0)),
                  pl.BlockSpec(memory_space=pltpu.VMEM)],
        out_specs=pl.BlockSpec((tt, D_MODEL), lambda i: (i, 0)),
        out_shape=jax.ShapeDtypeStruct((n, D_MODEL), jnp.float32),
        scratch_shapes=[pltpu.VMEM((N_PICKS * ROW_SUB, LANES), jnp.int32),
                        pltpu.VMEM((N_PICKS * ROW_SUB, LANES), jnp.int32),
                        pltpu.VMEM((N_PICKS, tt), jnp.float32),
                        pltpu.VMEM((tt, D_MODEL), jnp.float32)],
        compiler_params=_cparams(("arbitrary",)),
        name="peer_v",
    )(e, w, x1, gate2, tab)


SC_CORES = 2
SC_SUBCORES = 16
SC_LANES = 16
SC_WORKERS = SC_CORES * SC_SUBCORES
SC_HALF = N_PICKS // 2
SC_CHUNKS = ROW_WORDS // SC_LANES
SC_ACC_CHUNKS = 8


def _sc_params():
    cp = pltpu.CompilerParams()
    if "needs_layout_passes" in pltpu.CompilerParams.__dataclass_fields__:
        cp = dataclasses.replace(cp, needs_layout_passes=False)
    return cp


def _gelu_tanh_via_exp(x):
    c = math.sqrt(2.0 / math.pi)
    z = c * (x + 0.044715 * (x * x * x))
    t = jnp.exp(-2.0 * jnp.abs(z))
    return 0.5 * x * (1.0 + jnp.sign(z) * (1.0 - t) / (1.0 + t))


def _peer_sc(e2, g, h2, tab_u, tab_v):
    n_sc = e2.shape[0]
    assert n_sc % (2 * SC_WORKERS) == 0
    per_w = n_sc // SC_WORKERS
    mesh = plsc.VectorSubcoreMesh(core_axis_name="c", subcore_axis_name="s")
    ln = SC_LANES
    per_token_inputs = lambda: [pltpu.VMEM((SC_HALF,), jnp.int32), pltpu.VMEM((SC_HALF,), jnp.int32),
                                pltpu.VMEM((D_MODEL,), jnp.float32), pltpu.VMEM((N_PICKS,), jnp.float32)]

    @functools.partial(
        pl.kernel, mesh=mesh,
        out_type=jax.ShapeDtypeStruct((n_sc, D_MODEL), jnp.float32),
        scratch_types=per_token_inputs() + per_token_inputs() + [
            pltpu.VMEM((SC_HALF, ROW_WORDS), jnp.int32), pltpu.VMEM((SC_HALF, ROW_WORDS), jnp.int32),
            pltpu.VMEM((N_PICKS,), jnp.float32), pltpu.VMEM((D_MODEL,), jnp.float32),
            pltpu.SemaphoreType.DMA, pltpu.SemaphoreType.DMA,
            pltpu.SemaphoreType.DMA, pltpu.SemaphoreType.DMA],
        compiler_params=_sc_params(),
        cost_estimate=pl.CostEstimate(
            flops=4 * n_sc * N_PICKS * D_MODEL,
            transcendentals=n_sc * N_PICKS,
            bytes_accessed=n_sc * (2 * N_PICKS * ROW_WORDS * 4 + 3 * D_MODEL * 4 + 3 * N_PICKS * 4)),
    )
    def body(e_hbm, g_hbm, x_hbm, tu_hbm, tv_hbm, out_hbm,
             ia0, ib0, x0, g0, ia1, ib1, x1, g1, buf_a, buf_b, w_v, o_v, sem_a, sem_b, sem_in0, sem_in1):
        wid = lax.axis_index("s") * SC_CORES + lax.axis_index("c")
        base = wid * per_w
        lane = lax.iota(jnp.int32, ln)
        zero = jnp.zeros((ln,), jnp.float32)
        in_sets = ((ia0, ib0, x0, g0, sem_in0), (ia1, ib1, x1, g1, sem_in1))

        def input_copies(st, i):
            ia, ib, xv, gv, sem = st
            tl = base + jnp.minimum(i, per_w - 1)
            return [pltpu.make_async_copy(e_hbm.at[tl, 0], ia, sem),
                    pltpu.make_async_copy(e_hbm.at[tl, 1], ib, sem),
                    pltpu.make_async_copy(x_hbm.at[tl], xv, sem),
                    pltpu.make_async_copy(g_hbm.at[tl], gv, sem)]

        def u_half(buf, k0, x_v, g_v):
            @pl.loop(0, SC_HALF // ln)
            def _(grp):
                def chunk(j, accs):
                    xl = x_v[pl.ds(j * ln, ln)]
                    xh = x_v[pl.ds(ROW_WORDS + j * ln, ln)]
                    new = []
                    for kk in range(ln):
                        lo, hi = _unpack_row(buf[grp * ln + kk, pl.ds(j * ln, ln)])
                        new.append(accs[kk] + lo * xl + hi * xh)
                    return tuple(new)

                accs = lax.fori_loop(0, SC_CHUNKS, chunk, tuple(zero for _ in range(ln)))
                a = zero
                for kk in range(ln):
                    a = jnp.where(lane == kk, jnp.sum(accs[kk]), a)
                sl = pl.ds(k0 + grp * ln, ln)
                w_v[sl] = g_v[sl] * _gelu_tanh_via_exp(a)

        def v_half(buf, k0):
            @pl.loop(0, SC_CHUNKS // SC_ACC_CHUNKS)
            def _(jb):
                def pick(kq, accs):
                    wk = plsc.load_gather(w_v, [jnp.full((ln,), k0, jnp.int32) + kq])
                    new = []
                    for c in range(SC_ACC_CHUNKS):
                        lo, hi = _unpack_row(buf[kq, pl.ds((jb * SC_ACC_CHUNKS + c) * ln, ln)])
                        new.append(accs[2 * c] + wk * lo)
                        new.append(accs[2 * c + 1] + wk * hi)
                    return tuple(new)

                accs = lax.fori_loop(0, SC_HALF, pick, tuple(zero for _ in range(2 * SC_ACC_CHUNKS)))
                for c in range(SC_ACC_CHUNKS):
                    col = (jb * SC_ACC_CHUNKS + c) * ln
                    o_v[pl.ds(col, ln)] = o_v[pl.ds(col, ln)] + accs[2 * c]
                    o_v[pl.ds(ROW_WORDS + col, ln)] = o_v[pl.ds(ROW_WORDS + col, ln)] + accs[2 * c + 1]

        def token(i, cur, nxt):
            ia, ib, x_v, g_v, _ = cur
            for c in input_copies(nxt, i + 1):
                c.start()
            u_b = pltpu.async_copy(tu_hbm.at[ib], buf_b, sem_b)

            @pl.loop(0, D_MODEL // ln)
            def _(c):
                o_v[pl.ds(c * ln, ln)] = zero

            pltpu.make_async_copy(tu_hbm.at[ia], buf_a, sem_a).wait()
            u_half(buf_a, 0, x_v, g_v)
            v_a = pltpu.async_copy(tv_hbm.at[ia], buf_a, sem_a)
            u_b.wait()
            u_half(buf_b, SC_HALF, x_v, g_v)
            v_b = pltpu.async_copy(tv_hbm.at[ib], buf_b, sem_b)
            v_a.wait()
            v_half(buf_a, 0)
            for c in input_copies(nxt, i + 1):
                c.wait()
            pltpu.async_copy(tu_hbm.at[nxt[0]], buf_a, sem_a)
            v_b.wait()
            v_half(buf_b, SC_HALF)
            pltpu.sync_copy(o_v, out_hbm.at[base + i])

        for c in input_copies(in_sets[0], 0):
            c.start()
        for c in input_copies(in_sets[0], 0):
            c.wait()
        pltpu.async_copy(tu_hbm.at[ia0], buf_a, sem_a)

        @pl.loop(0, per_w // 2)
        def _(pair):
            token(2 * pair, in_sets[0], in_sets[1])
            token(2 * pair + 1, in_sets[1], in_sets[0])

        pltpu.make_async_copy(tu_hbm.at[ia0], buf_a, sem_a).wait()

    return body(e2, g, h2, tab_u, tab_v)


def _finish_kernel(x_ref, gate_ref, p_ref, y_ref):
    y_ref[...] = x_ref[...] + gate_ref[...] * p_ref[...]


def _peer_finish(x1, gate2, peer, seq):
    n = x1.shape[0]
    tt = TT_PEER
    row = pl.BlockSpec((tt, D_MODEL), lambda i: (i, 0))
    return pl.pallas_call(
        _finish_kernel,
        grid=(n // tt,),
        in_specs=[row, pl.BlockSpec((None, 1, D_MODEL), lambda i: ((i * tt) // seq, 0, 0)), row],
        out_specs=row,
        out_shape=jax.ShapeDtypeStruct((n, D_MODEL), jnp.float32),
        compiler_params=_cparams(("arbitrary",)),
        name="peer_finish",
    )(x1, gate2, peer)


def _pad_heads(w, used, lead=0):
    r = w.shape[0]
    w3 = w.reshape(r, N_HEADS, used)
    out = jnp.zeros((r, N_HEADS, HEAD_SLOT), w.dtype).at[:, :, lead:lead + used].set(w3)
    return out.reshape(r, N_HEADS * HEAD_SLOT)


def _pack_table(t):
    tb = t.astype(jnp.bfloat16)
    lo = lax.bitcast_convert_type(tb[:, :ROW_WORDS], jnp.uint16).astype(jnp.uint32)
    hi = lax.bitcast_convert_type(tb[:, ROW_WORDS:], jnp.uint16).astype(jnp.uint32)
    words = lax.bitcast_convert_type(lo | (hi << 16), jnp.int32)
    return words.reshape(t.shape[0] * ROW_SUB, LANES)


def _prep_weights(g_norm1, w_in, conv_w, w_conv_out, g_q_lora, w_uq, g_kv_lora, w_ukv, g_qnorm,
                  g_knorm, w_attn_out, w_out, g_norm2, peer_wq, peer_k1, peer_k2, peer_u, peer_v):
    bf = jnp.bfloat16
    o1 = 3 * D_CONV
    o2 = o1 + Q_LORA
    o3 = o2 + KV_LORA
    o4 = o3 + QK_ROPE
    o5 = o4 + D_MODEL
    kr_slot = jnp.zeros((D_MODEL, HEAD_SLOT), w_in.dtype).at[:, QK_NOPE:QK_HEAD].set(w_in[:, o3:o4])
    w_in_r = jnp.concatenate([w_in[:, :o3], kr_slot, w_in[:, o4:]], axis=1).astype(bf)
    w_ukv3 = w_ukv.reshape(KV_LORA, N_HEADS, QK_NOPE + V_HEAD)
    w_uk = _pad_heads(w_ukv3[:, :, :QK_NOPE].reshape(KV_LORA, -1), QK_NOPE)
    w_uv = _pad_heads(w_ukv3[:, :, QK_NOPE:].reshape(KV_LORA, -1), V_HEAD)
    pad_gain = lambda g: jnp.zeros((1, HEAD_SLOT), jnp.float32).at[0, :QK_HEAD].set(g)
    w_ao = jnp.zeros((N_HEADS, HEAD_SLOT, D_MODEL), w_attn_out.dtype).at[:, :V_HEAD].set(
        w_attn_out.reshape(N_HEADS, V_HEAD, D_MODEL)).reshape(N_HEADS * HEAD_SLOT, D_MODEL)
    half = PEER_DK // 2
    kcat = jnp.zeros((PEER_HEADS, 2 * N_KEYS, PEER_DK), jnp.float32)
    kcat = kcat.at[:, :N_KEYS, :half].set(peer_k1).at[:, N_KEYS:, half:].set(peer_k2)
    tab_u = _pack_table(peer_u)
    tab_v = _pack_table(peer_v)
    return {
        "g_norm1": g_norm1.reshape(1, -1), "w_in": w_in_r, "conv_w": conv_w,
        "w_conv_out": w_conv_out.astype(bf), "g_q_lora": g_q_lora.reshape(1, -1),
        "w_uq": _pad_heads(w_uq, QK_HEAD).astype(bf), "g_kv_lora": g_kv_lora.reshape(1, -1),
        "w_uk": w_uk.astype(bf), "w_uv": w_uv.astype(bf),
        "g_qnorm": pad_gain(g_qnorm), "g_knorm": pad_gain(g_knorm),
        "w_attn_out": w_ao.astype(bf), "w_out": w_out.astype(bf), "g_norm2": g_norm2.reshape(1, -1),
        "peer_wq": peer_wq.astype(bf), "kcat": kcat.astype(bf),
        "tab_u": tab_u, "tab_v": tab_v,
        "tab_u_sc": tab_u.reshape(-1, ROW_WORDS), "tab_v_sc": tab_v.reshape(-1, ROW_WORDS),
    }


def _rope_tables(seq):
    pos = jnp.arange(seq, dtype=jnp.float32)
    inv = ROPE_THETA ** (-jnp.arange(0, QK_ROPE, 2, dtype=jnp.float32) / QK_ROPE)
    ang = pos[:, None] * inv[None, :]
    cos, sin = jnp.cos(ang), jnp.sin(ang)
    t1 = slice(QK_NOPE, QK_NOPE + HALF_ROPE)
    t2 = slice(QK_NOPE + HALF_ROPE, QK_HEAD)
    cos_t = jnp.ones((seq, HEAD_SLOT), jnp.float32).at[:, t1].set(cos).at[:, t2].set(cos)
    sin_a = jnp.zeros((seq, HEAD_SLOT), jnp.float32).at[:, t1].set(-sin)
    sin_b = jnp.zeros((seq, HEAD_SLOT), jnp.float32).at[:, t2].set(sin)
    return cos_t, sin_a, sin_b


def _front(x, c, w_ada, b_ada, wts, after):
    bsz, seq, d = x.shape
    n = bsz * seq
    ada = _ada(c, w_ada, b_ada)[:, None, :]
    shift1, scale1, gate1, shift2, scale2, gate2 = jnp.split(ada, 6, axis=-1)
    q, k, v, a, sgb = _inproj(x, scale1, shift1, wts, _rope_tables(seq), after)
    o = _attention(q, k, v)
    x1, h2, e, g = _mix(x, a, sgb, o, gate1, scale2, shift2, wts)
    return dict(x1=x1.reshape(n, d), h2=h2.reshape(n, d), e=e.reshape(n, N_PICKS),
                g=g.reshape(n, N_PICKS), gate2=gate2, seq=seq)


def _peer_on_sparsecores(f, wts):
    ids = lax.shift_right_logical(f["e"], ROW_SUB.bit_length() - 1)
    peer = _peer_sc(ids.reshape(-1, 2, SC_HALF), f["g"], f["h2"], wts["tab_u_sc"], wts["tab_v_sc"])
    return _peer_finish(f["x1"], f["gate2"], peer, f["seq"])


def _peer_on_tensorcore(f, wts):
    e = f["e"].reshape(-1)
    w = _peer_u(e, f["h2"], f["g"], wts["tab_u"])
    return _peer_v(e, w, f["x1"], f["gate2"], wts["tab_v"], f["seq"])


def kernel(x_prompt, x_sample, c_prompt, c_sample, w_ada, b_ada, g_norm1, w_in, conv_w, w_conv_out, g_q_lora, w_uq, g_kv_lora, w_ukv, g_qnorm, g_knorm, w_attn_out, w_out, g_norm2, peer_wq, peer_k1, peer_k2, peer_u, peer_v):
    wts = _prep_weights(g_norm1[0], w_in[0], conv_w[0], w_conv_out[0], g_q_lora[0], w_uq[0],
                        g_kv_lora[0], w_ukv[0], g_qnorm[0], g_knorm[0], w_attn_out[0], w_out[0],
                        g_norm2[0], peer_wq[0], peer_k1[0], peer_k2[0], peer_u[0], peer_v[0])
    sc_tables = (wts["tab_u_sc"], wts["tab_v_sc"])
    fs = _front(x_sample, c_sample, w_ada[0], b_ada[0], wts, sc_tables)
    y_sample = _peer_on_sparsecores(fs, wts)
    fp = _front(x_prompt, c_prompt, w_ada[0], b_ada[0], wts, sc_tables + (fs["g"],))
    y_prompt = _peer_on_tensorcore(fp, wts)
    return y_prompt.reshape(x_prompt.shape), y_sample.reshape(x_sample.shape)
```

```python
import dataclasses
import functools
import math

import jax
import jax.numpy as jnp
import numpy as np
from jax import lax
from jax.experimental import pallas as pl
from jax.experimental.pallas import tpu as pltpu
from jax.experimental.pallas import tpu_sc as plsc

D_MODEL = 1024
D_CONV = 512
N_HEADS = 8
QK_NOPE = 64
QK_ROPE = 32
V_HEAD = 64
Q_LORA = 256
KV_LORA = 128
QK_HEAD = QK_NOPE + QK_ROPE
ROPE_THETA = 10000.0
PEER_HEADS = 8
N_KEYS = 128
PEER_DK = 128
PEER_TOPK = 16
EPS = 1e-6

LANES = 128
SUBLANES = 8
HEAD_SLOT = LANES
HALF_ROPE = QK_ROPE // 2
N_PICKS = PEER_HEADS * PEER_TOPK
ROW_WORDS = D_MODEL // 2
ROW_SUB = ROW_WORDS // LANES
VMEM_LIMIT = 56 * 1024 * 1024

C_BG, C_CG, C_HC = 0, 512, 1024
C_CQ = 1536
C_CKV = C_CQ + Q_LORA
C_KR = C_CKV + KV_LORA
C_GA = C_KR + HEAD_SLOT
C_GB = C_GA + D_MODEL
IN_COLS_R = C_GB + D_MODEL

TS_IN = 512
TQ = 512
TK = 2048
KV_UNROLL = 2
TS_MIX = 256
TT_PEER = 128
TOK_UNROLL_U = 32
TOK_UNROLL_V = 32
IDX_GROUP = 8

_NEG_INF = float("-inf")


def _cparams(sem):
    return pltpu.CompilerParams(dimension_semantics=sem, vmem_limit_bytes=VMEM_LIMIT)


def _ada_kernel(c_ref, w_ref, b_ref, o_ref):
    o_ref[...] = jnp.dot(c_ref[...], w_ref[...], preferred_element_type=jnp.float32,
                         precision=lax.Precision.HIGHEST) + b_ref[...]


def _ada(c, w_ada, b_ada):
    bsz = c.shape[0]
    rows = -(-bsz // SUBLANES) * SUBLANES
    c_p = jnp.zeros((rows, D_MODEL), jnp.float32).at[:bsz].set(c)
    out = pl.pallas_call(
        _ada_kernel,
        grid=(6,),
        in_specs=[pl.BlockSpec((rows, D_MODEL), lambda j: (0, 0)),
                  pl.BlockSpec((D_MODEL, D_MODEL), lambda j: (0, j)),
                  pl.BlockSpec((1, D_MODEL), lambda j: (0, j))],
        out_specs=pl.BlockSpec((rows, D_MODEL), lambda j: (0, j)),
        out_shape=jax.ShapeDtypeStruct((rows, 6 * D_MODEL), jnp.float32),
        compiler_params=_cparams(("arbitrary",)),
        name="ada",
    )(c_p, w_ada, b_ada.reshape(1, -1))
    return out[:bsz]


def _adaln(x, g, scale, shift):
    ms = jnp.mean(x * x, axis=-1, keepdims=True)
    return (x * lax.rsqrt(ms + EPS) * g) * (1.0 + scale) + shift


def _rms_rows(x, g, n):
    ms = jnp.sum(x * x, axis=-1, keepdims=True) * (1.0 / n)
    return x * lax.rsqrt(ms + EPS) * g


def _rope(x, cos, sin_a, sin_b):
    return (x * cos + pltpu.roll(x, LANES - HALF_ROPE, axis=1) * sin_a
            + pltpu.roll(x, HALF_ROPE, axis=1) * sin_b)


def _inproj_kernel(x_ref, xp_ref, xn_ref, scale_ref, shift_ref, g1_ref, w_in_ref, conv_w_ref,
                   w_co_ref, gql_ref, w_uq_ref, gkvl_ref, w_uk_ref, w_uv_ref, gqn_ref, gkn_ref,
                   cos_ref, sa_ref, sb_ref, *rest):
    q_ref, k_ref, v_ref, a_ref, sgb_ref = rest[-5:]
    i = pl.program_id(1)
    n_i = pl.num_programs(1)
    g1 = g1_ref[...]
    scale = scale_ref[...]
    shift = shift_ref[...]
    ts = x_ref.shape[0]

    h = _adaln(x_ref[...], g1, scale, shift).astype(jnp.bfloat16)

    def proj(lo, width):
        return jnp.dot(h, w_in_ref[:, lo:lo + width], preferred_element_type=jnp.float32)

    z = proj(C_CG, D_CONV) * proj(C_HC, D_CONV)

    def halo_z(xh_ref):
        hh = _adaln(xh_ref[...], g1, scale, shift).astype(jnp.bfloat16)
        zc = jnp.dot(hh, w_in_ref[:, C_CG:C_CG + 2 * D_CONV], preferred_element_type=jnp.float32)
        return zc[:, :D_CONV] * zc[:, D_CONV:]

    z_prev = halo_z(xp_ref)[SUBLANES - 1:SUBLANES, :]
    z_next = halo_z(xn_ref)[0:1, :]
    z_prev = jnp.where(i == 0, 0.0, z_prev)
    z_next = jnp.where(i == n_i - 1, 0.0, z_next)
    row = lax.broadcasted_iota(jnp.int32, (ts, D_CONV), 0)
    z_up = jnp.where(row == 0, z_prev, pltpu.roll(z, 1, axis=0))
    z_dn = jnp.where(row == ts - 1, z_next, pltpu.roll(z, ts - 1, axis=0))
    cw = conv_w_ref[...]
    y = z_up * cw[0:1, :] + z * cw[1:2, :] + z_dn * cw[2:3, :]
    out_a = jnp.dot((proj(C_BG, D_CONV) * y).astype(jnp.bfloat16), w_co_ref[...],
                    preferred_element_type=jnp.float32)
    a_ref[...] = jax.nn.sigmoid(proj(C_GA, D_MODEL)) * out_a
    sgb_ref[...] = jax.nn.sigmoid(proj(C_GB, D_MODEL))

    cos = cos_ref[...]
    sin_a = sa_ref[...]
    sin_b = sb_ref[...]
    cq = _rms_rows(proj(C_CQ, Q_LORA), gql_ref[...], Q_LORA).astype(jnp.bfloat16)
    qf = jnp.dot(cq, w_uq_ref[...], preferred_element_type=jnp.float32)
    ckv = _rms_rows(proj(C_CKV, KV_LORA), gkvl_ref[...], KV_LORA).astype(jnp.bfloat16)
    kf = jnp.dot(ckv, w_uk_ref[...], preferred_element_type=jnp.float32)
    lane = lax.broadcasted_iota(jnp.int32, (1, N_HEADS * HEAD_SLOT), 1)
    ones_lane = (jnp.bitwise_and(lane, HEAD_SLOT - 1) == V_HEAD).astype(jnp.float32)
    v_ref[...] = (jnp.dot(ckv, w_uv_ref[...], preferred_element_type=jnp.float32)
                  + ones_lane).astype(jnp.bfloat16)
    kr = proj(C_KR, HEAD_SLOT)
    gqn = gqn_ref[...]
    gkn = gkn_ref[...]
    q_scale = QK_HEAD ** -0.5 * math.log2(math.e)
    for hd in range(N_HEADS):
        sl = slice(hd * HEAD_SLOT, (hd + 1) * HEAD_SLOT)
        qh = _rope(_rms_rows(qf[:, sl], gqn, QK_HEAD), cos, sin_a, sin_b)
        q_ref[:, sl] = (qh * q_scale).astype(jnp.bfloat16)
        kh = _rope(_rms_rows(kf[:, sl] + kr, gkn, QK_HEAD), cos, sin_a, sin_b)
        k_ref[:, sl] = kh.astype(jnp.bfloat16)


def _inproj(x, scale1, shift1, wts, rope, after):
    bsz, seq, _ = x.shape
    ts = min(TS_IN, seq)
    n_i = seq // ts
    nb8 = seq // SUBLANES
    per8 = ts // SUBLANES
    tile = lambda b, i: (b, i, 0)
    per_b = lambda b, i: (b, 0, 0)
    full2 = lambda b, i: (0, 0)
    hw = N_HEADS * HEAD_SLOT
    in_specs = [
        pl.BlockSpec((None, ts, D_MODEL), tile),
        pl.BlockSpec((None, SUBLANES, D_MODEL), lambda b, i: (b, jnp.maximum(i * per8 - 1, 0), 0)),
        pl.BlockSpec((None, SUBLANES, D_MODEL), lambda b, i: (b, jnp.minimum((i + 1) * per8, nb8 - 1), 0)),
        pl.BlockSpec((None, 1, D_MODEL), per_b),
        pl.BlockSpec((None, 1, D_MODEL), per_b),
        pl.BlockSpec((1, D_MODEL), full2),
        pl.BlockSpec((D_MODEL, IN_COLS_R), full2),
        pl.BlockSpec((3, D_CONV), full2),
        pl.BlockSpec((D_CONV, D_MODEL), full2),
        pl.BlockSpec((1, Q_LORA), full2),
        pl.BlockSpec((Q_LORA, hw), full2),
        pl.BlockSpec((1, KV_LORA), full2),
        pl.BlockSpec((KV_LORA, hw), full2),
        pl.BlockSpec((KV_LORA, hw), full2),
        pl.BlockSpec((1, HEAD_SLOT), full2),
        pl.BlockSpec((1, HEAD_SLOT), full2),
        pl.BlockSpec((ts, HEAD_SLOT), lambda b, i: (i, 0)),
        pl.BlockSpec((ts, HEAD_SLOT), lambda b, i: (i, 0)),
        pl.BlockSpec((ts, HEAD_SLOT), lambda b, i: (i, 0)),
    ] + [pl.BlockSpec(memory_space=pl.ANY)] * len(after)
    out_specs = [pl.BlockSpec((None, ts, hw), tile)] * 3 + [pl.BlockSpec((None, ts, D_MODEL), tile)] * 2
    out_shape = ([jax.ShapeDtypeStruct((bsz, seq, hw), jnp.bfloat16)] * 3
                 + [jax.ShapeDtypeStruct((bsz, seq, D_MODEL), jnp.float32)] * 2)
    return pl.pallas_call(
        _inproj_kernel,
        grid=(bsz, n_i),
        in_specs=in_specs,
        out_specs=out_specs,
        out_shape=out_shape,
        compiler_params=_cparams(("parallel", "parallel")),
        name="inproj",
    )(x, x, x, scale1, shift1, wts["g_norm1"], wts["w_in"], wts["conv_w"], wts["w_conv_out"],
      wts["g_q_lora"], wts["w_uq"], wts["g_kv_lora"], wts["w_uk"], wts["w_uv"],
      wts["g_qnorm"], wts["g_knorm"], rope[0], rope[1], rope[2], *after)


def _attn_kernel(q_ref, k_ref, v_ref, o_ref):
    q = q_ref[...]
    tq = q.shape[0]
    seq = k_ref.shape[0]
    tk = min(TK, seq)

    n_chunks = seq // tk
    unroll = KV_UNROLL if n_chunks % KV_UNROLL == 0 else 1

    def chunk(off, carry):
        m, acc = carry
        kc = k_ref[pl.ds(off, tk), :]
        vc = v_ref[pl.ds(off, tk), :]
        s = lax.dot_general(q, kc, (((1,), (1,)), ((), ())), preferred_element_type=jnp.float32)
        m_new = jnp.maximum(m, jnp.max(s, axis=-1, keepdims=True))
        p = jnp.exp2(s - m_new).astype(jnp.bfloat16)
        alpha = jnp.exp2(m - m_new)
        acc = alpha * acc + jnp.dot(p, vc, preferred_element_type=jnp.float32)
        return m_new, acc

    def body(j, carry):
        for u in range(unroll):
            carry = chunk(pl.multiple_of((j * unroll + u) * tk, tk), carry)
        return carry

    m0 = jnp.full((tq, 1), _NEG_INF, jnp.float32)
    acc0 = jnp.zeros((tq, HEAD_SLOT), jnp.float32)
    _, acc = lax.fori_loop(0, n_chunks // unroll, body, (m0, acc0))
    o_ref[...] = (acc / acc[:, V_HEAD:V_HEAD + 1]).astype(jnp.bfloat16)


def _attention(q, k, v):
    bsz, seq, hw = q.shape
    tq = min(TQ, seq)
    return pl.pallas_call(
        _attn_kernel,
        grid=(bsz, N_HEADS, seq // tq),
        in_specs=[pl.BlockSpec((None, tq, HEAD_SLOT), lambda b, h, i: (b, i, h)),
                  pl.BlockSpec((None, seq, HEAD_SLOT), lambda b, h, i: (b, 0, h)),
                  pl.BlockSpec((None, seq, HEAD_SLOT), lambda b, h, i: (b, 0, h))],
        out_specs=pl.BlockSpec((None, tq, HEAD_SLOT), lambda b, h, i: (b, i, h)),
        out_shape=jax.ShapeDtypeStruct((bsz, seq, hw), jnp.bfloat16),
        compiler_params=_cparams(("parallel", "parallel", "arbitrary")),
        name="attn",
    )(q, k, v)


def _topk_rows(s, iota, iota_blocks):
    vals, idxs = [], []
    big = jnp.float32(1e9)
    n_blk = s.shape[0] // SUBLANES
    for _ in range(PEER_TOPK):
        vb = [s[j * SUBLANES:(j + 1) * SUBLANES, :] for j in range(n_blk)]
        ib = list(iota_blocks)
        while len(vb) > 1:
            nv, ni = [], []
            for a in range(0, len(vb) - 1, 2):
                keep = vb[a] >= vb[a + 1]
                nv.append(jnp.where(keep, vb[a], vb[a + 1]))
                ni.append(jnp.where(keep, ib[a], ib[a + 1]))
            if len(vb) % 2:
                nv.append(vb[-1])
                ni.append(ib[-1])
            vb, ib = nv, ni
        m = jnp.max(vb[0], axis=0, keepdims=True)
        am = jnp.min(jnp.where(vb[0] == m, ib[0], big), axis=0, keepdims=True)
        vals.append(m)
        idxs.append(am)
        s = jnp.where(iota == am, _NEG_INF, s)
    return jnp.concatenate(vals, axis=0), jnp.concatenate(idxs, axis=0)


def _take_rows(tab, idx):
    out = jnp.zeros_like(tab)
    for a in range(tab.shape[0]):
        out = out + jnp.where(idx == jnp.float32(a), tab[a:a + 1, :], 0.0)
    return out


_CAND = [(a, b) for a in range(PEER_TOPK) for b in range(PEER_TOPK) if (a + 1) * (b + 1) <= PEER_TOPK]
N_CAND = -(-len(_CAND) // SUBLANES) * SUBLANES
_PAD_LABEL = float(PEER_TOPK * PEER_TOPK)


def _candidate_tables():
    sel = np.zeros((2, N_CAND, PEER_TOPK), np.float32)
    lab = np.full((N_CAND, 1), _PAD_LABEL, np.float32)
    for r, (a, b) in enumerate(_CAND):
        sel[0, r, a] = 1.0
        sel[1, r, b] = 1.0
        lab[r, 0] = a * PEER_TOPK + b
    return sel, lab


def _mix_kernel(x_ref, a_ref, sgb_ref, o_ref, gate1_ref, scale2_ref, shift2_ref, g2_ref,
                w_ao_ref, w_out_ref, w_pq_ref, kcat_ref, sel_ref, lab_ref,
                x1_ref, h2_ref, e_ref, g_ref):
    ts = x_ref.shape[0]
    out_b = jnp.dot(o_ref[...], w_ao_ref[...], preferred_element_type=jnp.float32)
    merged = (a_ref[...] + sgb_ref[...] * out_b).astype(jnp.bfloat16)
    x1 = x_ref[...] + gate1_ref[...] * jnp.dot(merged, w_out_ref[...],
                                              preferred_element_type=jnp.float32)
    x1_ref[...] = x1
    h2 = _adaln(x1, g2_ref[...], scale2_ref[...], shift2_ref[...])
    h2_ref[...] = h2
    qp = jnp.dot(h2.astype(jnp.bfloat16), w_pq_ref[...],
                 preferred_element_type=jnp.float32).astype(jnp.bfloat16)

    iota_k = lax.broadcasted_iota(jnp.int32, (N_KEYS, ts), 0).astype(jnp.float32)
    iota8 = lax.broadcasted_iota(jnp.int32, (SUBLANES, ts), 0).astype(jnp.float32)
    iota_k_blocks = [iota8 + float(SUBLANES * j) for j in range(N_KEYS // SUBLANES)]
    lab = lab_ref[...]
    lab_blocks = [lab_ref[j * SUBLANES:(j + 1) * SUBLANES, :] for j in range(N_CAND // SUBLANES)]
    valid = lab < _PAD_LABEL
    sel_a = sel_ref[0]
    sel_b = sel_ref[1]
    exact = dict(preferred_element_type=jnp.float32, precision=lax.Precision.HIGHEST)
    e_rows, g_rows = [], []
    for hd in range(PEER_HEADS):
        qh = qp[:, hd * PEER_DK:(hd + 1) * PEER_DK]
        st = lax.dot_general(kcat_ref[hd], qh, (((1,), (1,)), ((), ())),
                             preferred_element_type=jnp.float32)
        v1, i1 = _topk_rows(st[:N_KEYS], iota_k, iota_k_blocks)
        v2, i2 = _topk_rows(st[N_KEYS:], iota_k, iota_k_blocks)
        comb = jnp.dot(sel_a, v1, **exact) + jnp.dot(sel_b, v2, **exact)
        sc, pos = _topk_rows(jnp.where(valid, comb, _NEG_INF), lab, lab_blocks)
        ia = jnp.floor(pos * (1.0 / PEER_TOPK))
        ib = pos - ia * PEER_TOPK
        e_rows.append(_take_rows(i1, ia) * N_KEYS + _take_rows(i2, ib))
        p = jnp.exp(sc - sc[0:1, :])
        g_rows.append(p / jnp.sum(p, axis=0, keepdims=True))
    e_all = jnp.concatenate(e_rows, axis=0)
    g_all = jnp.concatenate(g_rows, axis=0)
    for c in range(ts // LANES):
        cs = slice(c * LANES, (c + 1) * LANES)
        e_ref[cs, :] = (e_all[:, cs].T * ROW_SUB).astype(jnp.int32)
        g_ref[cs, :] = g_all[:, cs].T


def _mix(x, a, sgb, o, gate1, scale2, shift2, wts):
    bsz, seq, _ = x.shape
    ts = min(TS_MIX, seq)
    tile = lambda b, i: (b, i, 0)
    per_b = lambda b, i: (b, 0, 0)
    full2 = lambda b, i: (0, 0)
    hw = N_HEADS * HEAD_SLOT
    big = pl.BlockSpec((None, ts, D_MODEL), tile)
    vec = pl.BlockSpec((None, 1, D_MODEL), per_b)
    picks = pl.BlockSpec((None, ts, N_PICKS), tile)
    sel, lab = _candidate_tables()
    return pl.pallas_call(
        _mix_kernel,
        grid=(bsz, seq // ts),
        in_specs=[big, big, big, pl.BlockSpec((None, ts, hw), tile), vec, vec, vec,
                  pl.BlockSpec((1, D_MODEL), full2),
                  pl.BlockSpec((hw, D_MODEL), full2),
                  pl.BlockSpec((D_MODEL, D_MODEL), full2),
                  pl.BlockSpec((D_MODEL, PEER_HEADS * PEER_DK), full2),
                  pl.BlockSpec((PEER_HEADS, 2 * N_KEYS, PEER_DK), lambda b, i: (0, 0, 0)),
                  pl.BlockSpec((2, N_CAND, PEER_TOPK), lambda b, i: (0, 0, 0)),
                  pl.BlockSpec((N_CAND, ts), full2)],
        out_specs=[big, big, picks, picks],
        out_shape=[jax.ShapeDtypeStruct((bsz, seq, D_MODEL), jnp.float32),
                   jax.ShapeDtypeStruct((bsz, seq, D_MODEL), jnp.float32),
                   jax.ShapeDtypeStruct((bsz, seq, N_PICKS), jnp.int32),
                   jax.ShapeDtypeStruct((bsz, seq, N_PICKS), jnp.float32)],
        compiler_params=_cparams(("parallel", "parallel")),
        name="mix",
    )(x, a, sgb, o, gate1, scale2, shift2, wts["g_norm2"], wts["w_attn_out"], wts["w_out"],
      wts["peer_wq"], wts["kcat"], jnp.asarray(sel), jnp.broadcast_to(jnp.asarray(lab), (N_CAND, ts)))


def _unpack_row(w):
    lo = lax.bitcast_convert_type(lax.shift_left(w, 16), jnp.float32)
    hi = lax.bitcast_convert_type(jnp.bitwise_and(w, jnp.int32(-65536)), jnp.float32)
    return lo, hi


def _gelu_tanh(x):
    c = math.sqrt(2.0 / math.pi)
    return 0.5 * x * (1.0 + jnp.tanh(c * (x + 0.044715 * (x * x * x))))


def _gather_rows(e_ref, t, tab_ref, pb):
    for j in range(N_PICKS // IDX_GROUP):
        row = e_ref.at[pl.ds(pl.multiple_of(t * N_PICKS + j * IDX_GROUP, IDX_GROUP), IDX_GROUP)]
        for c in range(IDX_GROUP):
            k = j * IDX_GROUP + c
            r = pl.multiple_of(row[c], ROW_SUB)
            pb[k * ROW_SUB:(k + 1) * ROW_SUB, :] = tab_ref[pl.ds(r, ROW_SUB), :]


def _pick_block(pb, s):
    return pb[pl.ds(s, N_PICKS, stride=ROW_SUB), :]


def _peer_u_kernel(e_ref, x_ref, g_ref, tab_ref, w_ref, pbuf0, pbuf1, abuf):
    tt = g_ref.shape[0]
    pbufs = (pbuf0, pbuf1)
    lane = lax.broadcasted_iota(jnp.int32, (N_PICKS, tt), 1)
    abuf[...] = jnp.zeros_like(abuf)

    def tok(i, carry):
        x_all = x_ref[pl.ds(pl.multiple_of(i * TOK_UNROLL_U, TOK_UNROLL_U), TOK_UNROLL_U), :]
        for u in range(TOK_UNROLL_U):
            t = i * TOK_UNROLL_U + u
            pb = pbufs[u % 2]
            _gather_rows(e_ref, t, tab_ref, pb)
            acc = None
            for s in range(ROW_SUB):
                lo, hi = _unpack_row(_pick_block(pb, s))
                x_lo = x_all[u:u + 1, s * LANES:(s + 1) * LANES]
                x_hi = x_all[u:u + 1, ROW_WORDS + s * LANES:ROW_WORDS + (s + 1) * LANES]
                term = lo * x_lo + hi * x_hi
                acc = term if acc is None else acc + term
            a_col = jnp.sum(acc, axis=1, keepdims=True)
            abuf[...] = jnp.where(lane == t, a_col, abuf[...])
        return carry

    lax.fori_loop(0, tt // TOK_UNROLL_U, tok, 0)
    w_ref[...] = g_ref[...] * _gelu_tanh(abuf[...].T)


def _peer_u(e, h2, g, tab):
    n = h2.shape[0]
    tt = TT_PEER
    return pl.pallas_call(
        _peer_u_kernel,
        grid=(n // tt,),
        in_specs=[pl.BlockSpec((tt * N_PICKS,), lambda i: (i,), memory_space=pltpu.SMEM),
                  pl.BlockSpec((tt, D_MODEL), lambda i: (i, 0)),
                  pl.BlockSpec((tt, N_PICKS), lambda i: (i, 0)),
                  pl.BlockSpec(memory_space=pltpu.VMEM)],
        out_specs=pl.BlockSpec((tt, N_PICKS), lambda i: (i, 0)),
        out_shape=jax.ShapeDtypeStruct((n, N_PICKS), jnp.float32),
        scratch_shapes=[pltpu.VMEM((N_PICKS * ROW_SUB, LANES), jnp.int32),
                        pltpu.VMEM((N_PICKS * ROW_SUB, LANES), jnp.int32),
                        pltpu.VMEM((N_PICKS, tt), jnp.float32)],
        compiler_params=_cparams(("arbitrary",)),
        name="peer_u",
    )(e, h2, g, tab)


def _peer_v_kernel(e_ref, w_ref, x_ref, gate_ref, tab_ref, y_ref, pbuf0, pbuf1, wt_ref, acc_ref):
    tt = x_ref.shape[0]
    pbufs = (pbuf0, pbuf1)
    wt_ref[...] = w_ref[...].T
    lane = lax.broadcasted_iota(jnp.int32, (N_PICKS, tt), 1)
    sub = lax.broadcasted_iota(jnp.int32, (SUBLANES, D_MODEL), 0)
    acc_ref[...] = jnp.zeros_like(acc_ref)

    def process(pb, t):
        w_col = jnp.sum(jnp.where(lane == t, wt_ref[...], 0.0), axis=1, keepdims=True)
        los, his = [], []
        for s in range(ROW_SUB):
            lo, hi = _unpack_row(_pick_block(pb, s))
            los.append(jnp.sum(lo * w_col, axis=0, keepdims=True))
            his.append(jnp.sum(hi * w_col, axis=0, keepdims=True))
        row = jnp.concatenate(los + his, axis=1)
        b0 = pl.multiple_of(jnp.bitwise_and(t, -SUBLANES), SUBLANES)
        mask = sub == jnp.bitwise_and(t, SUBLANES - 1)
        acc_ref[pl.ds(b0, SUBLANES), :] = jnp.where(mask, row, acc_ref[pl.ds(b0, SUBLANES), :])

    def step(i, carry):
        t0 = i * TOK_UNROLL_V
        for u in range(TOK_UNROLL_V):
            _gather_rows(e_ref, t0 + u, tab_ref, pbufs[u % 2])
            process(pbufs[u % 2], t0 + u)
        return carry

    lax.fori_loop(0, tt // TOK_UNROLL_V, step, 0)
    y_ref[...] = x_ref[...] + gate_ref[...] * acc_ref[...]


def _peer_v(e, w, x1, gate2, tab, seq):
    n = x1.shape[0]
    tt = TT_PEER
    return pl.pallas_call(
        _peer_v_kernel,
        grid=(n // tt,),
        in_specs=[pl.BlockSpec((tt * N_PICKS,), lambda i: (i,), memory_space=pltpu.SMEM),
                  pl.BlockSpec((tt, N_PICKS), lambda i: (i, 0)),
                  pl.BlockSpec((tt, D_MODEL), lambda i: (i, 0)),
                  pl.BlockSpec((None, 1, D_MODEL), lambda i: ((i * tt) // seq, 0, 0)),
                  pl.BlockSpec(memory_space=pltpu.VMEM)],
        out_specs=pl.BlockSpec((tt, D_MODEL), lambda i: (i, 0)),
        out_shape=jax.ShapeDtypeStruct((n, D_MODEL), jnp.float32),
        scratch_shapes=[pltpu.VMEM((N_PICKS * ROW_SUB, LANES), jnp.int32),
                        pltpu.VMEM((N_PICKS * ROW_SUB, LANES), jnp.int32),
                        pltpu.VMEM((N_PICKS, tt), jnp.float32),
                        pltpu.VMEM((tt, D_MODEL), jnp.float32)],
        compiler_params=_cparams(("arbitrary",)),
        name="peer_v",
    )(e, w, x1, gate2, tab)


SC_CORES = 2
SC_SUBCORES = 16
SC_LANES = 16
SC_WORKERS = SC_CORES * SC_SUBCORES
SC_HALF = N_PICKS // 2
SC_CHUNKS = ROW_WORDS // SC_LANES
SC_ACC_CHUNKS = 8


def _sc_params():
    cp = pltpu.CompilerParams()
    if "needs_layout_passes" in pltpu.CompilerParams.__dataclass_fields__:
        cp = dataclasses.replace(cp, needs_layout_passes=False)
    return cp


def _gelu_tanh_via_exp(x):
    c = math.sqrt(2.0 / math.pi)
    z = c * (x + 0.044715 * (x * x * x))
    t = jnp.exp(-2.0 * jnp.abs(z))
    return 0.5 * x * (1.0 + jnp.sign(z) * (1.0 - t) / (1.0 + t))


def _peer_sc(e2, g, h2, tab_u, tab_v):
    n_sc = e2.shape[0]
    assert n_sc % (2 * SC_WORKERS) == 0
    per_w = n_sc // SC_WORKERS
    mesh = plsc.VectorSubcoreMesh(core_axis_name="c", subcore_axis_name="s")
    ln = SC_LANES
    per_token_inputs = lambda: [pltpu.VMEM((SC_HALF,), jnp.int32), pltpu.VMEM((SC_HALF,), jnp.int32),
                                pltpu.VMEM((D_MODEL,), jnp.float32), pltpu.VMEM((N_PICKS,), jnp.float32)]

    @functools.partial(
        pl.kernel, mesh=mesh,
        out_type=jax.ShapeDtypeStruct((n_sc, D_MODEL), jnp.float32),
        scratch_types=per_token_inputs() + per_token_inputs() + [
            pltpu.VMEM((SC_HALF, ROW_WORDS), jnp.int32), pltpu.VMEM((SC_HALF, ROW_WORDS), jnp.int32),
            pltpu.VMEM((N_PICKS,), jnp.float32), pltpu.VMEM((D_MODEL,), jnp.float32),
            pltpu.SemaphoreType.DMA, pltpu.SemaphoreType.DMA,
            pltpu.SemaphoreType.DMA, pltpu.SemaphoreType.DMA],
        compiler_params=_sc_params(),
        cost_estimate=pl.CostEstimate(
            flops=4 * n_sc * N_PICKS * D_MODEL,
            transcendentals=n_sc * N_PICKS,
            bytes_accessed=n_sc * (2 * N_PICKS * ROW_WORDS * 4 + 3 * D_MODEL * 4 + 3 * N_PICKS * 4)),
    )
    def body(e_hbm, g_hbm, x_hbm, tu_hbm, tv_hbm, out_hbm,
             ia0, ib0, x0, g0, ia1, ib1, x1, g1, buf_a, buf_b, w_v, o_v, sem_a, sem_b, sem_in0, sem_in1):
        wid = lax.axis_index("s") * SC_CORES + lax.axis_index("c")
        base = wid * per_w
        lane = lax.iota(jnp.int32, ln)
        zero = jnp.zeros((ln,), jnp.float32)
        in_sets = ((ia0, ib0, x0, g0, sem_in0), (ia1, ib1, x1, g1, sem_in1))

        def input_copies(st, i):
            ia, ib, xv, gv, sem = st
            tl = base + jnp.minimum(i, per_w - 1)
            return [pltpu.make_async_copy(e_hbm.at[tl, 0], ia, sem),
                    pltpu.make_async_copy(e_hbm.at[tl, 1], ib, sem),
                    pltpu.make_async_copy(x_hbm.at[tl], xv, sem),
                    pltpu.make_async_copy(g_hbm.at[tl], gv, sem)]

        def u_half(buf, k0, x_v, g_v):
            @pl.loop(0, SC_HALF // ln)
            def _(grp):
                def chunk(j, accs):
                    xl = x_v[pl.ds(j * ln, ln)]
                    xh = x_v[pl.ds(ROW_WORDS + j * ln, ln)]
                    new = []
                    for kk in range(ln):
                        lo, hi = _unpack_row(buf[grp * ln + kk, pl.ds(j * ln, ln)])
                        new.append(accs[kk] + lo * xl + hi * xh)
                    return tuple(new)

                accs = lax.fori_loop(0, SC_CHUNKS, chunk, tuple(zero for _ in range(ln)))
                a = zero
                for kk in range(ln):
                    a = jnp.where(lane == kk, jnp.sum(accs[kk]), a)
                sl = pl.ds(k0 + grp * ln, ln)
                w_v[sl] = g_v[sl] * _gelu_tanh_via_exp(a)

        def v_half(buf, k0):
            @pl.loop(0, SC_CHUNKS // SC_ACC_CHUNKS)
            def _(jb):
                def pick(kq, accs):
                    wk = plsc.load_gather(w_v, [jnp.full((ln,), k0, jnp.int32) + kq])
                    new = []
                    for c in range(SC_ACC_CHUNKS):
                        lo, hi = _unpack_row(buf[kq, pl.ds((jb * SC_ACC_CHUNKS + c) * ln, ln)])
                        new.append(accs[2 * c] + wk * lo)
                        new.append(accs[2 * c + 1] + wk * hi)
                    return tuple(new)

                accs = lax.fori_loop(0, SC_HALF, pick, tuple(zero for _ in range(2 * SC_ACC_CHUNKS)))
                for c in range(SC_ACC_CHUNKS):
                    col = (jb * SC_ACC_CHUNKS + c) * ln
                    o_v[pl.ds(col, ln)] = o_v[pl.ds(col, ln)] + accs[2 * c]
                    o_v[pl.ds(ROW_WORDS + col, ln)] = o_v[pl.ds(ROW_WORDS + col, ln)] + accs[2 * c + 1]

        def token(i, cur, nxt):
            ia, ib, x_v, g_v, _ = cur
            for c in input_copies(nxt, i + 1):
                c.start()
            u_b = pltpu.async_copy(tu_hbm.at[ib], buf_b, sem_b)

            @pl.loop(0, D_MODEL // ln)
            def _(c):
                o_v[pl.ds(c * ln, ln)] = zero

            pltpu.make_async_copy(tu_hbm.at[ia], buf_a, sem_a).wait()
            u_half(buf_a, 0, x_v, g_v)
            v_a = pltpu.async_copy(tv_hbm.at[ia], buf_a, sem_a)
            u_b.wait()
            u_half(buf_b, SC_HALF, x_v, g_v)
            v_b = pltpu.async_copy(tv_hbm.at[ib], buf_b, sem_b)
            v_a.wait()
            v_half(buf_a, 0)
            for c in input_copies(nxt, i + 1):
                c.wait()
            pltpu.async_copy(tu_hbm.at[nxt[0]], buf_a, sem_a)
            v_b.wait()
            v_half(buf_b, SC_HALF)
            pltpu.sync_copy(o_v, out_hbm.at[base + i])

        for c in input_copies(in_sets[0], 0):
            c.start()
        for c in input_copies(in_sets[0], 0):
            c.wait()
        pltpu.async_copy(tu_hbm.at[ia0], buf_a, sem_a)

        @pl.loop(0, per_w // 2)
        def _(pair):
            token(2 * pair, in_sets[0], in_sets[1])
            token(2 * pair + 1, in_sets[1], in_sets[0])

        pltpu.make_async_copy(tu_hbm.at[ia0], buf_a, sem_a).wait()

    return body(e2, g, h2, tab_u, tab_v)


def _finish_kernel(x_ref, gate_ref, p_ref, y_ref):
    y_ref[...] = x_ref[...] + gate_ref[...] * p_ref[...]


def _peer_finish(x1, gate2, peer, seq):
    n = x1.shape[0]
    tt = TT_PEER
    row = pl.BlockSpec((tt, D_MODEL), lambda i: (i, 0))
    return pl.pallas_call(
        _finish_kernel,
        grid=(n // tt,),
        in_specs=[row, pl.BlockSpec((None, 1, D_MODEL), lambda i: ((i * tt) // seq, 0, 0)), row],
        out_specs=row,
        out_shape=jax.ShapeDtypeStruct((n, D_MODEL), jnp.float32),
        compiler_params=_cparams(("arbitrary",)),
        name="peer_finish",
    )(x1, gate2, peer)


def _pad_heads(w, used, lead=0):
    r = w.shape[0]
    w3 = w.reshape(r, N_HEADS, used)
    out = jnp.zeros((r, N_HEADS, HEAD_SLOT), w.dtype).at[:, :, lead:lead + used].set(w3)
    return out.reshape(r, N_HEADS * HEAD_SLOT)


def _pack_table(t):
    tb = t.astype(jnp.bfloat16)
    lo = lax.bitcast_convert_type(tb[:, :ROW_WORDS], jnp.uint16).astype(jnp.uint32)
    hi = lax.bitcast_convert_type(tb[:, ROW_WORDS:], jnp.uint16).astype(jnp.uint32)
    words = lax.bitcast_convert_type(lo | (hi << 16), jnp.int32)
    return words.reshape(t.shape[0] * ROW_SUB, LANES)


def _prep_weights(g_norm1, w_in, conv_w, w_conv_out, g_q_lora, w_uq, g_kv_lora, w_ukv, g_qnorm,
                  g_knorm, w_attn_out, w_out, g_norm2, peer_wq, peer_k1, peer_k2, peer_u, peer_v):
    bf = jnp.bfloat16
    o1 = 3 * D_CONV
    o2 = o1 + Q_LORA
    o3 = o2 + KV_LORA
    o4 = o3 + QK_ROPE
    o5 = o4 + D_MODEL
    kr_slot = jnp.zeros((D_MODEL, HEAD_SLOT), w_in.dtype).at[:, QK_NOPE:QK_HEAD].set(w_in[:, o3:o4])
    w_in_r = jnp.concatenate([w_in[:, :o3], kr_slot, w_in[:, o4:]], axis=1).astype(bf)
    w_ukv3 = w_ukv.reshape(KV_LORA, N_HEADS, QK_NOPE + V_HEAD)
    w_uk = _pad_heads(w_ukv3[:, :, :QK_NOPE].reshape(KV_LORA, -1), QK_NOPE)
    w_uv = _pad_heads(w_ukv3[:, :, QK_NOPE:].reshape(KV_LORA, -1), V_HEAD)
    pad_gain = lambda g: jnp.zeros((1, HEAD_SLOT), jnp.float32).at[0, :QK_HEAD].set(g)
    w_ao = jnp.zeros((N_HEADS, HEAD_SLOT, D_MODEL), w_attn_out.dtype).at[:, :V_HEAD].set(
        w_attn_out.reshape(N_HEADS, V_HEAD, D_MODEL)).reshape(N_HEADS * HEAD_SLOT, D_MODEL)
    half = PEER_DK // 2
    kcat = jnp.zeros((PEER_HEADS, 2 * N_KEYS, PEER_DK), jnp.float32)
    kcat = kcat.at[:, :N_KEYS, :half].set(peer_k1).at[:, N_KEYS:, half:].set(peer_k2)
    tab_u = _pack_table(peer_u)
    tab_v = _pack_table(peer_v)
    return {
        "g_norm1": g_norm1.reshape(1, -1), "w_in": w_in_r, "conv_w": conv_w,
        "w_conv_out": w_conv_out.astype(bf), "g_q_lora": g_q_lora.reshape(1, -1),
        "w_uq": _pad_heads(w_uq, QK_HEAD).astype(bf), "g_kv_lora": g_kv_lora.reshape(1, -1),
        "w_uk": w_uk.astype(bf), "w_uv": w_uv.astype(bf),
        "g_qnorm": pad_gain(g_qnorm), "g_knorm": pad_gain(g_knorm),
        "w_attn_out": w_ao.astype(bf), "w_out": w_out.astype(bf), "g_norm2": g_norm2.reshape(1, -1),
        "peer_wq": peer_wq.astype(bf), "kcat": kcat.astype(bf),
        "tab_u": tab_u, "tab_v": tab_v,
        "tab_u_sc": tab_u.reshape(-1, ROW_WORDS), "tab_v_sc": tab_v.reshape(-1, ROW_WORDS),
    }


def _rope_tables(seq):
    pos = jnp.arange(seq, dtype=jnp.float32)
    inv = ROPE_THETA ** (-jnp.arange(0, QK_ROPE, 2, dtype=jnp.float32) / QK_ROPE)
    ang = pos[:, None] * inv[None, :]
    cos, sin = jnp.cos(ang), jnp.sin(ang)
    t1 = slice(QK_NOPE, QK_NOPE + HALF_ROPE)
    t2 = slice(QK_NOPE + HALF_ROPE, QK_HEAD)
    cos_t = jnp.ones((seq, HEAD_SLOT), jnp.float32).at[:, t1].set(cos).at[:, t2].set(cos)
    sin_a = jnp.zeros((seq, HEAD_SLOT), jnp.float32).at[:, t1].set(-sin)
    sin_b = jnp.zeros((seq, HEAD_SLOT), jnp.float32).at[:, t2].set(sin)
    return cos_t, sin_a, sin_b


def _front(x, c, w_ada, b_ada, wts, after):
    bsz, seq, d = x.shape
    n = bsz * seq
    ada = _ada(c, w_ada, b_ada)[:, None, :]
    shift1, scale1, gate1, shift2, scale2, gate2 = jnp.split(ada, 6, axis=-1)
    q, k, v, a, sgb = _inproj(x, scale1, shift1, wts, _rope_tables(seq), after)
    o = _attention(q, k, v)
    x1, h2, e, g = _mix(x, a, sgb, o, gate1, scale2, shift2, wts)
    return dict(x1=x1.reshape(n, d), h2=h2.reshape(n, d), e=e.reshape(n, N_PICKS),
                g=g.reshape(n, N_PICKS), gate2=gate2, seq=seq)


def _peer_on_sparsecores(f, wts):
    ids = lax.shift_right_logical(f["e"], ROW_SUB.bit_length() - 1)
    peer = _peer_sc(ids.reshape(-1, 2, SC_HALF), f["g"], f["h2"], wts["tab_u_sc"], wts["tab_v_sc"])
    return _peer_finish(f["x1"], f["gate2"], peer, f["seq"])


def _peer_on_tensorcore(f, wts):
    e = f["e"].reshape(-1)
    w = _peer_u(e, f["h2"], f["g"], wts["tab_u"])
    return _peer_v(e, w, f["x1"], f["gate2"], wts["tab_v"], f["seq"])


def kernel(x_prompt, x_sample, c_prompt, c_sample, w_ada, b_ada, g_norm1, w_in, conv_w, w_conv_out, g_q_lora, w_uq, g_kv_lora, w_ukv, g_qnorm, g_knorm, w_attn_out, w_out, g_norm2, peer_wq, peer_k1, peer_k2, peer_u, peer_v):
    wts = _prep_weights(g_norm1[0], w_in[0], conv_w[0], w_conv_out[0], g_q_lora[0], w_uq[0],
                        g_kv_lora[0], w_ukv[0], g_qnorm[0], g_knorm[0], w_attn_out[0], w_out[0],
                        g_norm2[0], peer_wq[0], peer_k1[0], peer_k2[0], peer_u[0], peer_v[0])
    sc_tables = (wts["tab_u_sc"], wts["tab_v_sc"])
    fs = _front(x_sample, c_sample, w_ada[0], b_ada[0], wts, sc_tables)
    y_sample = _peer_on_sparsecores(fs, wts)
    fp = _front(x_prompt, c_prompt, w_ada[0], b_ada[0], wts, sc_tables + (fs["g"],))
    y_prompt = _peer_on_tensorcore(fp, wts)
    return y_prompt.reshape(x_prompt.shape), y_sample.reshape(x_sample.shape)
```

```python
import dataclasses
import functools
import math

import jax
import jax.numpy as jnp
import numpy as np
from jax import lax
from jax.experimental import pallas as pl
from jax.experimental.pallas import tpu as pltpu
from jax.experimental.pallas import tpu_sc as plsc

D_MODEL = 1024
D_CONV = 512
N_HEADS = 8
QK_NOPE = 64
QK_ROPE = 32
V_HEAD = 64
Q_LORA = 256
KV_LORA = 128
QK_HEAD = QK_NOPE + QK_ROPE
ROPE_THETA = 10000.0
PEER_HEADS = 8
N_KEYS = 128
PEER_DK = 128
PEER_TOPK = 16
EPS = 1e-6

LANES = 128
SUBLANES = 8
HEAD_SLOT = LANES
HALF_ROPE = QK_ROPE // 2
N_PICKS = PEER_HEADS * PEER_TOPK
ROW_WORDS = D_MODEL // 2
ROW_SUB = ROW_WORDS // LANES
VMEM_LIMIT = 56 * 1024 * 1024

C_BG, C_CG, C_HC = 0, 512, 1024
C_CQ = 1536
C_CKV = C_CQ + Q_LORA
C_KR = C_CKV + KV_LORA
C_GA = C_KR + HEAD_SLOT
C_GB = C_GA + D_MODEL
IN_COLS_R = C_GB + D_MODEL

TS_IN = 512
TQ = 1024
TK = 2048
KV_UNROLL = 2
TS_MIX = 256
TT_PEER = 128
TOK_UNROLL_U = 32
TOK_UNROLL_V = 32
IDX_GROUP = 8

_NEG_INF = float("-inf")


def _cparams(sem):
    return pltpu.CompilerParams(dimension_semantics=sem, vmem_limit_bytes=VMEM_LIMIT)


def _ada_kernel(c_ref, w_ref, b_ref, o_ref):
    o_ref[...] = jnp.dot(c_ref[...], w_ref[...], preferred_element_type=jnp.float32,
                         precision=lax.Precision.HIGHEST) + b_ref[...]


def _ada(c, w_ada, b_ada):
    bsz = c.shape[0]
    rows = -(-bsz // SUBLANES) * SUBLANES
    c_p = jnp.zeros((rows, D_MODEL), jnp.float32).at[:bsz].set(c)
    out = pl.pallas_call(
        _ada_kernel,
        grid=(6,),
        in_specs=[pl.BlockSpec((rows, D_MODEL), lambda j: (0, 0)),
                  pl.BlockSpec((D_MODEL, D_MODEL), lambda j: (0, j)),
                  pl.BlockSpec((1, D_MODEL), lambda j: (0, j))],
        out_specs=pl.BlockSpec((rows, D_MODEL), lambda j: (0, j)),
        out_shape=jax.ShapeDtypeStruct((rows, 6 * D_MODEL), jnp.float32),
        compiler_params=_cparams(("arbitrary",)),
        name="ada",
    )(c_p, w_ada, b_ada.reshape(1, -1))
    return out[:bsz]


def _adaln(x, g, scale, shift):
    ms = jnp.mean(x * x, axis=-1, keepdims=True)
    return (x * lax.rsqrt(ms + EPS) * g) * (1.0 + scale) + shift


def _rms_rows(x, g, n):
    ms = jnp.sum(x * x, axis=-1, keepdims=True) * (1.0 / n)
    return x * lax.rsqrt(ms + EPS) * g


def _rope(x, cos, sin_a, sin_b):
    return (x * cos + pltpu.roll(x, LANES - HALF_ROPE, axis=1) * sin_a
            + pltpu.roll(x, HALF_ROPE, axis=1) * sin_b)


def _inproj_kernel(x_ref, xp_ref, xn_ref, scale_ref, shift_ref, g1_ref, w_in_ref, conv_w_ref,
                   w_co_ref, gql_ref, w_uq_ref, gkvl_ref, w_uk_ref, w_uv_ref, gqn_ref, gkn_ref,
                   cos_ref, sa_ref, sb_ref, *rest):
    q_ref, k_ref, v_ref, a_ref, sgb_ref = rest[-5:]
    i = pl.program_id(1)
    n_i = pl.num_programs(1)
    g1 = g1_ref[...]
    scale = scale_ref[...]
    shift = shift_ref[...]
    ts = x_ref.shape[0]

    h = _adaln(x_ref[...], g1, scale, shift).astype(jnp.bfloat16)

    def proj(lo, width):
        return jnp.dot(h, w_in_ref[:, lo:lo + width], preferred_element_type=jnp.float32)

    z = proj(C_CG, D_CONV) * proj(C_HC, D_CONV)

    def halo_z(xh_ref):
        hh = _adaln(xh_ref[...], g1, scale, shift).astype(jnp.bfloat16)
        zc = jnp.dot(hh, w_in_ref[:, C_CG:C_CG + 2 * D_CONV], preferred_element_type=jnp.float32)
        return zc[:, :D_CONV] * zc[:, D_CONV:]

    z_prev = halo_z(xp_ref)[SUBLANES - 1:SUBLANES, :]
    z_next = halo_z(xn_ref)[0:1, :]
    z_prev = jnp.where(i == 0, 0.0, z_prev)
    z_next = jnp.where(i == n_i - 1, 0.0, z_next)
    row = lax.broadcasted_iota(jnp.int32, (ts, D_CONV), 0)
    z_up = jnp.where(row == 0, z_prev, pltpu.roll(z, 1, axis=0))
    z_dn = jnp.where(row == ts - 1, z_next, pltpu.roll(z, ts - 1, axis=0))
    cw = conv_w_ref[...]
    y = z_up * cw[0:1, :] + z * cw[1:2, :] + z_dn * cw[2:3, :]
    out_a = jnp.dot((proj(C_BG, D_CONV) * y).astype(jnp.bfloat16), w_co_ref[...],
                    preferred_element_type=jnp.float32)
    a_ref[...] = jax.nn.sigmoid(proj(C_GA, D_MODEL)) * out_a
    sgb_ref[...] = jax.nn.sigmoid(proj(C_GB, D_MODEL))

    cos = cos_ref[...]
    sin_a = sa_ref[...]
    sin_b = sb_ref[...]
    cq = _rms_rows(proj(C_CQ, Q_LORA), gql_ref[...], Q_LORA).astype(jnp.bfloat16)
    qf = jnp.dot(cq, w_uq_ref[...], preferred_element_type=jnp.float32)
    ckv = _rms_rows(proj(C_CKV, KV_LORA), gkvl_ref[...], KV_LORA).astype(jnp.bfloat16)
    kf = jnp.dot(ckv, w_uk_ref[...], preferred_element_type=jnp.float32)
    lane = lax.broadcasted_iota(jnp.int32, (1, N_HEADS * HEAD_SLOT), 1)
    ones_lane = (jnp.bitwise_and(lane, HEAD_SLOT - 1) == V_HEAD).astype(jnp.float32)
    v_ref[...] = (jnp.dot(ckv, w_uv_ref[...], preferred_element_type=jnp.float32)
                  + ones_lane).astype(jnp.bfloat16)
    kr = proj(C_KR, HEAD_SLOT)
    gqn = gqn_ref[...]
    gkn = gkn_ref[...]
    q_scale = QK_HEAD ** -0.5 * math.log2(math.e)
    for hd in range(N_HEADS):
        sl = slice(hd * HEAD_SLOT, (hd + 1) * HEAD_SLOT)
        qh = _rope(_rms_rows(qf[:, sl], gqn, QK_HEAD), cos, sin_a, sin_b)
        q_ref[:, sl] = (qh * q_scale).astype(jnp.bfloat16)
        kh = _rope(_rms_rows(kf[:, sl] + kr, gkn, QK_HEAD), cos, sin_a, sin_b)
        k_ref[:, sl] = kh.astype(jnp.bfloat16)


def _inproj(x, scale1, shift1, wts, rope, after):
    bsz, seq, _ = x.shape
    ts = min(TS_IN, seq)
    n_i = seq // ts
    nb8 = seq // SUBLANES
    per8 = ts // SUBLANES
    tile = lambda b, i: (b, i, 0)
    per_b = lambda b, i: (b, 0, 0)
    full2 = lambda b, i: (0, 0)
    hw = N_HEADS * HEAD_SLOT
    in_specs = [
        pl.BlockSpec((None, ts, D_MODEL), tile),
        pl.BlockSpec((None, SUBLANES, D_MODEL), lambda b, i: (b, jnp.maximum(i * per8 - 1, 0), 0)),
        pl.BlockSpec((None, SUBLANES, D_MODEL), lambda b, i: (b, jnp.minimum((i + 1) * per8, nb8 - 1), 0)),
        pl.BlockSpec((None, 1, D_MODEL), per_b),
        pl.BlockSpec((None, 1, D_MODEL), per_b),
        pl.BlockSpec((1, D_MODEL), full2),
        pl.BlockSpec((D_MODEL, IN_COLS_R), full2),
        pl.BlockSpec((3, D_CONV), full2),
        pl.BlockSpec((D_CONV, D_MODEL), full2),
        pl.BlockSpec((1, Q_LORA), full2),
        pl.BlockSpec((Q_LORA, hw), full2),
        pl.BlockSpec((1, KV_LORA), full2),
        pl.BlockSpec((KV_LORA, hw), full2),
        pl.BlockSpec((KV_LORA, hw), full2),
        pl.BlockSpec((1, HEAD_SLOT), full2),
        pl.BlockSpec((1, HEAD_SLOT), full2),
        pl.BlockSpec((ts, HEAD_SLOT), lambda b, i: (i, 0)),
        pl.BlockSpec((ts, HEAD_SLOT), lambda b, i: (i, 0)),
        pl.BlockSpec((ts, HEAD_SLOT), lambda b, i: (i, 0)),
    ] + [pl.BlockSpec(memory_space=pl.ANY)] * len(after)
    out_specs = [pl.BlockSpec((None, ts, hw), tile)] * 3 + [pl.BlockSpec((None, ts, D_MODEL), tile)] * 2
    out_shape = ([jax.ShapeDtypeStruct((bsz, seq, hw), jnp.bfloat16)] * 3
                 + [jax.ShapeDtypeStruct((bsz, seq, D_MODEL), jnp.float32)] * 2)
    return pl.pallas_call(
        _inproj_kernel,
        grid=(bsz, n_i),
        in_specs=in_specs,
        out_specs=out_specs,
        out_shape=out_shape,
        compiler_params=_cparams(("parallel", "parallel")),
        name="inproj",
    )(x, x, x, scale1, shift1, wts["g_norm1"], wts["w_in"], wts["conv_w"], wts["w_conv_out"],
      wts["g_q_lora"], wts["w_uq"], wts["g_kv_lora"], wts["w_uk"], wts["w_uv"],
      wts["g_qnorm"], wts["g_knorm"], rope[0], rope[1], rope[2], *after)


def _attn_kernel(q_ref, k_ref, v_ref, o_ref):
    q = q_ref[...]
    tq = q.shape[0]
    seq = k_ref.shape[0]
    tk = min(TK, seq)

    n_chunks = seq // tk
    unroll = KV_UNROLL if n_chunks % KV_UNROLL == 0 else 1

    def chunk(off, carry):
        m, acc = carry
        kc = k_ref[pl.ds(off, tk), :]
        vc = v_ref[pl.ds(off, tk), :]
        s = lax.dot_general(q, kc, (((1,), (1,)), ((), ())), preferred_element_type=jnp.float32)
        m_new = jnp.maximum(m, jnp.max(s, axis=-1, keepdims=True))
        p = jnp.exp2(s - m_new).astype(jnp.bfloat16)
        alpha = jnp.exp2(m - m_new)
        acc = alpha * acc + jnp.dot(p, vc, preferred_element_type=jnp.float32)
        return m_new, acc

    def body(j, carry):
        for u in range(unroll):
            carry = chunk(pl.multiple_of((j * unroll + u) * tk, tk), carry)
        return carry

    m0 = jnp.full((tq, 1), _NEG_INF, jnp.float32)
    acc0 = jnp.zeros((tq, HEAD_SLOT), jnp.float32)
    _, acc = lax.fori_loop(0, n_chunks // unroll, body, (m0, acc0))
    o_ref[...] = (acc / acc[:, V_HEAD:V_HEAD + 1]).astype(jnp.bfloat16)


def _attention(q, k, v):
    bsz, seq, hw = q.shape
    tq = min(TQ, seq)
    return pl.pallas_call(
        _attn_kernel,
        grid=(bsz, N_HEADS, seq // tq),
        in_specs=[pl.BlockSpec((None, tq, HEAD_SLOT), lambda b, h, i: (b, i, h)),
                  pl.BlockSpec((None, seq, HEAD_SLOT), lambda b, h, i: (b, 0, h)),
                  pl.BlockSpec((None, seq, HEAD_SLOT), lambda b, h, i: (b, 0, h))],
        out_specs=pl.BlockSpec((None, tq, HEAD_SLOT), lambda b, h, i: (b, i, h)),
        out_shape=jax.ShapeDtypeStruct((bsz, seq, hw), jnp.bfloat16),
        compiler_params=_cparams(("parallel", "parallel", "arbitrary")),
        name="attn",
    )(q, k, v)


def _topk_rows(s, iota, iota_blocks):
    vals, idxs = [], []
    big = jnp.float32(1e9)
    n_blk = s.shape[0] // SUBLANES
    for _ in range(PEER_TOPK):
        vb = [s[j * SUBLANES:(j + 1) * SUBLANES, :] for j in range(n_blk)]
        ib = list(iota_blocks)
        while len(vb) > 1:
            nv, ni = [], []
            for a in range(0, len(vb) - 1, 2):
                keep = vb[a] >= vb[a + 1]
                nv.append(jnp.where(keep, vb[a], vb[a + 1]))
                ni.append(jnp.where(keep, ib[a], ib[a + 1]))
            if len(vb) % 2:
                nv.append(vb[-1])
                ni.append(ib[-1])
            vb, ib = nv, ni
        m = jnp.max(vb[0], axis=0, keepdims=True)
        am = jnp.min(jnp.where(vb[0] == m, ib[0], big), axis=0, keepdims=True)
        vals.append(m)
        idxs.append(am)
        s = jnp.where(iota == am, _NEG_INF, s)
    return jnp.concatenate(vals, axis=0), jnp.concatenate(idxs, axis=0)


def _take_rows(tab, idx):
    out = jnp.zeros_like(tab)
    for a in range(tab.shape[0]):
        out = out + jnp.where(idx == jnp.float32(a), tab[a:a + 1, :], 0.0)
    return out


_CAND = [(a, b) for a in range(PEER_TOPK) for b in range(PEER_TOPK) if (a + 1) * (b + 1) <= PEER_TOPK]
N_CAND = -(-len(_CAND) // SUBLANES) * SUBLANES
_PAD_LABEL = float(PEER_TOPK * PEER_TOPK)


def _candidate_tables():
    sel = np.zeros((2, N_CAND, PEER_TOPK), np.float32)
    lab = np.full((N_CAND, 1), _PAD_LABEL, np.float32)
    for r, (a, b) in enumerate(_CAND):
        sel[0, r, a] = 1.0
        sel[1, r, b] = 1.0
        lab[r, 0] = a * PEER_TOPK + b
    return sel, lab


def _mix_kernel(x_ref, a_ref, sgb_ref, o_ref, gate1_ref, scale2_ref, shift2_ref, g2_ref,
                w_ao_ref, w_out_ref, w_pq_ref, kcat_ref, sel_ref, lab_ref,
                x1_ref, h2_ref, e_ref, g_ref):
    ts = x_ref.shape[0]
    out_b = jnp.dot(o_ref[...], w_ao_ref[...], preferred_element_type=jnp.float32)
    merged = (a_ref[...] + sgb_ref[...] * out_b).astype(jnp.bfloat16)
    x1 = x_ref[...] + gate1_ref[...] * jnp.dot(merged, w_out_ref[...],
                                              preferred_element_type=jnp.float32)
    x1_ref[...] = x1
    h2 = _adaln(x1, g2_ref[...], scale2_ref[...], shift2_ref[...])
    h2_ref[...] = h2
    qp = jnp.dot(h2.astype(jnp.bfloat16), w_pq_ref[...],
                 preferred_element_type=jnp.float32).astype(jnp.bfloat16)

    iota_k = lax.broadcasted_iota(jnp.int32, (N_KEYS, ts), 0).astype(jnp.float32)
    iota8 = lax.broadcasted_iota(jnp.int32, (SUBLANES, ts), 0).astype(jnp.float32)
    iota_k_blocks = [iota8 + float(SUBLANES * j) for j in range(N_KEYS // SUBLANES)]
    lab = lab_ref[...]
    lab_blocks = [lab_ref[j * SUBLANES:(j + 1) * SUBLANES, :] for j in range(N_CAND // SUBLANES)]
    valid = lab < _PAD_LABEL
    sel_a = sel_ref[0]
    sel_b = sel_ref[1]
    exact = dict(preferred_element_type=jnp.float32, precision=lax.Precision.HIGHEST)
    e_rows, g_rows = [], []
    for hd in range(PEER_HEADS):
        qh = qp[:, hd * PEER_DK:(hd + 1) * PEER_DK]
        st = lax.dot_general(kcat_ref[hd], qh, (((1,), (1,)), ((), ())),
                             preferred_element_type=jnp.float32)
        v1, i1 = _topk_rows(st[:N_KEYS], iota_k, iota_k_blocks)
        v2, i2 = _topk_rows(st[N_KEYS:], iota_k, iota_k_blocks)
        comb = jnp.dot(sel_a, v1, **exact) + jnp.dot(sel_b, v2, **exact)
        sc, pos = _topk_rows(jnp.where(valid, comb, _NEG_INF), lab, lab_blocks)
        ia = jnp.floor(pos * (1.0 / PEER_TOPK))
        ib = pos - ia * PEER_TOPK
        e_rows.append(_take_rows(i1, ia) * N_KEYS + _take_rows(i2, ib))
        p = jnp.exp(sc - sc[0:1, :])
        g_rows.append(p / jnp.sum(p, axis=0, keepdims=True))
    e_all = jnp.concatenate(e_rows, axis=0)
    g_all = jnp.concatenate(g_rows, axis=0)
    for c in range(ts // LANES):
        cs = slice(c * LANES, (c + 1) * LANES)
        e_ref[cs, :] = (e_all[:, cs].T * ROW_SUB).astype(jnp.int32)
        g_ref[cs, :] = g_all[:, cs].T


def _mix(x, a, sgb, o, gate1, scale2, shift2, wts):
    bsz, seq, _ = x.shape
    ts = min(TS_MIX, seq)
    tile = lambda b, i: (b, i, 0)
    per_b = lambda b, i: (b, 0, 0)
    full2 = lambda b, i: (0, 0)
    hw = N_HEADS * HEAD_SLOT
    big = pl.BlockSpec((None, ts, D_MODEL), tile)
    vec = pl.BlockSpec((None, 1, D_MODEL), per_b)
    picks = pl.BlockSpec((None, ts, N_PICKS), tile)
    sel, lab = _candidate_tables()
    return pl.pallas_call(
        _mix_kernel,
        grid=(bsz, seq // ts),
        in_specs=[big, big, big, pl.BlockSpec((None, ts, hw), tile), vec, vec, vec,
                  pl.BlockSpec((1, D_MODEL), full2),
                  pl.BlockSpec((hw, D_MODEL), full2),
                  pl.BlockSpec((D_MODEL, D_MODEL), full2),
                  pl.BlockSpec((D_MODEL, PEER_HEADS * PEER_DK), full2),
                  pl.BlockSpec((PEER_HEADS, 2 * N_KEYS, PEER_DK), lambda b, i: (0, 0, 0)),
                  pl.BlockSpec((2, N_CAND, PEER_TOPK), lambda b, i: (0, 0, 0)),
                  pl.BlockSpec((N_CAND, ts), full2)],
        out_specs=[big, big, picks, picks],
        out_shape=[jax.ShapeDtypeStruct((bsz, seq, D_MODEL), jnp.float32),
                   jax.ShapeDtypeStruct((bsz, seq, D_MODEL), jnp.float32),
                   jax.ShapeDtypeStruct((bsz, seq, N_PICKS), jnp.int32),
                   jax.ShapeDtypeStruct((bsz, seq, N_PICKS), jnp.float32)],
        compiler_params=_cparams(("parallel", "parallel")),
        name="mix",
    )(x, a, sgb, o, gate1, scale2, shift2, wts["g_norm2"], wts["w_attn_out"], wts["w_out"],
      wts["peer_wq"], wts["kcat"], jnp.asarray(sel), jnp.broadcast_to(jnp.asarray(lab), (N_CAND, ts)))


def _unpack_row(w):
    lo = lax.bitcast_convert_type(lax.shift_left(w, 16), jnp.float32)
    hi = lax.bitcast_convert_type(jnp.bitwise_and(w, jnp.int32(-65536)), jnp.float32)
    return lo, hi


def _gelu_tanh(x):
    c = math.sqrt(2.0 / math.pi)
    return 0.5 * x * (1.0 + jnp.tanh(c * (x + 0.044715 * (x * x * x))))


def _gather_rows(e_ref, t, tab_ref, pb):
    for j in range(N_PICKS // IDX_GROUP):
        row = e_ref.at[pl.ds(pl.multiple_of(t * N_PICKS + j * IDX_GROUP, IDX_GROUP), IDX_GROUP)]
        for c in range(IDX_GROUP):
            k = j * IDX_GROUP + c
            r = pl.multiple_of(row[c], ROW_SUB)
            pb[k * ROW_SUB:(k + 1) * ROW_SUB, :] = tab_ref[pl.ds(r, ROW_SUB), :]


def _pick_block(pb, s):
    return pb[pl.ds(s, N_PICKS, stride=ROW_SUB), :]


def _peer_u_kernel(e_ref, x_ref, g_ref, tab_ref, w_ref, pbuf0, pbuf1, abuf):
    tt = g_ref.shape[0]
    pbufs = (pbuf0, pbuf1)
    lane = lax.broadcasted_iota(jnp.int32, (N_PICKS, tt), 1)
    abuf[...] = jnp.zeros_like(abuf)

    def tok(i, carry):
        x_all = x_ref[pl.ds(pl.multiple_of(i * TOK_UNROLL_U, TOK_UNROLL_U), TOK_UNROLL_U), :]
        for u in range(TOK_UNROLL_U):
            t = i * TOK_UNROLL_U + u
            pb = pbufs[u % 2]
            _gather_rows(e_ref, t, tab_ref, pb)
            acc = None
            for s in range(ROW_SUB):
                lo, hi = _unpack_row(_pick_block(pb, s))
                x_lo = x_all[u:u + 1, s * LANES:(s + 1) * LANES]
                x_hi = x_all[u:u + 1, ROW_WORDS + s * LANES:ROW_WORDS + (s + 1) * LANES]
                term = lo * x_lo + hi * x_hi
                acc = term if acc is None else acc + term
            a_col = jnp.sum(acc, axis=1, keepdims=True)
            abuf[...] = jnp.where(lane == t, a_col, abuf[...])
        return carry

    lax.fori_loop(0, tt // TOK_UNROLL_U, tok, 0)
    w_ref[...] = g_ref[...] * _gelu_tanh(abuf[...].T)


def _peer_u(e, h2, g, tab):
    n = h2.shape[0]
    tt = TT_PEER
    return pl.pallas_call(
        _peer_u_kernel,
        grid=(n // tt,),
        in_specs=[pl.BlockSpec((tt * N_PICKS,), lambda i: (i,), memory_space=pltpu.SMEM),
                  pl.BlockSpec((tt, D_MODEL), lambda i: (i, 0)),
                  pl.BlockSpec((tt, N_PICKS), lambda i: (i, 0)),
                  pl.BlockSpec(memory_space=pltpu.VMEM)],
        out_specs=pl.BlockSpec((tt, N_PICKS), lambda i: (i, 0)),
        out_shape=jax.ShapeDtypeStruct((n, N_PICKS), jnp.float32),
        scratch_shapes=[pltpu.VMEM((N_PICKS * ROW_SUB, LANES), jnp.int32),
                        pltpu.VMEM((N_PICKS * ROW_SUB, LANES), jnp.int32),
                        pltpu.VMEM((N_PICKS, tt), jnp.float32)],
        compiler_params=_cparams(("arbitrary",)),
        name="peer_u",
    )(e, h2, g, tab)


def _peer_v_kernel(e_ref, w_ref, x_ref, gate_ref, tab_ref, y_ref, pbuf0, pbuf1, wt_ref, acc_ref):
    tt = x_ref.shape[0]
    pbufs = (pbuf0, pbuf1)
    wt_ref[...] = w_ref[...].T
    lane = lax.broadcasted_iota(jnp.int32, (N_PICKS, tt), 1)
    sub = lax.broadcasted_iota(jnp.int32, (SUBLANES, D_MODEL), 0)
    acc_ref[...] = jnp.zeros_like(acc_ref)

    def process(pb, t):
        w_col = jnp.sum(jnp.where(lane == t, wt_ref[...], 0.0), axis=1, keepdims=True)
        los, his = [], []
        for s in range(ROW_SUB):
            lo, hi = _unpack_row(_pick_block(pb, s))
            los.append(jnp.sum(lo * w_col, axis=0, keepdims=True))
            his.append(jnp.sum(hi * w_col, axis=0, keepdims=True))
        row = jnp.concatenate(los + his, axis=1)
        b0 = pl.multiple_of(jnp.bitwise_and(t, -SUBLANES), SUBLANES)
        mask = sub == jnp.bitwise_and(t, SUBLANES - 1)
        acc_ref[pl.ds(b0, SUBLANES), :] = jnp.where(mask, row, acc_ref[pl.ds(b0, SUBLANES), :])

    def step(i, carry):
        t0 = i * TOK_UNROLL_V
        for u in range(TOK_UNROLL_V):
            _gather_rows(e_ref, t0 + u, tab_ref, pbufs[u % 2])
            process(pbufs[u % 2], t0 + u)
        return carry

    lax.fori_loop(0, tt // TOK_UNROLL_V, step, 0)
    y_ref[...] = x_ref[...] + gate_ref[...] * acc_ref[...]


def _peer_v(e, w, x1, gate2, tab, seq):
    n = x1.shape[0]
    tt = TT_PEER
    return pl.pallas_call(
        _peer_v_kernel,
        grid=(n // tt,),
        in_specs=[pl.BlockSpec((tt * N_PICKS,), lambda i: (i,), memory_space=pltpu.SMEM),
                  pl.BlockSpec((tt, N_PICKS), lambda i: (i, 0)),
                  pl.BlockSpec((tt, D_MODEL), lambda i: (i, 0)),
                  pl.BlockSpec((None, 1, D_MODEL), lambda i: ((i * tt) // seq, 0, 0)),
                  pl.BlockSpec(memory_space=pltpu.VMEM)],
        out_specs=pl.BlockSpec((tt, D_MODEL), lambda i: (i, 0)),
        out_shape=jax.ShapeDtypeStruct((n, D_MODEL), jnp.float32),
        scratch_shapes=[pltpu.VMEM((N_PICKS * ROW_SUB, LANES), jnp.int32),
                        pltpu.VMEM((N_PICKS * ROW_SUB, LANES), jnp.int32),
                        pltpu.VMEM((N_PICKS, tt), jnp.float32),
                        pltpu.VMEM((tt, D_MODEL), jnp.float32)],
        compiler_params=_cparams(("arbitrary",)),
        name="peer_v",
    )(e, w, x1, gate2, tab)


SC_CORES = 2
SC_SUBCORES = 16
SC_LANES = 16
SC_WORKERS = SC_CORES * SC_SUBCORES
SC_HALF = N_PICKS // 2
SC_CHUNKS = ROW_WORDS // SC_LANES
SC_ACC_CHUNKS = 8


def _sc_params():
    cp = pltpu.CompilerParams()
    if "needs_layout_passes" in pltpu.CompilerParams.__dataclass_fields__:
        cp = dataclasses.replace(cp, needs_layout_passes=False)
    return cp


def _gelu_tanh_via_exp(x):
    c = math.sqrt(2.0 / math.pi)
    z = c * (x + 0.044715 * (x * x * x))
    t = jnp.exp(-2.0 * jnp.abs(z))
    return 0.5 * x * (1.0 + jnp.sign(z) * (1.0 - t) / (1.0 + t))


def _peer_sc(e2, g, h2, tab_u, tab_v):
    n_sc = e2.shape[0]
    assert n_sc % (2 * SC_WORKERS) == 0
    per_w = n_sc // SC_WORKERS
    mesh = plsc.VectorSubcoreMesh(core_axis_name="c", subcore_axis_name="s")
    ln = SC_LANES
    per_token_inputs = lambda: [pltpu.VMEM((SC_HALF,), jnp.int32), pltpu.VMEM((SC_HALF,), jnp.int32),
                                pltpu.VMEM((D_MODEL,), jnp.float32), pltpu.VMEM((N_PICKS,), jnp.float32)]

    @functools.partial(
        pl.kernel, mesh=mesh,
        out_type=jax.ShapeDtypeStruct((n_sc, D_MODEL), jnp.float32),
        scratch_types=per_token_inputs() + per_token_inputs() + [
            pltpu.VMEM((SC_HALF, ROW_WORDS), jnp.int32), pltpu.VMEM((SC_HALF, ROW_WORDS), jnp.int32),
            pltpu.VMEM((N_PICKS,), jnp.float32), pltpu.VMEM((D_MODEL,), jnp.float32),
            pltpu.SemaphoreType.DMA, pltpu.SemaphoreType.DMA,
            pltpu.SemaphoreType.DMA, pltpu.SemaphoreType.DMA],
        compiler_params=_sc_params(),
        cost_estimate=pl.CostEstimate(
            flops=4 * n_sc * N_PICKS * D_MODEL,
            transcendentals=n_sc * N_PICKS,
            bytes_accessed=n_sc * (2 * N_PICKS * ROW_WORDS * 4 + 3 * D_MODEL * 4 + 3 * N_PICKS * 4)),
    )
    def body(e_hbm, g_hbm, x_hbm, tu_hbm, tv_hbm, out_hbm,
             ia0, ib0, x0, g0, ia1, ib1, x1, g1, buf_a, buf_b, w_v, o_v, sem_a, sem_b, sem_in0, sem_in1):
        wid = lax.axis_index("s") * SC_CORES + lax.axis_index("c")
        base = wid * per_w
        lane = lax.iota(jnp.int32, ln)
        zero = jnp.zeros((ln,), jnp.float32)
        in_sets = ((ia0, ib0, x0, g0, sem_in0), (ia1, ib1, x1, g1, sem_in1))

        def input_copies(st, i):
            ia, ib, xv, gv, sem = st
            tl = base + jnp.minimum(i, per_w - 1)
            return [pltpu.make_async_copy(e_hbm.at[tl, 0], ia, sem),
                    pltpu.make_async_copy(e_hbm.at[tl, 1], ib, sem),
                    pltpu.make_async_copy(x_hbm.at[tl], xv, sem),
                    pltpu.make_async_copy(g_hbm.at[tl], gv, sem)]

        def u_half(buf, k0, x_v, g_v):
            @pl.loop(0, SC_HALF // ln)
            def _(grp):
                def chunk(j, accs):
                    xl = x_v[pl.ds(j * ln, ln)]
                    xh = x_v[pl.ds(ROW_WORDS + j * ln, ln)]
                    new = []
                    for kk in range(ln):
                        lo, hi = _unpack_row(buf[grp * ln + kk, pl.ds(j * ln, ln)])
                        new.append(accs[kk] + lo * xl + hi * xh)
                    return tuple(new)

                accs = lax.fori_loop(0, SC_CHUNKS, chunk, tuple(zero for _ in range(ln)))
                a = zero
                for kk in range(ln):
                    a = jnp.where(lane == kk, jnp.sum(accs[kk]), a)
                sl = pl.ds(k0 + grp * ln, ln)
                w_v[sl] = g_v[sl] * _gelu_tanh_via_exp(a)

        def v_half(buf, k0):
            @pl.loop(0, SC_CHUNKS // SC_ACC_CHUNKS)
            def _(jb):
                def pick(kq, accs):
                    wk = plsc.load_gather(w_v, [jnp.full((ln,), k0, jnp.int32) + kq])
                    new = []
                    for c in range(SC_ACC_CHUNKS):
                        lo, hi = _unpack_row(buf[kq, pl.ds((jb * SC_ACC_CHUNKS + c) * ln, ln)])
                        new.append(accs[2 * c] + wk * lo)
                        new.append(accs[2 * c + 1] + wk * hi)
                    return tuple(new)

                accs = lax.fori_loop(0, SC_HALF, pick, tuple(zero for _ in range(2 * SC_ACC_CHUNKS)))
                for c in range(SC_ACC_CHUNKS):
                    col = (jb * SC_ACC_CHUNKS + c) * ln
                    o_v[pl.ds(col, ln)] = o_v[pl.ds(col, ln)] + accs[2 * c]
                    o_v[pl.ds(ROW_WORDS + col, ln)] = o_v[pl.ds(ROW_WORDS + col, ln)] + accs[2 * c + 1]

        def token(i, cur, nxt):
            ia, ib, x_v, g_v, _ = cur
            for c in input_copies(nxt, i + 1):
                c.start()
            u_b = pltpu.async_copy(tu_hbm.at[ib], buf_b, sem_b)

            @pl.loop(0, D_MODEL // ln)
            def _(c):
                o_v[pl.ds(c * ln, ln)] = zero

            pltpu.make_async_copy(tu_hbm.at[ia], buf_a, sem_a).wait()
            u_half(buf_a, 0, x_v, g_v)
            v_a = pltpu.async_copy(tv_hbm.at[ia], buf_a, sem_a)
            u_b.wait()
            u_half(buf_b, SC_HALF, x_v, g_v)
            v_b = pltpu.async_copy(tv_hbm.at[ib], buf_b, sem_b)
            v_a.wait()
            v_half(buf_a, 0)
            for c in input_copies(nxt, i + 1):
                c.wait()
            pltpu.async_copy(tu_hbm.at[nxt[0]], buf_a, sem_a)
            v_b.wait()
            v_half(buf_b, SC_HALF)
            pltpu.sync_copy(o_v, out_hbm.at[base + i])

        for c in input_copies(in_sets[0], 0):
            c.start()
        for c in input_copies(in_sets[0], 0):
            c.wait()
        pltpu.async_copy(tu_hbm.at[ia0], buf_a, sem_a)

        @pl.loop(0, per_w // 2)
        def _(pair):
            token(2 * pair, in_sets[0], in_sets[1])
            token(2 * pair + 1, in_sets[1], in_sets[0])

        pltpu.make_async_copy(tu_hbm.at[ia0], buf_a, sem_a).wait()

    return body(e2, g, h2, tab_u, tab_v)


def _finish_kernel(x_ref, gate_ref, p_ref, y_ref):
    y_ref[...] = x_ref[...] + gate_ref[...] * p_ref[...]


def _peer_finish(x1, gate2, peer, seq):
    n = x1.shape[0]
    tt = TT_PEER
    row = pl.BlockSpec((tt, D_MODEL), lambda i: (i, 0))
    return pl.pallas_call(
        _finish_kernel,
        grid=(n // tt,),
        in_specs=[row, pl.BlockSpec((None, 1, D_MODEL), lambda i: ((i * tt) // seq, 0, 0)), row],
        out_specs=row,
        out_shape=jax.ShapeDtypeStruct((n, D_MODEL), jnp.float32),
        compiler_params=_cparams(("arbitrary",)),
        name="peer_finish",
    )(x1, gate2, peer)


def _pad_heads(w, used, lead=0):
    r = w.shape[0]
    w3 = w.reshape(r, N_HEADS, used)
    out = jnp.zeros((r, N_HEADS, HEAD_SLOT), w.dtype).at[:, :, lead:lead + used].set(w3)
    return out.reshape(r, N_HEADS * HEAD_SLOT)


def _pack_table(t):
    tb = t.astype(jnp.bfloat16)
    lo = lax.bitcast_convert_type(tb[:, :ROW_WORDS], jnp.uint16).astype(jnp.uint32)
    hi = lax.bitcast_convert_type(tb[:, ROW_WORDS:], jnp.uint16).astype(jnp.uint32)
    words = lax.bitcast_convert_type(lo | (hi << 16), jnp.int32)
    return words.reshape(t.shape[0] * ROW_SUB, LANES)


def _prep_weights(g_norm1, w_in, conv_w, w_conv_out, g_q_lora, w_uq, g_kv_lora, w_ukv, g_qnorm,
                  g_knorm, w_attn_out, w_out, g_norm2, peer_wq, peer_k1, peer_k2, peer_u, peer_v):
    bf = jnp.bfloat16
    o1 = 3 * D_CONV
    o2 = o1 + Q_LORA
    o3 = o2 + KV_LORA
    o4 = o3 + QK_ROPE
    o5 = o4 + D_MODEL
    kr_slot = jnp.zeros((D_MODEL, HEAD_SLOT), w_in.dtype).at[:, QK_NOPE:QK_HEAD].set(w_in[:, o3:o4])
    w_in_r = jnp.concatenate([w_in[:, :o3], kr_slot, w_in[:, o4:]], axis=1).astype(bf)
    w_ukv3 = w_ukv.reshape(KV_LORA, N_HEADS, QK_NOPE + V_HEAD)
    w_uk = _pad_heads(w_ukv3[:, :, :QK_NOPE].reshape(KV_LORA, -1), QK_NOPE)
    w_uv = _pad_heads(w_ukv3[:, :, QK_NOPE:].reshape(KV_LORA, -1), V_HEAD)
    pad_gain = lambda g: jnp.zeros((1, HEAD_SLOT), jnp.float32).at[0, :QK_HEAD].set(g)
    w_ao = jnp.zeros((N_HEADS, HEAD_SLOT, D_MODEL), w_attn_out.dtype).at[:, :V_HEAD].set(
        w_attn_out.reshape(N_HEADS, V_HEAD, D_MODEL)).reshape(N_HEADS * HEAD_SLOT, D_MODEL)
    half = PEER_DK // 2
    kcat = jnp.zeros((PEER_HEADS, 2 * N_KEYS, PEER_DK), jnp.float32)
    kcat = kcat.at[:, :N_KEYS, :half].set(peer_k1).at[:, N_KEYS:, half:].set(peer_k2)
    tab_u = _pack_table(peer_u)
    tab_v = _pack_table(peer_v)
    return {
        "g_norm1": g_norm1.reshape(1, -1), "w_in": w_in_r, "conv_w": conv_w,
        "w_conv_out": w_conv_out.astype(bf), "g_q_lora": g_q_lora.reshape(1, -1),
        "w_uq": _pad_heads(w_uq, QK_HEAD).astype(bf), "g_kv_lora": g_kv_lora.reshape(1, -1),
        "w_uk": w_uk.astype(bf), "w_uv": w_uv.astype(bf),
        "g_qnorm": pad_gain(g_qnorm), "g_knorm": pad_gain(g_knorm),
        "w_attn_out": w_ao.astype(bf), "w_out": w_out.astype(bf), "g_norm2": g_norm2.reshape(1, -1),
        "peer_wq": peer_wq.astype(bf), "kcat": kcat.astype(bf),
        "tab_u": tab_u, "tab_v": tab_v,
        "tab_u_sc": tab_u.reshape(-1, ROW_WORDS), "tab_v_sc": tab_v.reshape(-1, ROW_WORDS),
    }


def _rope_tables(seq):
    pos = jnp.arange(seq, dtype=jnp.float32)
    inv = ROPE_THETA ** (-jnp.arange(0, QK_ROPE, 2, dtype=jnp.float32) / QK_ROPE)
    ang = pos[:, None] * inv[None, :]
    cos, sin = jnp.cos(ang), jnp.sin(ang)
    t1 = slice(QK_NOPE, QK_NOPE + HALF_ROPE)
    t2 = slice(QK_NOPE + HALF_ROPE, QK_HEAD)
    cos_t = jnp.ones((seq, HEAD_SLOT), jnp.float32).at[:, t1].set(cos).at[:, t2].set(cos)
    sin_a = jnp.zeros((seq, HEAD_SLOT), jnp.float32).at[:, t1].set(-sin)
    sin_b = jnp.zeros((seq, HEAD_SLOT), jnp.float32).at[:, t2].set(sin)
    return cos_t, sin_a, sin_b


def _front(x, c, w_ada, b_ada, wts, after):
    bsz, seq, d = x.shape
    n = bsz * seq
    ada = _ada(c, w_ada, b_ada)[:, None, :]
    shift1, scale1, gate1, shift2, scale2, gate2 = jnp.split(ada, 6, axis=-1)
    q, k, v, a, sgb = _inproj(x, scale1, shift1, wts, _rope_tables(seq), after)
    o = _attention(q, k, v)
    x1, h2, e, g = _mix(x, a, sgb, o, gate1, scale2, shift2, wts)
    return dict(x1=x1.reshape(n, d), h2=h2.reshape(n, d), e=e.reshape(n, N_PICKS),
                g=g.reshape(n, N_PICKS), gate2=gate2, seq=seq)


def _peer_on_sparsecores(f, wts):
    ids = lax.shift_right_logical(f["e"], ROW_SUB.bit_length() - 1)
    peer = _peer_sc(ids.reshape(-1, 2, SC_HALF), f["g"], f["h2"], wts["tab_u_sc"], wts["tab_v_sc"])
    return _peer_finish(f["x1"], f["gate2"], peer, f["seq"])


def _peer_on_tensorcore(f, wts):
    e = f["e"].reshape(-1)
    w = _peer_u(e, f["h2"], f["g"], wts["tab_u"])
    return _peer_v(e, w, f["x1"], f["gate2"], wts["tab_v"], f["seq"])


def kernel(x_prompt, x_sample, c_prompt, c_sample, w_ada, b_ada, g_norm1, w_in, conv_w, w_conv_out, g_q_lora, w_uq, g_kv_lora, w_ukv, g_qnorm, g_knorm, w_attn_out, w_out, g_norm2, peer_wq, peer_k1, peer_k2, peer_u, peer_v):
    wts = _prep_weights(g_norm1[0], w_in[0], conv_w[0], w_conv_out[0], g_q_lora[0], w_uq[0],
                        g_kv_lora[0], w_ukv[0], g_qnorm[0], g_knorm[0], w_attn_out[0], w_out[0],
                        g_norm2[0], peer_wq[0], peer_k1[0], peer_k2[0], peer_u[0], peer_v[0])
    sc_tables = (wts["tab_u_sc"], wts["tab_v_sc"])
    fs = _front(x_sample, c_sample, w_ada[0], b_ada[0], wts, sc_tables)
    y_sample = _peer_on_sparsecores(fs, wts)
    fp = _front(x_prompt, c_prompt, w_ada[0], b_ada[0], wts, sc_tables + (fs["g"],))
    y_prompt = _peer_on_tensorcore(fp, wts)
    return y_prompt.reshape(x_prompt.shape), y_sample.reshape(x_sample.shape)
```

```python
import dataclasses
import functools
import math

import jax
import jax.numpy as jnp
import numpy as np
from jax import lax
from jax.experimental import pallas as pl
from jax.experimental.pallas import tpu as pltpu
from jax.experimental.pallas import tpu_sc as plsc

D_MODEL = 1024
D_CONV = 512
N_HEADS = 8
QK_NOPE = 64
QK_ROPE = 32
V_HEAD = 64
Q_LORA = 256
KV_LORA = 128
QK_HEAD = QK_NOPE + QK_ROPE
ROPE_THETA = 10000.0
PEER_HEADS = 8
N_KEYS = 128
PEER_DK = 128
PEER_TOPK = 16
EPS = 1e-6

LANES = 128
SUBLANES = 8
HEAD_SLOT = LANES
HALF_ROPE = QK_ROPE // 2
N_PICKS = PEER_HEADS * PEER_TOPK
ROW_WORDS = D_MODEL // 2
ROW_SUB = ROW_WORDS // LANES
VMEM_LIMIT = 56 * 1024 * 1024

C_BG, C_CG, C_HC = 0, 512, 1024
C_CQ = 1536
C_CKV = C_CQ + Q_LORA
C_KR = C_CKV + KV_LORA
C_GA = C_KR + HEAD_SLOT
C_GB = C_GA + D_MODEL
IN_COLS_R = C_GB + D_MODEL

TS_IN = 512
TQ = 1024
TK = 2048
KV_UNROLL = 2
TS_MIX = 128
TT_PEER = 128
TOK_UNROLL_U = 32
TOK_UNROLL_V = 32
IDX_GROUP = 8

_NEG_INF = float("-inf")


def _cparams(sem):
    return pltpu.CompilerParams(dimension_semantics=sem, vmem_limit_bytes=VMEM_LIMIT)


def _ada_kernel(c_ref, w_ref, b_ref, o_ref):
    o_ref[...] = jnp.dot(c_ref[...], w_ref[...], preferred_element_type=jnp.float32,
                         precision=lax.Precision.HIGHEST) + b_ref[...]


def _ada(c, w_ada, b_ada):
    bsz = c.shape[0]
    rows = -(-bsz // SUBLANES) * SUBLANES
    c_p = jnp.zeros((rows, D_MODEL), jnp.float32).at[:bsz].set(c)
    out = pl.pallas_call(
        _ada_kernel,
        grid=(6,),
        in_specs=[pl.BlockSpec((rows, D_MODEL), lambda j: (0, 0)),
                  pl.BlockSpec((D_MODEL, D_MODEL), lambda j: (0, j)),
                  pl.BlockSpec((1, D_MODEL), lambda j: (0, j))],
        out_specs=pl.BlockSpec((rows, D_MODEL), lambda j: (0, j)),
        out_shape=jax.ShapeDtypeStruct((rows, 6 * D_MODEL), jnp.float32),
        compiler_params=_cparams(("arbitrary",)),
        name="ada",
    )(c_p, w_ada, b_ada.reshape(1, -1))
    return out[:bsz]


def _adaln(x, g, scale, shift):
    ms = jnp.mean(x * x, axis=-1, keepdims=True)
    return (x * lax.rsqrt(ms + EPS) * g) * (1.0 + scale) + shift


def _rms_rows(x, g, n):
    ms = jnp.sum(x * x, axis=-1, keepdims=True) * (1.0 / n)
    return x * lax.rsqrt(ms + EPS) * g


def _rope(x, cos, sin_a, sin_b):
    return (x * cos + pltpu.roll(x, LANES - HALF_ROPE, axis=1) * sin_a
            + pltpu.roll(x, HALF_ROPE, axis=1) * sin_b)


def _inproj_kernel(x_ref, xp_ref, xn_ref, scale_ref, shift_ref, g1_ref, w_in_ref, conv_w_ref,
                   w_co_ref, gql_ref, w_uq_ref, gkvl_ref, w_uk_ref, w_uv_ref, gqn_ref, gkn_ref,
                   cos_ref, sa_ref, sb_ref, *rest):
    q_ref, k_ref, v_ref, a_ref, sgb_ref = rest[-5:]
    i = pl.program_id(1)
    n_i = pl.num_programs(1)
    g1 = g1_ref[...]
    scale = scale_ref[...]
    shift = shift_ref[...]
    ts = x_ref.shape[0]

    h = _adaln(x_ref[...], g1, scale, shift).astype(jnp.bfloat16)

    def proj(lo, width):
        return jnp.dot(h, w_in_ref[:, lo:lo + width], preferred_element_type=jnp.float32)

    z = proj(C_CG, D_CONV) * proj(C_HC, D_CONV)

    def halo_z(xh_ref):
        hh = _adaln(xh_ref[...], g1, scale, shift).astype(jnp.bfloat16)
        zc = jnp.dot(hh, w_in_ref[:, C_CG:C_CG + 2 * D_CONV], preferred_element_type=jnp.float32)
        return zc[:, :D_CONV] * zc[:, D_CONV:]

    z_prev = halo_z(xp_ref)[SUBLANES - 1:SUBLANES, :]
    z_next = halo_z(xn_ref)[0:1, :]
    z_prev = jnp.where(i == 0, 0.0, z_prev)
    z_next = jnp.where(i == n_i - 1, 0.0, z_next)
    row = lax.broadcasted_iota(jnp.int32, (ts, D_CONV), 0)
    z_up = jnp.where(row == 0, z_prev, pltpu.roll(z, 1, axis=0))
    z_dn = jnp.where(row == ts - 1, z_next, pltpu.roll(z, ts - 1, axis=0))
    cw = conv_w_ref[...]
    y = z_up * cw[0:1, :] + z * cw[1:2, :] + z_dn * cw[2:3, :]
    out_a = jnp.dot((proj(C_BG, D_CONV) * y).astype(jnp.bfloat16), w_co_ref[...],
                    preferred_element_type=jnp.float32)
    a_ref[...] = jax.nn.sigmoid(proj(C_GA, D_MODEL)) * out_a
    sgb_ref[...] = jax.nn.sigmoid(proj(C_GB, D_MODEL))

    cos = cos_ref[...]
    sin_a = sa_ref[...]
    sin_b = sb_ref[...]
    cq = _rms_rows(proj(C_CQ, Q_LORA), gql_ref[...], Q_LORA).astype(jnp.bfloat16)
    qf = jnp.dot(cq, w_uq_ref[...], preferred_element_type=jnp.float32)
    ckv = _rms_rows(proj(C_CKV, KV_LORA), gkvl_ref[...], KV_LORA).astype(jnp.bfloat16)
    kf = jnp.dot(ckv, w_uk_ref[...], preferred_element_type=jnp.float32)
    lane = lax.broadcasted_iota(jnp.int32, (1, N_HEADS * HEAD_SLOT), 1)
    ones_lane = (jnp.bitwise_and(lane, HEAD_SLOT - 1) == V_HEAD).astype(jnp.float32)
    v_ref[...] = (jnp.dot(ckv, w_uv_ref[...], preferred_element_type=jnp.float32)
                  + ones_lane).astype(jnp.bfloat16)
    kr = proj(C_KR, HEAD_SLOT)
    gqn = gqn_ref[...]
    gkn = gkn_ref[...]
    q_scale = QK_HEAD ** -0.5 * math.log2(math.e)
    for hd in range(N_HEADS):
        sl = slice(hd * HEAD_SLOT, (hd + 1) * HEAD_SLOT)
        qh = _rope(_rms_rows(qf[:, sl], gqn, QK_HEAD), cos, sin_a, sin_b)
        q_ref[:, sl] = (qh * q_scale).astype(jnp.bfloat16)
        kh = _rope(_rms_rows(kf[:, sl] + kr, gkn, QK_HEAD), cos, sin_a, sin_b)
        k_ref[:, sl] = kh.astype(jnp.bfloat16)


def _inproj(x, scale1, shift1, wts, rope, after):
    bsz, seq, _ = x.shape
    ts = min(TS_IN, seq)
    n_i = seq // ts
    nb8 = seq // SUBLANES
    per8 = ts // SUBLANES
    tile = lambda b, i: (b, i, 0)
    per_b = lambda b, i: (b, 0, 0)
    full2 = lambda b, i: (0, 0)
    hw = N_HEADS * HEAD_SLOT
    in_specs = [
        pl.BlockSpec((None, ts, D_MODEL), tile),
        pl.BlockSpec((None, SUBLANES, D_MODEL), lambda b, i: (b, jnp.maximum(i * per8 - 1, 0), 0)),
        pl.BlockSpec((None, SUBLANES, D_MODEL), lambda b, i: (b, jnp.minimum((i + 1) * per8, nb8 - 1), 0)),
        pl.BlockSpec((None, 1, D_MODEL), per_b),
        pl.BlockSpec((None, 1, D_MODEL), per_b),
        pl.BlockSpec((1, D_MODEL), full2),
        pl.BlockSpec((D_MODEL, IN_COLS_R), full2),
        pl.BlockSpec((3, D_CONV), full2),
        pl.BlockSpec((D_CONV, D_MODEL), full2),
        pl.BlockSpec((1, Q_LORA), full2),
        pl.BlockSpec((Q_LORA, hw), full2),
        pl.BlockSpec((1, KV_LORA), full2),
        pl.BlockSpec((KV_LORA, hw), full2),
        pl.BlockSpec((KV_LORA, hw), full2),
        pl.BlockSpec((1, HEAD_SLOT), full2),
        pl.BlockSpec((1, HEAD_SLOT), full2),
        pl.BlockSpec((ts, HEAD_SLOT), lambda b, i: (i, 0)),
        pl.BlockSpec((ts, HEAD_SLOT), lambda b, i: (i, 0)),
        pl.BlockSpec((ts, HEAD_SLOT), lambda b, i: (i, 0)),
    ] + [pl.BlockSpec(memory_space=pl.ANY)] * len(after)
    out_specs = [pl.BlockSpec((None, ts, hw), tile)] * 3 + [pl.BlockSpec((None, ts, D_MODEL), tile)] * 2
    out_shape = ([jax.ShapeDtypeStruct((bsz, seq, hw), jnp.bfloat16)] * 3
                 + [jax.ShapeDtypeStruct((bsz, seq, D_MODEL), jnp.float32)] * 2)
    return pl.pallas_call(
        _inproj_kernel,
        grid=(bsz, n_i),
        in_specs=in_specs,
        out_specs=out_specs,
        out_shape=out_shape,
        compiler_params=_cparams(("parallel", "parallel")),
        name="inproj",
    )(x, x, x, scale1, shift1, wts["g_norm1"], wts["w_in"], wts["conv_w"], wts["w_conv_out"],
      wts["g_q_lora"], wts["w_uq"], wts["g_kv_lora"], wts["w_uk"], wts["w_uv"],
      wts["g_qnorm"], wts["g_knorm"], rope[0], rope[1], rope[2], *after)


def _attn_kernel(q_ref, k_ref, v_ref, o_ref):
    q = q_ref[...]
    tq = q.shape[0]
    seq = k_ref.shape[0]
    tk = min(TK, seq)

    n_chunks = seq // tk
    unroll = KV_UNROLL if n_chunks % KV_UNROLL == 0 else 1

    def chunk(off, carry):
        m, acc = carry
        kc = k_ref[pl.ds(off, tk), :]
        vc = v_ref[pl.ds(off, tk), :]
        s = lax.dot_general(q, kc, (((1,), (1,)), ((), ())), preferred_element_type=jnp.float32)
        m_new = jnp.maximum(m, jnp.max(s, axis=-1, keepdims=True))
        p = jnp.exp2(s - m_new).astype(jnp.bfloat16)
        alpha = jnp.exp2(m - m_new)
        acc = alpha * acc + jnp.dot(p, vc, preferred_element_type=jnp.float32)
        return m_new, acc

    def body(j, carry):
        for u in range(unroll):
            carry = chunk(pl.multiple_of((j * unroll + u) * tk, tk), carry)
        return carry

    m0 = jnp.full((tq, 1), _NEG_INF, jnp.float32)
    acc0 = jnp.zeros((tq, HEAD_SLOT), jnp.float32)
    _, acc = lax.fori_loop(0, n_chunks // unroll, body, (m0, acc0))
    o_ref[...] = (acc / acc[:, V_HEAD:V_HEAD + 1]).astype(jnp.bfloat16)


def _attention(q, k, v):
    bsz, seq, hw = q.shape
    tq = min(TQ, seq)
    return pl.pallas_call(
        _attn_kernel,
        grid=(bsz, N_HEADS, seq // tq),
        in_specs=[pl.BlockSpec((None, tq, HEAD_SLOT), lambda b, h, i: (b, i, h)),
                  pl.BlockSpec((None, seq, HEAD_SLOT), lambda b, h, i: (b, 0, h)),
                  pl.BlockSpec((None, seq, HEAD_SLOT), lambda b, h, i: (b, 0, h))],
        out_specs=pl.BlockSpec((None, tq, HEAD_SLOT), lambda b, h, i: (b, i, h)),
        out_shape=jax.ShapeDtypeStruct((bsz, seq, hw), jnp.bfloat16),
        compiler_params=_cparams(("parallel", "parallel", "arbitrary")),
        name="attn",
    )(q, k, v)


def _topk_rows(s, iota, iota_blocks):
    vals, idxs = [], []
    big = jnp.float32(1e9)
    n_blk = s.shape[0] // SUBLANES
    for _ in range(PEER_TOPK):
        vb = [s[j * SUBLANES:(j + 1) * SUBLANES, :] for j in range(n_blk)]
        ib = list(iota_blocks)
        while len(vb) > 1:
            nv, ni = [], []
            for a in range(0, len(vb) - 1, 2):
                keep = vb[a] >= vb[a + 1]
                nv.append(jnp.where(keep, vb[a], vb[a + 1]))
                ni.append(jnp.where(keep, ib[a], ib[a + 1]))
            if len(vb) % 2:
                nv.append(vb[-1])
                ni.append(ib[-1])
            vb, ib = nv, ni
        m = jnp.max(vb[0], axis=0, keepdims=True)
        am = jnp.min(jnp.where(vb[0] == m, ib[0], big), axis=0, keepdims=True)
        vals.append(m)
        idxs.append(am)
        s = jnp.where(iota == am, _NEG_INF, s)
    return jnp.concatenate(vals, axis=0), jnp.concatenate(idxs, axis=0)


def _take_rows(tab, idx):
    out = jnp.zeros_like(tab)
    for a in range(tab.shape[0]):
        out = out + jnp.where(idx == jnp.float32(a), tab[a:a + 1, :], 0.0)
    return out


_CAND = [(a, b) for a in range(PEER_TOPK) for b in range(PEER_TOPK) if (a + 1) * (b + 1) <= PEER_TOPK]
N_CAND = -(-len(_CAND) // SUBLANES) * SUBLANES
_PAD_LABEL = float(PEER_TOPK * PEER_TOPK)


def _candidate_tables():
    sel = np.zeros((2, N_CAND, PEER_TOPK), np.float32)
    lab = np.full((N_CAND, 1), _PAD_LABEL, np.float32)
    for r, (a, b) in enumerate(_CAND):
        sel[0, r, a] = 1.0
        sel[1, r, b] = 1.0
        lab[r, 0] = a * PEER_TOPK + b
    return sel, lab


def _mix_kernel(x_ref, a_ref, sgb_ref, o_ref, gate1_ref, scale2_ref, shift2_ref, g2_ref,
                w_ao_ref, w_out_ref, w_pq_ref, kcat_ref, sel_ref, lab_ref,
                x1_ref, h2_ref, e_ref, g_ref):
    ts = x_ref.shape[0]
    out_b = jnp.dot(o_ref[...], w_ao_ref[...], preferred_element_type=jnp.float32)
    merged = (a_ref[...] + sgb_ref[...] * out_b).astype(jnp.bfloat16)
    x1 = x_ref[...] + gate1_ref[...] * jnp.dot(merged, w_out_ref[...],
                                              preferred_element_type=jnp.float32)
    x1_ref[...] = x1
    h2 = _adaln(x1, g2_ref[...], scale2_ref[...], shift2_ref[...])
    h2_ref[...] = h2
    qp = jnp.dot(h2.astype(jnp.bfloat16), w_pq_ref[...],
                 preferred_element_type=jnp.float32).astype(jnp.bfloat16)

    iota_k = lax.broadcasted_iota(jnp.int32, (N_KEYS, ts), 0).astype(jnp.float32)
    iota8 = lax.broadcasted_iota(jnp.int32, (SUBLANES, ts), 0).astype(jnp.float32)
    iota_k_blocks = [iota8 + float(SUBLANES * j) for j in range(N_KEYS // SUBLANES)]
    lab = lab_ref[...]
    lab_blocks = [lab_ref[j * SUBLANES:(j + 1) * SUBLANES, :] for j in range(N_CAND // SUBLANES)]
    valid = lab < _PAD_LABEL
    sel_a = sel_ref[0]
    sel_b = sel_ref[1]
    exact = dict(preferred_element_type=jnp.float32, precision=lax.Precision.HIGHEST)
    e_rows, g_rows = [], []
    for hd in range(PEER_HEADS):
        qh = qp[:, hd * PEER_DK:(hd + 1) * PEER_DK]
        st = lax.dot_general(kcat_ref[hd], qh, (((1,), (1,)), ((), ())),
                             preferred_element_type=jnp.float32)
        v1, i1 = _topk_rows(st[:N_KEYS], iota_k, iota_k_blocks)
        v2, i2 = _topk_rows(st[N_KEYS:], iota_k, iota_k_blocks)
        comb = jnp.dot(sel_a, v1, **exact) + jnp.dot(sel_b, v2, **exact)
        sc, pos = _topk_rows(jnp.where(valid, comb, _NEG_INF), lab, lab_blocks)
        ia = jnp.floor(pos * (1.0 / PEER_TOPK))
        ib = pos - ia * PEER_TOPK
        e_rows.append(_take_rows(i1, ia) * N_KEYS + _take_rows(i2, ib))
        p = jnp.exp(sc - sc[0:1, :])
        g_rows.append(p / jnp.sum(p, axis=0, keepdims=True))
    e_all = jnp.concatenate(e_rows, axis=0)
    g_all = jnp.concatenate(g_rows, axis=0)
    for c in range(ts // LANES):
        cs = slice(c * LANES, (c + 1) * LANES)
        e_ref[cs, :] = (e_all[:, cs].T * ROW_SUB).astype(jnp.int32)
        g_ref[cs, :] = g_all[:, cs].T


def _mix(x, a, sgb, o, gate1, scale2, shift2, wts):
    bsz, seq, _ = x.shape
    ts = min(TS_MIX, seq)
    tile = lambda b, i: (b, i, 0)
    per_b = lambda b, i: (b, 0, 0)
    full2 = lambda b, i: (0, 0)
    hw = N_HEADS * HEAD_SLOT
    big = pl.BlockSpec((None, ts, D_MODEL), tile)
    vec = pl.BlockSpec((None, 1, D_MODEL), per_b)
    picks = pl.BlockSpec((None, ts, N_PICKS), tile)
    sel, lab = _candidate_tables()
    return pl.pallas_call(
        _mix_kernel,
        grid=(bsz, seq // ts),
        in_specs=[big, big, big, pl.BlockSpec((None, ts, hw), tile), vec, vec, vec,
                  pl.BlockSpec((1, D_MODEL), full2),
                  pl.BlockSpec((hw, D_MODEL), full2),
                  pl.BlockSpec((D_MODEL, D_MODEL), full2),
                  pl.BlockSpec((D_MODEL, PEER_HEADS * PEER_DK), full2),
                  pl.BlockSpec((PEER_HEADS, 2 * N_KEYS, PEER_DK), lambda b, i: (0, 0, 0)),
                  pl.BlockSpec((2, N_CAND, PEER_TOPK), lambda b, i: (0, 0, 0)),
                  pl.BlockSpec((N_CAND, ts), full2)],
        out_specs=[big, big, picks, picks],
        out_shape=[jax.ShapeDtypeStruct((bsz, seq, D_MODEL), jnp.float32),
                   jax.ShapeDtypeStruct((bsz, seq, D_MODEL), jnp.float32),
                   jax.ShapeDtypeStruct((bsz, seq, N_PICKS), jnp.int32),
                   jax.ShapeDtypeStruct((bsz, seq, N_PICKS), jnp.float32)],
        compiler_params=_cparams(("parallel", "parallel")),
        name="mix",
    )(x, a, sgb, o, gate1, scale2, shift2, wts["g_norm2"], wts["w_attn_out"], wts["w_out"],
      wts["peer_wq"], wts["kcat"], jnp.asarray(sel), jnp.broadcast_to(jnp.asarray(lab), (N_CAND, ts)))


def _unpack_row(w):
    lo = lax.bitcast_convert_type(lax.shift_left(w, 16), jnp.float32)
    hi = lax.bitcast_convert_type(jnp.bitwise_and(w, jnp.int32(-65536)), jnp.float32)
    return lo, hi


def _gelu_tanh(x):
    c = math.sqrt(2.0 / math.pi)
    return 0.5 * x * (1.0 + jnp.tanh(c * (x + 0.044715 * (x * x * x))))


def _gather_rows(e_ref, t, tab_ref, pb):
    for j in range(N_PICKS // IDX_GROUP):
        row = e_ref.at[pl.ds(pl.multiple_of(t * N_PICKS + j * IDX_GROUP, IDX_GROUP), IDX_GROUP)]
        for c in range(IDX_GROUP):
            k = j * IDX_GROUP + c
            r = pl.multiple_of(row[c], ROW_SUB)
            pb[k * ROW_SUB:(k + 1) * ROW_SUB, :] = tab_ref[pl.ds(r, ROW_SUB), :]


def _pick_block(pb, s):
    return pb[pl.ds(s, N_PICKS, stride=ROW_SUB), :]


def _peer_u_kernel(e_ref, x_ref, g_ref, tab_ref, w_ref, pbuf0, pbuf1, abuf):
    tt = g_ref.shape[0]
    pbufs = (pbuf0, pbuf1)
    lane = lax.broadcasted_iota(jnp.int32, (N_PICKS, tt), 1)
    abuf[...] = jnp.zeros_like(abuf)

    def tok(i, carry):
        x_all = x_ref[pl.ds(pl.multiple_of(i * TOK_UNROLL_U, TOK_UNROLL_U), TOK_UNROLL_U), :]
        for u in range(TOK_UNROLL_U):
            t = i * TOK_UNROLL_U + u
            pb = pbufs[u % 2]
            _gather_rows(e_ref, t, tab_ref, pb)
            acc = None
            for s in range(ROW_SUB):
                lo, hi = _unpack_row(_pick_block(pb, s))
                x_lo = x_all[u:u + 1, s * LANES:(s + 1) * LANES]
                x_hi = x_all[u:u + 1, ROW_WORDS + s * LANES:ROW_WORDS + (s + 1) * LANES]
                term = lo * x_lo + hi * x_hi
                acc = term if acc is None else acc + term
            a_col = jnp.sum(acc, axis=1, keepdims=True)
            abuf[...] = jnp.where(lane == t, a_col, abuf[...])
        return carry

    lax.fori_loop(0, tt // TOK_UNROLL_U, tok, 0)
    w_ref[...] = g_ref[...] * _gelu_tanh(abuf[...].T)


def _peer_u(e, h2, g, tab):
    n = h2.shape[0]
    tt = TT_PEER
    return pl.pallas_call(
        _peer_u_kernel,
        grid=(n // tt,),
        in_specs=[pl.BlockSpec((tt * N_PICKS,), lambda i: (i,), memory_space=pltpu.SMEM),
                  pl.BlockSpec((tt, D_MODEL), lambda i: (i, 0)),
                  pl.BlockSpec((tt, N_PICKS), lambda i: (i, 0)),
                  pl.BlockSpec(memory_space=pltpu.VMEM)],
        out_specs=pl.BlockSpec((tt, N_PICKS), lambda i: (i, 0)),
        out_shape=jax.ShapeDtypeStruct((n, N_PICKS), jnp.float32),
        scratch_shapes=[pltpu.VMEM((N_PICKS * ROW_SUB, LANES), jnp.int32),
                        pltpu.VMEM((N_PICKS * ROW_SUB, LANES), jnp.int32),
                        pltpu.VMEM((N_PICKS, tt), jnp.float32)],
        compiler_params=_cparams(("arbitrary",)),
        name="peer_u",
    )(e, h2, g, tab)


def _peer_v_kernel(e_ref, w_ref, x_ref, gate_ref, tab_ref, y_ref, pbuf0, pbuf1, wt_ref, acc_ref):
    tt = x_ref.shape[0]
    pbufs = (pbuf0, pbuf1)
    wt_ref[...] = w_ref[...].T
    lane = lax.broadcasted_iota(jnp.int32, (N_PICKS, tt), 1)
    sub = lax.broadcasted_iota(jnp.int32, (SUBLANES, D_MODEL), 0)
    acc_ref[...] = jnp.zeros_like(acc_ref)

    def process(pb, t):
        w_col = jnp.sum(jnp.where(lane == t, wt_ref[...], 0.0), axis=1, keepdims=True)
        los, his = [], []
        for s in range(ROW_SUB):
            lo, hi = _unpack_row(_pick_block(pb, s))
            los.append(jnp.sum(lo * w_col, axis=0, keepdims=True))
            his.append(jnp.sum(hi * w_col, axis=0, keepdims=True))
        row = jnp.concatenate(los + his, axis=1)
        b0 = pl.multiple_of(jnp.bitwise_and(t, -SUBLANES), SUBLANES)
        mask = sub == jnp.bitwise_and(t, SUBLANES - 1)
        acc_ref[pl.ds(b0, SUBLANES), :] = jnp.where(mask, row, acc_ref[pl.ds(b0, SUBLANES), :])

    def step(i, carry):
        t0 = i * TOK_UNROLL_V
        for u in range(TOK_UNROLL_V):
            _gather_rows(e_ref, t0 + u, tab_ref, pbufs[u % 2])
            process(pbufs[u % 2], t0 + u)
        return carry

    lax.fori_loop(0, tt // TOK_UNROLL_V, step, 0)
    y_ref[...] = x_ref[...] + gate_ref[...] * acc_ref[...]


def _peer_v(e, w, x1, gate2, tab, seq):
    n = x1.shape[0]
    tt = TT_PEER
    return pl.pallas_call(
        _peer_v_kernel,
        grid=(n // tt,),
        in_specs=[pl.BlockSpec((tt * N_PICKS,), lambda i: (i,), memory_space=pltpu.SMEM),
                  pl.BlockSpec((tt, N_PICKS), lambda i: (i, 0)),
                  pl.BlockSpec((tt, D_MODEL), lambda i: (i, 0)),
                  pl.BlockSpec((None, 1, D_MODEL), lambda i: ((i * tt) // seq, 0, 0)),
                  pl.BlockSpec(memory_space=pltpu.VMEM)],
        out_specs=pl.BlockSpec((tt, D_MODEL), lambda i: (i, 0)),
        out_shape=jax.ShapeDtypeStruct((n, D_MODEL), jnp.float32),
        scratch_shapes=[pltpu.VMEM((N_PICKS * ROW_SUB, LANES), jnp.int32),
                        pltpu.VMEM((N_PICKS * ROW_SUB, LANES), jnp.int32),
                        pltpu.VMEM((N_PICKS, tt), jnp.float32),
                        pltpu.VMEM((tt, D_MODEL), jnp.float32)],
        compiler_params=_cparams(("arbitrary",)),
        name="peer_v",
    )(e, w, x1, gate2, tab)


SC_CORES = 2
SC_SUBCORES = 16
SC_LANES = 16
SC_WORKERS = SC_CORES * SC_SUBCORES
SC_HALF = N_PICKS // 2
SC_CHUNKS = ROW_WORDS // SC_LANES
SC_ACC_CHUNKS = 8


def _sc_params():
    cp = pltpu.CompilerParams()
    if "needs_layout_passes" in pltpu.CompilerParams.__dataclass_fields__:
        cp = dataclasses.replace(cp, needs_layout_passes=False)
    return cp


def _gelu_tanh_via_exp(x):
    c = math.sqrt(2.0 / math.pi)
    z = c * (x + 0.044715 * (x * x * x))
    t = jnp.exp(-2.0 * jnp.abs(z))
    return 0.5 * x * (1.0 + jnp.sign(z) * (1.0 - t) / (1.0 + t))


def _peer_sc(e2, g, h2, tab_u, tab_v):
    n_sc = e2.shape[0]
    assert n_sc % (2 * SC_WORKERS) == 0
    per_w = n_sc // SC_WORKERS
    mesh = plsc.VectorSubcoreMesh(core_axis_name="c", subcore_axis_name="s")
    ln = SC_LANES
    per_token_inputs = lambda: [pltpu.VMEM((SC_HALF,), jnp.int32), pltpu.VMEM((SC_HALF,), jnp.int32),
                                pltpu.VMEM((D_MODEL,), jnp.float32), pltpu.VMEM((N_PICKS,), jnp.float32)]

    @functools.partial(
        pl.kernel, mesh=mesh,
        out_type=jax.ShapeDtypeStruct((n_sc, D_MODEL), jnp.float32),
        scratch_types=per_token_inputs() + per_token_inputs() + [
            pltpu.VMEM((SC_HALF, ROW_WORDS), jnp.int32), pltpu.VMEM((SC_HALF, ROW_WORDS), jnp.int32),
            pltpu.VMEM((N_PICKS,), jnp.float32), pltpu.VMEM((D_MODEL,), jnp.float32),
            pltpu.SemaphoreType.DMA, pltpu.SemaphoreType.DMA,
            pltpu.SemaphoreType.DMA, pltpu.SemaphoreType.DMA],
        compiler_params=_sc_params(),
        cost_estimate=pl.CostEstimate(
            flops=4 * n_sc * N_PICKS * D_MODEL,
            transcendentals=n_sc * N_PICKS,
            bytes_accessed=n_sc * (2 * N_PICKS * ROW_WORDS * 4 + 3 * D_MODEL * 4 + 3 * N_PICKS * 4)),
    )
    def body(e_hbm, g_hbm, x_hbm, tu_hbm, tv_hbm, out_hbm,
             ia0, ib0, x0, g0, ia1, ib1, x1, g1, buf_a, buf_b, w_v, o_v, sem_a, sem_b, sem_in0, sem_in1):
        wid = lax.axis_index("s") * SC_CORES + lax.axis_index("c")
        base = wid * per_w
        lane = lax.iota(jnp.int32, ln)
        zero = jnp.zeros((ln,), jnp.float32)
        in_sets = ((ia0, ib0, x0, g0, sem_in0), (ia1, ib1, x1, g1, sem_in1))

        def input_copies(st, i):
            ia, ib, xv, gv, sem = st
            tl = base + jnp.minimum(i, per_w - 1)
            return [pltpu.make_async_copy(e_hbm.at[tl, 0], ia, sem),
                    pltpu.make_async_copy(e_hbm.at[tl, 1], ib, sem),
                    pltpu.make_async_copy(x_hbm.at[tl], xv, sem),
                    pltpu.make_async_copy(g_hbm.at[tl], gv, sem)]

        def u_half(buf, k0, x_v, g_v):
            @pl.loop(0, SC_HALF // ln)
            def _(grp):
                def chunk(j, accs):
                    xl = x_v[pl.ds(j * ln, ln)]
                    xh = x_v[pl.ds(ROW_WORDS + j * ln, ln)]
                    new = []
                    for kk in range(ln):
                        lo, hi = _unpack_row(buf[grp * ln + kk, pl.ds(j * ln, ln)])
                        new.append(accs[kk] + lo * xl + hi * xh)
                    return tuple(new)

                accs = lax.fori_loop(0, SC_CHUNKS, chunk, tuple(zero for _ in range(ln)))
                a = zero
                for kk in range(ln):
                    a = jnp.where(lane == kk, jnp.sum(accs[kk]), a)
                sl = pl.ds(k0 + grp * ln, ln)
                w_v[sl] = g_v[sl] * _gelu_tanh_via_exp(a)

        def v_half(buf, k0):
            @pl.loop(0, SC_CHUNKS // SC_ACC_CHUNKS)
            def _(jb):
                def pick(kq, accs):
                    wk = plsc.load_gather(w_v, [jnp.full((ln,), k0, jnp.int32) + kq])
                    new = []
                    for c in range(SC_ACC_CHUNKS):
                        lo, hi = _unpack_row(buf[kq, pl.ds((jb * SC_ACC_CHUNKS + c) * ln, ln)])
                        new.append(accs[2 * c] + wk * lo)
                        new.append(accs[2 * c + 1] + wk * hi)
                    return tuple(new)

                accs = lax.fori_loop(0, SC_HALF, pick, tuple(zero for _ in range(2 * SC_ACC_CHUNKS)))
                for c in range(SC_ACC_CHUNKS):
                    col = (jb * SC_ACC_CHUNKS + c) * ln
                    o_v[pl.ds(col, ln)] = o_v[pl.ds(col, ln)] + accs[2 * c]
                    o_v[pl.ds(ROW_WORDS + col, ln)] = o_v[pl.ds(ROW_WORDS + col, ln)] + accs[2 * c + 1]

        def token(i, cur, nxt):
            ia, ib, x_v, g_v, _ = cur
            for c in input_copies(nxt, i + 1):
                c.start()
            u_b = pltpu.async_copy(tu_hbm.at[ib], buf_b, sem_b)

            @pl.loop(0, D_MODEL // ln)
            def _(c):
                o_v[pl.ds(c * ln, ln)] = zero

            pltpu.make_async_copy(tu_hbm.at[ia], buf_a, sem_a).wait()
            u_half(buf_a, 0, x_v, g_v)
            v_a = pltpu.async_copy(tv_hbm.at[ia], buf_a, sem_a)
            u_b.wait()
            u_half(buf_b, SC_HALF, x_v, g_v)
            v_b = pltpu.async_copy(tv_hbm.at[ib], buf_b, sem_b)
            v_a.wait()
            v_half(buf_a, 0)
            for c in input_copies(nxt, i + 1):
                c.wait()
            pltpu.async_copy(tu_hbm.at[nxt[0]], buf_a, sem_a)
            v_b.wait()
            v_half(buf_b, SC_HALF)
            pltpu.sync_copy(o_v, out_hbm.at[base + i])

        for c in input_copies(in_sets[0], 0):
            c.start()
        for c in input_copies(in_sets[0], 0):
            c.wait()
        pltpu.async_copy(tu_hbm.at[ia0], buf_a, sem_a)

        @pl.loop(0, per_w // 2)
        def _(pair):
            token(2 * pair, in_sets[0], in_sets[1])
            token(2 * pair + 1, in_sets[1], in_sets[0])

        pltpu.make_async_copy(tu_hbm.at[ia0], buf_a, sem_a).wait()

    return body(e2, g, h2, tab_u, tab_v)


def _finish_kernel(x_ref, gate_ref, p_ref, y_ref):
    y_ref[...] = x_ref[...] + gate_ref[...] * p_ref[...]


def _peer_finish(x1, gate2, peer, seq):
    n = x1.shape[0]
    tt = TT_PEER
    row = pl.BlockSpec((tt, D_MODEL), lambda i: (i, 0))
    return pl.pallas_call(
        _finish_kernel,
        grid=(n // tt,),
        in_specs=[row, pl.BlockSpec((None, 1, D_MODEL), lambda i: ((i * tt) // seq, 0, 0)), row],
        out_specs=row,
        out_shape=jax.ShapeDtypeStruct((n, D_MODEL), jnp.float32),
        compiler_params=_cparams(("arbitrary",)),
        name="peer_finish",
    )(x1, gate2, peer)


def _pad_heads(w, used, lead=0):
    r = w.shape[0]
    w3 = w.reshape(r, N_HEADS, used)
    out = jnp.zeros((r, N_HEADS, HEAD_SLOT), w.dtype).at[:, :, lead:lead + used].set(w3)
    return out.reshape(r, N_HEADS * HEAD_SLOT)


def _pack_table(t):
    tb = t.astype(jnp.bfloat16)
    lo = lax.bitcast_convert_type(tb[:, :ROW_WORDS], jnp.uint16).astype(jnp.uint32)
    hi = lax.bitcast_convert_type(tb[:, ROW_WORDS:], jnp.uint16).astype(jnp.uint32)
    words = lax.bitcast_convert_type(lo | (hi << 16), jnp.int32)
    return words.reshape(t.shape[0] * ROW_SUB, LANES)


def _prep_weights(g_norm1, w_in, conv_w, w_conv_out, g_q_lora, w_uq, g_kv_lora, w_ukv, g_qnorm,
                  g_knorm, w_attn_out, w_out, g_norm2, peer_wq, peer_k1, peer_k2, peer_u, peer_v):
    bf = jnp.bfloat16
    o1 = 3 * D_CONV
    o2 = o1 + Q_LORA
    o3 = o2 + KV_LORA
    o4 = o3 + QK_ROPE
    o5 = o4 + D_MODEL
    kr_slot = jnp.zeros((D_MODEL, HEAD_SLOT), w_in.dtype).at[:, QK_NOPE:QK_HEAD].set(w_in[:, o3:o4])
    w_in_r = jnp.concatenate([w_in[:, :o3], kr_slot, w_in[:, o4:]], axis=1).astype(bf)
    w_ukv3 = w_ukv.reshape(KV_LORA, N_HEADS, QK_NOPE + V_HEAD)
    w_uk = _pad_heads(w_ukv3[:, :, :QK_NOPE].reshape(KV_LORA, -1), QK_NOPE)
    w_uv = _pad_heads(w_ukv3[:, :, QK_NOPE:].reshape(KV_LORA, -1), V_HEAD)
    pad_gain = lambda g: jnp.zeros((1, HEAD_SLOT), jnp.float32).at[0, :QK_HEAD].set(g)
    w_ao = jnp.zeros((N_HEADS, HEAD_SLOT, D_MODEL), w_attn_out.dtype).at[:, :V_HEAD].set(
        w_attn_out.reshape(N_HEADS, V_HEAD, D_MODEL)).reshape(N_HEADS * HEAD_SLOT, D_MODEL)
    half = PEER_DK // 2
    kcat = jnp.zeros((PEER_HEADS, 2 * N_KEYS, PEER_DK), jnp.float32)
    kcat = kcat.at[:, :N_KEYS, :half].set(peer_k1).at[:, N_KEYS:, half:].set(peer_k2)
    tab_u = _pack_table(peer_u)
    tab_v = _pack_table(peer_v)
    return {
        "g_norm1": g_norm1.reshape(1, -1), "w_in": w_in_r, "conv_w": conv_w,
        "w_conv_out": w_conv_out.astype(bf), "g_q_lora": g_q_lora.reshape(1, -1),
        "w_uq": _pad_heads(w_uq, QK_HEAD).astype(bf), "g_kv_lora": g_kv_lora.reshape(1, -1),
        "w_uk": w_uk.astype(bf), "w_uv": w_uv.astype(bf),
        "g_qnorm": pad_gain(g_qnorm), "g_knorm": pad_gain(g_knorm),
        "w_attn_out": w_ao.astype(bf), "w_out": w_out.astype(bf), "g_norm2": g_norm2.reshape(1, -1),
        "peer_wq": peer_wq.astype(bf), "kcat": kcat.astype(bf),
        "tab_u": tab_u, "tab_v": tab_v,
        "tab_u_sc": tab_u.reshape(-1, ROW_WORDS), "tab_v_sc": tab_v.reshape(-1, ROW_WORDS),
    }


def _rope_tables(seq):
    pos = jnp.arange(seq, dtype=jnp.float32)
    inv = ROPE_THETA ** (-jnp.arange(0, QK_ROPE, 2, dtype=jnp.float32) / QK_ROPE)
    ang = pos[:, None] * inv[None, :]
    cos, sin = jnp.cos(ang), jnp.sin(ang)
    t1 = slice(QK_NOPE, QK_NOPE + HALF_ROPE)
    t2 = slice(QK_NOPE + HALF_ROPE, QK_HEAD)
    cos_t = jnp.ones((seq, HEAD_SLOT), jnp.float32).at[:, t1].set(cos).at[:, t2].set(cos)
    sin_a = jnp.zeros((seq, HEAD_SLOT), jnp.float32).at[:, t1].set(-sin)
    sin_b = jnp.zeros((seq, HEAD_SLOT), jnp.float32).at[:, t2].set(sin)
    return cos_t, sin_a, sin_b


def _front(x, c, w_ada, b_ada, wts, after):
    bsz, seq, d = x.shape
    n = bsz * seq
    ada = _ada(c, w_ada, b_ada)[:, None, :]
    shift1, scale1, gate1, shift2, scale2, gate2 = jnp.split(ada, 6, axis=-1)
    q, k, v, a, sgb = _inproj(x, scale1, shift1, wts, _rope_tables(seq), after)
    o = _attention(q, k, v)
    x1, h2, e, g = _mix(x, a, sgb, o, gate1, scale2, shift2, wts)
    return dict(x1=x1.reshape(n, d), h2=h2.reshape(n, d), e=e.reshape(n, N_PICKS),
                g=g.reshape(n, N_PICKS), gate2=gate2, seq=seq)


def _peer_on_sparsecores(f, wts):
    ids = lax.shift_right_logical(f["e"], ROW_SUB.bit_length() - 1)
    peer = _peer_sc(ids.reshape(-1, 2, SC_HALF), f["g"], f["h2"], wts["tab_u_sc"], wts["tab_v_sc"])
    return _peer_finish(f["x1"], f["gate2"], peer, f["seq"])


def _peer_on_tensorcore(f, wts):
    e = f["e"].reshape(-1)
    w = _peer_u(e, f["h2"], f["g"], wts["tab_u"])
    return _peer_v(e, w, f["x1"], f["gate2"], wts["tab_v"], f["seq"])


def kernel(x_prompt, x_sample, c_prompt, c_sample, w_ada, b_ada, g_norm1, w_in, conv_w, w_conv_out, g_q_lora, w_uq, g_kv_lora, w_ukv, g_qnorm, g_knorm, w_attn_out, w_out, g_norm2, peer_wq, peer_k1, peer_k2, peer_u, peer_v):
    wts = _prep_weights(g_norm1[0], w_in[0], conv_w[0], w_conv_out[0], g_q_lora[0], w_uq[0],
                        g_kv_lora[0], w_ukv[0], g_qnorm[0], g_knorm[0], w_attn_out[0], w_out[0],
                        g_norm2[0], peer_wq[0], peer_k1[0], peer_k2[0], peer_u[0], peer_v[0])
    sc_tables = (wts["tab_u_sc"], wts["tab_v_sc"])
    fs = _front(x_sample, c_sample, w_ada[0], b_ada[0], wts, sc_tables)
    y_sample = _peer_on_sparsecores(fs, wts)
    fp = _front(x_prompt, c_prompt, w_ada[0], b_ada[0], wts, sc_tables + (fs["g"],))
    y_prompt = _peer_on_tensorcore(fp, wts)
    return y_prompt.reshape(x_prompt.shape), y_sample.reshape(x_sample.shape)
```
